```python
import math
import jax, jax.numpy as jnp
from jax import lax
import numpy as np

D_MODEL = 2048
BATCH = 8
SEQ = 4096
DEPTH = 2

CHUNK = 64

N_ATT_HEADS = 8
ATT_HEAD_DIM = 128
ATT_V_DIM = 2 * ATT_HEAD_DIM
ATT_WIDTH = N_ATT_HEADS * ATT_V_DIM
Q_BLOCK = 128

REL_BUCKETS = 32
REL_MAX_DIST = 128

SSD_EXPAND = 2
SSD_WIDTH = SSD_EXPAND * D_MODEL
SSD_HEAD_DIM = 64
SSD_HEADS = SSD_WIDTH // SSD_HEAD_DIM
SSD_GROUPS = 8
SSD_HEADS_PER_GROUP = SSD_HEADS // SSD_GROUPS
SSD_STATE = 128
SSD_CONV = 4
SSD_CONV_CH = SSD_WIDTH + 2 * SSD_GROUPS * SSD_STATE

Q_COLS = 2 * N_ATT_HEADS * ATT_HEAD_DIM
K_COLS = 2 * N_ATT_HEADS * ATT_HEAD_DIM
V_COLS = ATT_WIDTH
Z_COLS = SSD_WIDTH
XBC_COLS = SSD_CONV_CH
DT_COLS = SSD_HEADS
GATE_COLS = 2 * D_MODEL
IN_COLS = Q_COLS + K_COLS + V_COLS + Z_COLS + XBC_COLS + DT_COLS + GATE_COLS
IN_SPLITS = [Q_COLS,
             Q_COLS + K_COLS,
             Q_COLS + K_COLS + V_COLS,
             Q_COLS + K_COLS + V_COLS + Z_COLS,
             Q_COLS + K_COLS + V_COLS + Z_COLS + XBC_COLS,
             Q_COLS + K_COLS + V_COLS + Z_COLS + XBC_COLS + DT_COLS]

N_EXPERTS = 32
TOP_K = 4
EXPERT_FF = 3 * D_MODEL // 4
SWIGLU_LIMIT = 7.0
SWIGLU_ALPHA = 1.702
EXPERT_BLOCK = 256

NORM_EPS = 1e-6
SUBLN_EPS = 1e-5

kernel_name = "hybrid_diffattn_ssd_moe_adaln_chunk_causal"


def rms_norm(x, g, eps=NORM_EPS):
    xf = x.astype(jnp.float32)
    y = xf * lax.rsqrt(jnp.mean(xf * xf, axis=-1, keepdims=True) + eps)
    return (y * g.astype(jnp.float32)).astype(x.dtype)


def t5_bucket(rel):
    half = REL_BUCKETS // 2
    exact = half // 2
    ret = jnp.where(rel > 0, half, 0)
    n = jnp.abs(rel)
    nf = jnp.maximum(n, 1).astype(jnp.float32)
    large = exact + (jnp.log(nf / exact) / math.log(REL_MAX_DIST / exact) * (half - exact)).astype(jnp.int32)
    large = jnp.minimum(large, half - 1)
    return ret + jnp.where(n < exact, n, large)


def diff_attention(q, k, v, lam, lam_init, subln_g, rel_bias):
    b, s = q.shape[0], q.shape[1]
    nblk = s // Q_BLOCK
    scale = ATT_HEAD_DIM ** -0.5
    k_pos = jnp.arange(s)
    k_chunk = k_pos // CHUNK
    q_blocks = q.reshape(b, nblk, Q_BLOCK, 2, N_ATT_HEADS, ATT_HEAD_DIM).swapaxes(0, 1)

    def block(args):
        q_blk, i = args
        q_pos = i * Q_BLOCK + jnp.arange(Q_BLOCK)
        logits = jnp.einsum("bqmhd,bkmhd->bmhqk", q_blk, k).astype(jnp.float32) * scale
        bias = rel_bias[t5_bucket(k_pos[None, :] - q_pos[:, None])]
        bias = jnp.transpose(bias, (2, 0, 1)).astype(jnp.float32)
        visible = k_chunk[None, :] <= (q_pos // CHUNK)[:, None]
        logits = jnp.where(visible, logits + bias, -jnp.inf)
        probs = jax.nn.softmax(logits, axis=-1)
        diff = probs[:, 0] - lam * probs[:, 1]
        return jnp.einsum("bhqk,bkhd->bqhd", diff.astype(v.dtype), v)

    out = lax.map(block, (q_blocks, jnp.arange(nblk)))
    out = out.swapaxes(0, 1).reshape(b, s, N_ATT_HEADS, ATT_V_DIM)
    out = rms_norm(out, subln_g, eps=SUBLN_EPS) * (1.0 - lam_init)
    return out.reshape(b, s, ATT_WIDTH)


def causal_depthwise_conv(x, w, bias):
    y = lax.conv_general_dilated(
        x, w[:, None, :].astype(x.dtype), window_strides=(1,), padding=[(SSD_CONV - 1, 0)],
        dimension_numbers=("NWC", "WIO", "NWC"), feature_group_count=x.shape[-1])
    return y + bias.astype(x.dtype)


def ssd_chunk_scan(xdt, da, bm, cm):
    b, s = xdt.shape[0], xdt.shape[1]
    g, r, p = xdt.shape[2], xdt.shape[3], xdt.shape[4]
    n = bm.shape[-1]
    nc = s // CHUNK

    def chunks(t):
        return t.reshape((b, nc, CHUNK) + t.shape[2:]).swapaxes(0, 1)

    causal = jnp.tril(jnp.ones((CHUNK, CHUNK), dtype=bool))[None, :, :, None, None]

    def step(state, inp):
        x_c, a_c, b_c, c_c = inp
        a_cum = jnp.cumsum(a_c, axis=1)
        seg = a_cum[:, :, None] - a_cum[:, None, :]
        decay = jnp.exp(jnp.where(causal, seg, -jnp.inf))
        cb = jnp.einsum("blgn,bsgn->blsg", c_c, b_c)
        y = jnp.einsum("blsg,blsgr,bsgrp->blgrp", cb, decay, x_c)
        y = y + jnp.einsum("blgn,bgrpn,blgr->blgrp", c_c, state, jnp.exp(a_cum))
        a_last = a_cum[:, -1]
        to_end = jnp.exp(a_last[:, None] - a_cum)
        state = state * jnp.exp(a_last)[..., None, None] + jnp.einsum(
            "bsgn,bsgr,bsgrp->bgrpn", b_c, to_end, x_c)
        return state, y

    state0 = jnp.zeros((b, g, r, p, n), jnp.float32)
    _, y = lax.scan(step, state0, (chunks(xdt), chunks(da), chunks(bm), chunks(cm)))
    return y.swapaxes(0, 1).reshape(b, s, g, r, p)


def ssd_mixer(z, xbc, dt_raw, conv_w, conv_b, dt_bias, a_log, d_skip, norm_g):
    b, s = z.shape[0], z.shape[1]
    xbc = jax.nn.silu(causal_depthwise_conv(xbc, conv_w, conv_b))
    xs, bm, cm = jnp.split(xbc, [SSD_WIDTH, SSD_WIDTH + SSD_GROUPS * SSD_STATE], axis=-1)
    xs = xs.reshape(b, s, SSD_GROUPS, SSD_HEADS_PER_GROUP, SSD_HEAD_DIM).astype(jnp.float32)
    bm = bm.reshape(b, s, SSD_GROUPS, SSD_STATE).astype(jnp.float32)
    cm = cm.reshape(b, s, SSD_GROUPS, SSD_STATE).astype(jnp.float32)
    dt = jax.nn.softplus(dt_raw.astype(jnp.float32) + dt_bias.astype(jnp.float32))
    dt = dt.reshape(b, s, SSD_GROUPS, SSD_HEADS_PER_GROUP)
    a = -jnp.exp(a_log.astype(jnp.float32)).reshape(SSD_GROUPS, SSD_HEADS_PER_GROUP)
    y = ssd_chunk_scan(xs * dt[..., None], dt * a, bm, cm)
    y = y + d_skip.astype(jnp.float32).reshape(SSD_GROUPS, SSD_HEADS_PER_GROUP)[:, :, None] * xs
    y = y.reshape(b, s, SSD_WIDTH) * jax.nn.silu(z.astype(jnp.float32))
    yg = y.reshape(b, s, SSD_GROUPS, SSD_WIDTH // SSD_GROUPS)
    yg = yg * lax.rsqrt(jnp.mean(yg * yg, axis=-1, keepdims=True) + SUBLN_EPS)
    return (yg.reshape(b, s, SSD_WIDTH) * norm_g.astype(jnp.float32)).astype(z.dtype)


def moe_ffn(h, router_w, router_b, w_gate_up, b_gate_up, w_down, b_down):
    b, s, d = h.shape
    t = b * s
    hf = h.reshape(t, d)
    logits = (hf @ router_w + router_b).astype(jnp.float32)
    top_v, top_i = lax.top_k(logits, TOP_K)
    gates = jax.nn.softmax(top_v, axis=-1)
    n_assign = t * TOP_K
    flat_e = top_i.reshape(-1)
    order = jnp.argsort(flat_e)
    sorted_e = flat_e[order]
    sorted_tok = order // TOP_K
    sorted_gate = gates.reshape(-1)[order]
    counts = jnp.bincount(flat_e, length=N_EXPERTS)
    padded = (counts + EXPERT_BLOCK - 1) // EXPERT_BLOCK * EXPERT_BLOCK
    pad_end = jnp.cumsum(padded)
    pad_start = pad_end - padded
    start = jnp.cumsum(counts) - counts
    dest = pad_start[sorted_e] + jnp.arange(n_assign) - start[sorted_e]
    n_blocks = n_assign // EXPERT_BLOCK + N_EXPERTS
    row_tok = jnp.zeros((n_blocks * EXPERT_BLOCK,), jnp.int32).at[dest].set(sorted_tok)
    row_gate = jnp.zeros((n_blocks * EXPERT_BLOCK,), jnp.float32).at[dest].set(sorted_gate)
    block_e = jnp.minimum(
        jnp.searchsorted(pad_end, jnp.arange(n_blocks) * EXPERT_BLOCK, side="right"), N_EXPERTS - 1)

    def expert_block(args):
        tok, e = args
        xb = hf[tok]
        gu = xb @ w_gate_up[e] + b_gate_up[e]
        g, u = jnp.split(gu, 2, axis=-1)
        g = jnp.minimum(g, SWIGLU_LIMIT)
        u = jnp.clip(u, -SWIGLU_LIMIT, SWIGLU_LIMIT)
        act = (u + 1.0) * (g * jax.nn.sigmoid(SWIGLU_ALPHA * g))
        return act @ w_down[e] + b_down[e]

    ys = lax.map(expert_block, (row_tok.reshape(n_blocks, EXPERT_BLOCK), block_e))
    ys = ys.reshape(-1, d).astype(jnp.float32) * row_gate[:, None]
    out = jnp.zeros((t, d), jnp.float32).at[row_tok].add(ys)
    return out.reshape(b, s, d).astype(h.dtype)


def setup_inputs(seed: int = 0) -> dict:
    key = jax.random.key(seed)
    ks = jax.random.split(key, 32)
    L, D, E, F = DEPTH, D_MODEL, N_EXPERTS, EXPERT_FF

    def nrm(k, shape, s):
        return jax.random.normal(k, shape, jnp.float32) * s

    x = nrm(ks[0], (BATCH, SEQ, D), 1.0)
    c = nrm(ks[1], (BATCH, D), 1.0)
    ada_w = nrm(ks[2], (L, D, 6 * D), 0.5 * D ** -0.5)
    ada_b = nrm(ks[3], (L, 6 * D), 0.02)
    norm1_g = 1.0 + nrm(ks[4], (L, D), 0.02)
    w_in = nrm(ks[5], (L, D, IN_COLS), D ** -0.5)
    rel_bias = nrm(ks[6], (REL_BUCKETS, N_ATT_HEADS), 0.2)
    lam_q1 = nrm(ks[7], (L, ATT_HEAD_DIM), 0.1)
    lam_k1 = nrm(ks[8], (L, ATT_HEAD_DIM), 0.1)
    lam_q2 = nrm(ks[9], (L, ATT_HEAD_DIM), 0.1)
    lam_k2 = nrm(ks[10], (L, ATT_HEAD_DIM), 0.1)
    attn_subln_g = 1.0 + nrm(ks[11], (L, ATT_V_DIM), 0.02)
    conv_w = nrm(ks[12], (L, SSD_CONV, SSD_CONV_CH), SSD_CONV ** -0.5)
    conv_b = nrm(ks[13], (L, SSD_CONV_CH), 0.02)
    dt0 = jnp.exp(jax.random.uniform(ks[14], (L, SSD_HEADS), jnp.float32)
                  * (math.log(0.1) - math.log(0.001)) + math.log(0.001))
    dt_bias = dt0 + jnp.log(-jnp.expm1(-dt0))
    a_log = jnp.log(jax.random.uniform(ks[15], (L, SSD_HEADS), jnp.float32, minval=1.0, maxval=16.0))
    d_skip = 1.0 + nrm(ks[16], (L, SSD_HEADS), 0.1)
    ssd_norm_g = 1.0 + nrm(ks[17], (L, SSD_WIDTH), 0.02)
    row_scale = jnp.concatenate([jnp.full((ATT_WIDTH,), ATT_WIDTH ** -0.5, jnp.float32),
                                 jnp.full((SSD_WIDTH,), SSD_WIDTH ** -0.5, jnp.float32)])
    w_branch = nrm(ks[18], (L, ATT_WIDTH + SSD_WIDTH, D), 1.0) * row_scale[None, :, None]
    w_o = nrm(ks[19], (L, D, D), D ** -0.5)
    norm2_g = 1.0 + nrm(ks[20], (L, D), 0.02)
    router_w = nrm(ks[21], (L, D, E), D ** -0.5)
    router_b = nrm(ks[22], (L, E), 0.01)
    w_gate_up = nrm(ks[23], (L, E, D, 2 * F), D ** -0.5)
    b_gate_up = nrm(ks[24], (L, E, 2 * F), 0.02)
    w_down = nrm(ks[25], (L, E, F, D), F ** -0.5)
    b_down = nrm(ks[26], (L, E, D), 0.02)
    final_g = 1.0 + nrm(ks[27], (D,), 0.02)
    return {"x": x, "c": c, "ada_w": ada_w, "ada_b": ada_b, "norm1_g": norm1_g, "w_in": w_in,
            "rel_bias": rel_bias, "lam_q1": lam_q1, "lam_k1": lam_k1, "lam_q2": lam_q2,
            "lam_k2": lam_k2, "attn_subln_g": attn_subln_g, "conv_w": conv_w, "conv_b": conv_b,
            "dt_bias": dt_bias, "a_log": a_log, "d_skip": d_skip, "ssd_norm_g": ssd_norm_g,
            "w_branch": w_branch, "w_o": w_o, "norm2_g": norm2_g, "router_w": router_w,
            "router_b": router_b, "w_gate_up": w_gate_up, "b_gate_up": b_gate_up,
            "w_down": w_down, "b_down": b_down, "final_g": final_g}


def reference(x, c, ada_w, ada_b, norm1_g, w_in, rel_bias, lam_q1, lam_k1, lam_q2, lam_k2,
              attn_subln_g, conv_w, conv_b, dt_bias, a_log, d_skip, ssd_norm_g, w_branch, w_o,
              norm2_g, router_w, router_b, w_gate_up, b_gate_up, w_down, b_down, final_g):
    b, s, _ = x.shape
    c_act = jax.nn.silu(c)
    for layer in range(DEPTH):
        mod = (c_act @ ada_w[layer] + ada_b[layer])[:, None, :]
        sh1, sc1, g1, sh2, sc2, g2 = jnp.split(mod, 6, axis=-1)

        h = rms_norm(x, norm1_g[layer]) * (1.0 + sc1) + sh1
        proj = h @ w_in[layer]
        q, k, v, z, xbc, dt_raw, gates = jnp.split(proj, IN_SPLITS, axis=-1)
        q = q.reshape(b, s, 2, N_ATT_HEADS, ATT_HEAD_DIM)
        k = k.reshape(b, s, 2, N_ATT_HEADS, ATT_HEAD_DIM)
        v = v.reshape(b, s, N_ATT_HEADS, ATT_V_DIM)
        lam_init = 0.8 - 0.6 * math.exp(-0.3 * layer)
        lam = (jnp.exp(jnp.sum(lam_q1[layer] * lam_k1[layer]).astype(jnp.float32))
               - jnp.exp(jnp.sum(lam_q2[layer] * lam_k2[layer]).astype(jnp.float32)) + lam_init)
        o_att = diff_attention(q, k, v, lam, lam_init, attn_subln_g[layer], rel_bias)
        o_ssd = ssd_mixer(z, xbc, dt_raw, conv_w[layer], conv_b[layer], dt_bias[layer],
                          a_log[layer], d_skip[layer], ssd_norm_g[layer])
        y_att = o_att @ w_branch[layer, :ATT_WIDTH]
        y_ssd = o_ssd @ w_branch[layer, ATT_WIDTH:]
        gate_att, gate_ssd = jnp.split(gates, 2, axis=-1)
        merged = jax.nn.sigmoid(gate_att) * y_att + jax.nn.sigmoid(gate_ssd) * y_ssd
        x = x + g1 * (merged @ w_o[layer])

        h2 = rms_norm(x, norm2_g[layer]) * (1.0 + sc2) + sh2
        x = x + g2 * moe_ffn(h2, router_w[layer], router_b[layer], w_gate_up[layer],
                             b_gate_up[layer], w_down[layer], b_down[layer])
    return rms_norm(x, final_g)
```

```python
import functools
import math

import numpy as np
import jax
import jax.numpy as jnp
from jax import lax
from jax.experimental import pallas as pl
from jax.experimental.pallas import tpu as pltpu

F32 = jnp.float32
BF16 = jnp.bfloat16
HIGHEST = lax.Precision.HIGHEST

DEPTH = 2
CHUNK = 64
N_ATT_HEADS = 8
ATT_HEAD_DIM = 128
ATT_V_DIM = 2 * ATT_HEAD_DIM
REL_BUCKETS = 32
REL_MAX_DIST = 128
SSD_HEAD_DIM = 64
SSD_GROUPS = 8
SSD_HEADS_PER_GROUP = 8
SSD_STATE = 128
SSD_CONV = 4
N_EXPERTS = 32
TOP_K = 4
SWIGLU_LIMIT = 7.0
SWIGLU_ALPHA = 1.702
NORM_EPS = 1e-6
SUBLN_EPS = 1e-5

LANES = 128
SUBLANES = 8
VMEM_LIMIT = 56 * 1024 * 1024

ATT_BLOCK = 256
SSD_BLOCK = 256
EXPERT_BLOCK = 256
GROUP_W = SSD_HEADS_PER_GROUP * SSD_HEAD_DIM


def _params(semantics):
    return pltpu.CompilerParams(dimension_semantics=semantics, vmem_limit_bytes=VMEM_LIMIT)


def _silu(v):
    return v * jax.nn.sigmoid(v)


def _adaln_kernel(c_ref, w_ref, b_ref, o_ref):
    ca = _silu(c_ref[...])
    o_ref[...] = jnp.dot(ca, w_ref[...], precision=HIGHEST, preferred_element_type=F32) + b_ref[...]


def _adaln(c, ada_w, ada_b):
    nl, d, n = ada_w.shape
    b = c.shape[0]
    tn = 1024
    return pl.pallas_call(
        _adaln_kernel,
        grid=(nl, n // tn),
        in_specs=[pl.BlockSpec((b, d), lambda l, j: (0, 0)),
                  pl.BlockSpec((None, d, tn), lambda l, j: (l, 0, j)),
                  pl.BlockSpec((None, 1, tn), lambda l, j: (l, 0, j))],
        out_specs=pl.BlockSpec((None, b, tn), lambda l, j: (l, 0, j)),
        out_shape=jax.ShapeDtypeStruct((nl, b, n), F32),
        compiler_params=_params(("parallel", "parallel")),
        name="adaln",
    )(c, ada_w, ada_b.reshape(nl, 1, n))


def _normmod_matmul_kernel(x_ref, g_ref, sc_ref, sh_ref, w_ref, o_ref, h_ref):
    @pl.when(pl.program_id(1) == 0)
    def _():
        x = x_ref[...]
        y = x * lax.rsqrt(jnp.mean(x * x, axis=-1, keepdims=True) + NORM_EPS) * g_ref[...]
        h_ref[...] = (y * (1.0 + sc_ref[...]) + sh_ref[...]).astype(BF16)

    o_ref[...] = jnp.dot(h_ref[...], w_ref[...], preferred_element_type=F32).astype(o_ref.dtype)


def _normmod_matmul(xf, g, mod_l, sc_idx, sh_idx, w, out_dtype, seq, tm, tn):
    t, d = xf.shape
    n = w.shape[1]
    tpb = seq // tm
    return pl.pallas_call(
        _normmod_matmul_kernel,
        grid=(t // tm, n // tn),
        in_specs=[pl.BlockSpec((tm, d), lambda i, j: (i, 0)),
                  pl.BlockSpec((1, d), lambda i, j: (0, 0)),
                  pl.BlockSpec((None, 1, d), lambda i, j: (i // tpb, 0, sc_idx)),
                  pl.BlockSpec((None, 1, d), lambda i, j: (i // tpb, 0, sh_idx)),
                  pl.BlockSpec((d, tn), lambda i, j: (0, j))],
        out_specs=pl.BlockSpec((tm, tn), lambda i, j: (i, j)),
        out_shape=jax.ShapeDtypeStruct((t, n), out_dtype),
        scratch_shapes=[pltpu.VMEM((tm, d), BF16)],
        compiler_params=_params(("parallel", "arbitrary")),
        name="inproj",
    )(xf, g.reshape(1, d), mod_l, mod_l, w)


def _t5_bucket(rel):
    half = REL_BUCKETS // 2
    exact = half // 2
    ret = jnp.where(rel > 0, half, 0)
    n = jnp.abs(rel)
    nf = jnp.maximum(n, 1).astype(jnp.float32)
    large = exact + (jnp.log(nf / exact) / math.log(REL_MAX_DIST / exact) * (half - exact)).astype(jnp.int32)
    large = jnp.minimum(large, half - 1)
    return ret + jnp.where(n < exact, n, large)


def _attn_bias_tiles(rel_bias, seq, blk):
    rel = jnp.arange(-(2 * blk - 1), blk, dtype=jnp.int32)
    vec = rel_bias[_t5_bucket(rel)].astype(F32)
    qq = np.arange(blk)[:, None]
    kk = np.arange(blk)[None, :]
    diag_idx = (kk - qq) + (2 * blk - 1)
    prev_idx = (kk - qq - blk) + (2 * blk - 1)
    visible = (kk // CHUNK) <= (qq // CHUNK)
    diag = jnp.where(visible[:, :, None], vec[diag_idx], -jnp.inf)
    prev = vec[prev_idx]
    tiles = jnp.stack([prev, diag], axis=0)
    far = rel_bias[_t5_bucket(jnp.full((1,), -(seq - 1), jnp.int32))][0].astype(F32)
    return jnp.transpose(tiles, (3, 0, 1, 2)), far


def _attn_kernel(far_ref, q1_ref, q2_ref, k1_ref, k2_ref, v_ref, bias_ref, lam_ref, g_ref, o_ref,
                 m1_ref, l1_ref, acc1_ref, m2_ref, l2_ref, acc2_ref, *, blk, scale, lam_init):
    h = pl.program_id(1)
    i = pl.program_id(2)
    far = far_ref[h]

    for m_ref, l_ref, acc_ref in ((m1_ref, l1_ref, acc1_ref), (m2_ref, l2_ref, acc2_ref)):
        m_ref[...] = jnp.full(m_ref.shape, -jnp.inf, F32)
        l_ref[...] = jnp.zeros(l_ref.shape, F32)
        acc_ref[...] = jnp.zeros(acc_ref.shape, F32)

    def update(q_ref, k_ref, m_ref, l_ref, acc_ref, ks, bias):
        k = k_ref[pl.ds(ks, blk), :]
        s = lax.dot_general(q_ref[...], k, (((1,), (1,)), ((), ())), preferred_element_type=F32)
        s = s * scale + bias
        m_prev = m_ref[...]
        m_next = jnp.maximum(m_prev, jnp.max(s, axis=-1, keepdims=True))
        alpha = jnp.exp(m_prev - m_next)
        p = jnp.exp(s - m_next[:, :1])
        l_ref[...] = alpha * l_ref[...] + jnp.sum(p, axis=-1, keepdims=True)
        m_ref[...] = m_next
        pv = jnp.dot(p.astype(BF16), v_ref[pl.ds(ks, blk), :], preferred_element_type=F32)
        acc_ref[...] = acc_ref[...] * alpha[:, :1] + pv

    def both(ks, bias):
        update(q1_ref, k1_ref, m1_ref, l1_ref, acc1_ref, ks, bias)
        update(q2_ref, k2_ref, m2_ref, l2_ref, acc2_ref, ks, bias)

    def far_body(kb, carry):
        both(pl.multiple_of(kb * blk, blk), far)
        return carry

    def prev_body(kb, carry):
        both(pl.multiple_of(kb * blk, blk), bias_ref[0])
        return carry

    n_far = jnp.maximum(i - 1, 0)
    lax.fori_loop(0, n_far, far_body, 0)
    lax.fori_loop(n_far, i, prev_body, 0)
    both(pl.multiple_of(i * blk, blk), bias_ref[1])

    lamv = lam_ref[...]
    lam = (jnp.exp(jnp.sum(lamv[0:1] * lamv[1:2], axis=-1, keepdims=True))
           - jnp.exp(jnp.sum(lamv[2:3] * lamv[3:4], axis=-1, keepdims=True)) + lam_init)
    o = acc1_ref[...] / l1_ref[...][:, :1] - lam * (acc2_ref[...] / l2_ref[...][:, :1])
    y = o * lax.rsqrt(jnp.mean(o * o, axis=-1, keepdims=True) + SUBLN_EPS) * g_ref[...]
    o_ref[...] = (y * (1.0 - lam_init)).astype(o_ref.dtype)


def _attention(proj, bias_tiles, far, lamv, subln_g, batch, seq, lam_init):
    blk = ATT_BLOCK
    nq = seq // blk
    nh = N_ATT_HEADS
    dk, dv = ATT_HEAD_DIM, ATT_V_DIM
    t = batch * seq
    k_off = 2 * nh
    v_off = 4 * nh * dk // dv
    kern = functools.partial(_attn_kernel, blk=blk, scale=dk ** -0.5, lam_init=lam_init)
    return pl.pallas_call(
        kern,
        grid=(batch, nh, nq),
        in_specs=[pl.BlockSpec(memory_space=pltpu.SMEM),
                  pl.BlockSpec((blk, dk), lambda b, h, i: (b * nq + i, h)),
                  pl.BlockSpec((blk, dk), lambda b, h, i: (b * nq + i, nh + h)),
                  pl.BlockSpec((seq, dk), lambda b, h, i: (b, k_off + h)),
                  pl.BlockSpec((seq, dk), lambda b, h, i: (b, k_off + nh + h)),
                  pl.BlockSpec((seq, dv), lambda b, h, i: (b, v_off + h)),
                  pl.BlockSpec((None, 2, blk, blk), lambda b, h, i: (h, 0, 0, 0)),
                  pl.BlockSpec((4, dk), lambda b, h, i: (0, 0)),
                  pl.BlockSpec((1, dv), lambda b, h, i: (0, 0))],
        out_specs=pl.BlockSpec((blk, dv), lambda b, h, i: (b * nq + i, h)),
        out_shape=jax.ShapeDtypeStruct((t, nh * dv), BF16),
        scratch_shapes=[pltpu.VMEM((blk, LANES), F32), pltpu.VMEM((blk, LANES), F32), pltpu.VMEM((blk, dv), F32),
                        pltpu.VMEM((blk, LANES), F32), pltpu.VMEM((blk, LANES), F32), pltpu.VMEM((blk, dv), F32)],
        compiler_params=_params(("parallel", "parallel", "parallel")),
        name="diff_attention",
    )(far, proj, proj, proj, proj, proj, bias_tiles, lamv, subln_g.reshape(1, dv))


def _conv_silu(x_ref, tail_ref, w_ref, b_ref, blk):
    x = x_ref[...].astype(F32)
    width = x.shape[1]
    tail = tail_ref[...]
    w = w_ref[...]
    row = lax.broadcasted_iota(jnp.int32, (SUBLANES, width), 0)
    acc = x * w[SSD_CONV - 1:SSD_CONV] + b_ref[...]
    for j in range(1, SSD_CONV):
        sh = pltpu.roll(x, j, 0)
        top = jnp.where(row < j, pltpu.roll(tail, j, 0), sh[:SUBLANES])
        sh = jnp.concatenate([top, sh[SUBLANES:]], axis=0)
        acc = acc + sh * w[SSD_CONV - 1 - j:SSD_CONV - j]
    tail_ref[...] = x[blk - SUBLANES:]
    return _silu(acc)


def _ssd_kernel(xs_ref, bm_ref, cm_ref, z_ref, dt_ref, e_ref, cwx_ref, cwb_ref, cwc_ref, cbx_ref, cbb_ref,
                cbc_ref, dtb_ref, alog_ref, dskip_ref, ng_ref, o_ref,
                state_ref, tx_ref, tb_ref, tc_ref, act_ref, *, blk):
    g = pl.program_id(1)

    @pl.when(pl.program_id(2) == 0)
    def _():
        state_ref[...] = jnp.zeros(state_ref.shape, F32)
        tx_ref[...] = jnp.zeros(tx_ref.shape, F32)
        tb_ref[...] = jnp.zeros(tb_ref.shape, F32)
        tc_ref[...] = jnp.zeros(tc_ref.shape, F32)

    xs = _conv_silu(xs_ref, tx_ref, cwx_ref, cbx_ref, blk)
    bm = _conv_silu(bm_ref, tb_ref, cwb_ref, cbb_ref, blk)
    cm = _conv_silu(cm_ref, tc_ref, cwc_ref, cbc_ref, blk)

    dt = jax.nn.softplus(dt_ref[...] + dtb_ref[...])
    da = dt * (-jnp.exp(alog_ref[...]))
    row = lax.broadcasted_iota(jnp.int32, (blk, blk), 0)
    col = lax.broadcasted_iota(jnp.int32, (blk, blk), 1)
    causal = col <= row
    acum = jnp.dot(causal.astype(F32), da, precision=HIGHEST, preferred_element_type=F32)
    expand = e_ref[...]
    dt_e = jnp.dot(dt, expand, precision=HIGHEST, preferred_element_type=F32)
    acum_e = jnp.dot(acum, expand, precision=HIGHEST, preferred_element_type=F32)
    act_ref[...] = acum.T

    xdt = xs * dt_e
    xdt_b = xdt.astype(BF16)
    cm_b = cm.astype(BF16)
    cb = lax.dot_general(cm_b, bm.astype(BF16), (((1,), (1,)), ((), ())), preferred_element_type=F32)
    lane = lax.broadcasted_iota(jnp.int32, (blk, LANES), 1)
    halves = (lane < SSD_HEAD_DIM, lane >= SSD_HEAD_DIM)
    pieces = []
    for pair in range(GROUP_W // LANES):
        xp = xdt_b[:, pair * LANES:(pair + 1) * LANES]
        yp = jnp.zeros((blk, LANES), F32)
        for hh in range(2):
            r = 2 * pair + hh
            a_col = acum_e[:, r * SSD_HEAD_DIM:r * SSD_HEAD_DIM + 1]
            a_row = act_ref[pl.ds(g * SSD_HEADS_PER_GROUP + r, 1), :]
            decay = jnp.exp(jnp.where(causal, a_col - a_row, -jnp.inf))
            mat = (cb * decay).astype(BF16)
            yp = yp + jnp.dot(mat, jnp.where(halves[hh], xp, jnp.zeros_like(xp)), preferred_element_type=F32)
        pieces.append(yp)
    y = jnp.concatenate(pieces, axis=1)

    st = state_ref[...]
    y = y + jnp.exp(acum_e) * jnp.dot(cm_b, st.astype(BF16), preferred_element_type=F32)
    a_last = acum_e[blk - 1:blk, :]
    wgt = (xdt * jnp.exp(a_last - acum_e)).astype(BF16)
    state_ref[...] = st * jnp.exp(a_last) + jnp.dot(bm.T.astype(BF16), wgt, preferred_element_type=F32)

    y = y + dskip_ref[...] * xs
    y = y * _silu(z_ref[...].astype(F32))
    y = y * lax.rsqrt(jnp.mean(y * y, axis=-1, keepdims=True) + SUBLN_EPS)
    o_ref[...] = (y * ng_ref[...]).astype(o_ref.dtype)


def _ssd(proj, dt_raw, expand, conv_w, conv_b, dt_bias, a_log, d_skip, norm_g, batch, seq, z_col, xbc_col):
    blk = SSD_BLOCK
    nc = seq // blk
    ng = SSD_GROUPS
    t = batch * seq
    width = ng * GROUP_W
    heads = ng * SSD_HEADS_PER_GROUP
    z_blk = z_col // GROUP_W
    xs_blk = xbc_col // GROUP_W
    b_blk = (xbc_col + width) // SSD_STATE
    c_blk = b_blk + ng
    cw_b_blk = width // SSD_STATE
    pad = LANES - heads
    row = lambda b, g, c: b * nc + c
    dtb = jnp.pad(dt_bias, (0, pad)).reshape(1, LANES)
    alog = jnp.pad(a_log, (0, pad)).reshape(1, LANES)
    dskip = jnp.repeat(d_skip, SSD_HEAD_DIM).reshape(1, width)
    cb2 = conv_b.reshape(1, -1)
    kern = functools.partial(_ssd_kernel, blk=blk)
    return pl.pallas_call(
        kern,
        grid=(batch, ng, nc),
        in_specs=[pl.BlockSpec((blk, GROUP_W), lambda b, g, c: (row(b, g, c), xs_blk + g)),
                  pl.BlockSpec((blk, SSD_STATE), lambda b, g, c: (row(b, g, c), b_blk + g)),
                  pl.BlockSpec((blk, SSD_STATE), lambda b, g, c: (row(b, g, c), c_blk + g)),
                  pl.BlockSpec((blk, GROUP_W), lambda b, g, c: (row(b, g, c), z_blk + g)),
                  pl.BlockSpec((blk, LANES), lambda b, g, c: (row(b, g, c), 0)),
                  pl.BlockSpec((None, LANES, GROUP_W), lambda b, g, c: (g, 0, 0)),
                  pl.BlockSpec((SSD_CONV, GROUP_W), lambda b, g, c: (0, g)),
                  pl.BlockSpec((SSD_CONV, SSD_STATE), lambda b, g, c: (0, cw_b_blk + g)),
                  pl.BlockSpec((SSD_CONV, SSD_STATE), lambda b, g, c: (0, cw_b_blk + ng + g)),
                  pl.BlockSpec((1, GROUP_W), lambda b, g, c: (0, g)),
                  pl.BlockSpec((1, SSD_STATE), lambda b, g, c: (0, cw_b_blk + g)),
                  pl.BlockSpec((1, SSD_STATE), lambda b, g, c: (0, cw_b_blk + ng + g)),
                  pl.BlockSpec((1, LANES), lambda b, g, c: (0, 0)),
                  pl.BlockSpec((1, LANES), lambda b, g, c: (0, 0)),
                  pl.BlockSpec((1, GROUP_W), lambda b, g, c: (0, g)),
                  pl.BlockSpec((1, GROUP_W), lambda b, g, c: (0, g))],
        out_specs=pl.BlockSpec((blk, GROUP_W), lambda b, g, c: (row(b, g, c), g)),
        out_shape=jax.ShapeDtypeStruct((t, width), BF16),
        scratch_shapes=[pltpu.VMEM((SSD_STATE, GROUP_W), F32),
                        pltpu.VMEM((SUBLANES, GROUP_W), F32),
                        pltpu.VMEM((SUBLANES, SSD_STATE), F32),
                        pltpu.VMEM((SUBLANES, SSD_STATE), F32),
                        pltpu.VMEM((LANES, blk), F32)],
        compiler_params=_params(("parallel", "parallel", "arbitrary")),
        name="ssd",
    )(proj, proj, proj, proj, dt_raw, expand, conv_w, conv_w, conv_w, cb2, cb2, cb2, dtb, alog, dskip,
      norm_g.reshape(1, width))


def _head_expand():
    e = np.zeros((SSD_GROUPS, LANES, GROUP_W), np.float32)
    for g in range(SSD_GROUPS):
        for r in range(SSD_HEADS_PER_GROUP):
            e[g, g * SSD_HEADS_PER_GROUP + r, r * SSD_HEAD_DIM:(r + 1) * SSD_HEAD_DIM] = 1.0
    return jnp.asarray(e)


def _merge_kernel(oa_ref, os_ref, wa_ref, ws_ref, ga_ref, gs_ref, o_ref):
    ya = jnp.dot(oa_ref[...], wa_ref[...], preferred_element_type=F32)
    ys = jnp.dot(os_ref[...], ws_ref[...], preferred_element_type=F32)
    merged = jax.nn.sigmoid(ga_ref[...].astype(F32)) * ya + jax.nn.sigmoid(gs_ref[...].astype(F32)) * ys
    o_ref[...] = merged.astype(o_ref.dtype)


def _merge(o_att, o_ssd, proj, wa, ws, gate_col, tm, tn):
    t, ka = o_att.shape
    ks = o_ssd.shape[1]
    d = wa.shape[1]
    ga_blk = gate_col // tn
    gs_blk = (gate_col + d) // tn
    return pl.pallas_call(
        _merge_kernel,
        grid=(t // tm, d // tn),
        in_specs=[pl.BlockSpec((tm, ka), lambda i, j: (i, 0)),
                  pl.BlockSpec((tm, ks), lambda i, j: (i, 0)),
                  pl.BlockSpec((ka, tn), lambda i, j: (0, j)),
                  pl.BlockSpec((ks, tn), lambda i, j: (0, j)),
                  pl.BlockSpec((tm, tn), lambda i, j: (i, ga_blk + j)),
                  pl.BlockSpec((tm, tn), lambda i, j: (i, gs_blk + j))],
        out_specs=pl.BlockSpec((tm, tn), lambda i, j: (i, j)),
        out_shape=jax.ShapeDtypeStruct((t, d), BF16),
        compiler_params=_params(("parallel", "arbitrary")),
        name="branch_merge",
    )(o_att, o_ssd, wa, ws, proj, proj)


def _wo_kernel(m_ref, x_ref, w_ref, g1_ref, ng_ref, sc_ref, sh_ref, xo_ref, h_ref):
    y = jnp.dot(m_ref[...], w_ref[...], preferred_element_type=F32)
    x = x_ref[...] + g1_ref[...] * y
    xo_ref[...] = x
    hn = x * lax.rsqrt(jnp.mean(x * x, axis=-1, keepdims=True) + NORM_EPS) * ng_ref[...]
    h_ref[...] = hn * (1.0 + sc_ref[...]) + sh_ref[...]


def _wo_residual(merged, xf, w_o, mod_l, norm2_g, seq, tm):
    t, d = xf.shape
    tpb = seq // tm
    mod_spec = lambda idx: pl.BlockSpec((None, 1, d), lambda i: (i // tpb, 0, idx))
    return pl.pallas_call(
        _wo_kernel,
        grid=(t // tm,),
        in_specs=[pl.BlockSpec((tm, d), lambda i: (i, 0)),
                  pl.BlockSpec((tm, d), lambda i: (i, 0)),
                  pl.BlockSpec((d, d), lambda i: (0, 0)),
                  mod_spec(2),
                  pl.BlockSpec((1, d), lambda i: (0, 0)),
                  mod_spec(4), mod_spec(3)],
        out_specs=[pl.BlockSpec((tm, d), lambda i: (i, 0)), pl.BlockSpec((tm, d), lambda i: (i, 0))],
        out_shape=[jax.ShapeDtypeStruct((t, d), F32), jax.ShapeDtypeStruct((t, d), F32)],
        compiler_params=_params(("parallel",)),
        name="wo_residual",
    )(merged, xf, w_o, mod_l, norm2_g.reshape(1, d), mod_l, mod_l)


def _router_kernel(h_ref, w_ref, b_ref, idx_ref, gate_ref, rank_ref, cnt_ref, run_ref, *, tm):
    @pl.when(pl.program_id(0) == 0)
    def _():
        run_ref[...] = jnp.zeros(run_ref.shape, F32)

    logits = jnp.dot(h_ref[...], w_ref[...], precision=HIGHEST, preferred_element_type=F32) + b_ref[...]
    lane = lax.broadcasted_iota(jnp.int32, (tm, LANES), 1)
    vals = logits
    picked = jnp.zeros((tm, LANES), F32)
    top_v, top_sel, top_i = [], [], []
    for _ in range(TOP_K):
        m = jnp.max(vals, axis=-1, keepdims=True)
        idx = jnp.min(jnp.where(vals == m, lane, LANES), axis=-1, keepdims=True)
        sel = lane == idx
        top_v.append(m)
        top_i.append(idx)
        top_sel.append(sel)
        vals = jnp.where(sel, -jnp.inf, vals)
        picked = picked + sel.astype(F32)

    row = lax.broadcasted_iota(jnp.int32, (tm, tm), 0)
    col = lax.broadcasted_iota(jnp.int32, (tm, tm), 1)
    before = jnp.dot((col < row).astype(BF16), picked.astype(BF16), preferred_element_type=F32) + run_ref[...]
    run_ref[...] = run_ref[...] + jnp.sum(picked, axis=0, keepdims=True)
    cnt_ref[...] = run_ref[...]

    exps = [jnp.exp(v - top_v[0]) for v in top_v]
    denom = exps[0] + exps[1] + exps[2] + exps[3]
    idx_out = jnp.zeros((tm, LANES), jnp.int32)
    rank_out = jnp.zeros((tm, LANES), jnp.int32)
    gate_out = jnp.zeros((tm, LANES), F32)
    for k in range(TOP_K):
        rank_k = jnp.sum(jnp.where(top_sel[k], before, 0.0), axis=-1, keepdims=True).astype(jnp.int32)
        idx_out = jnp.where(lane == k, top_i[k], idx_out)
        rank_out = jnp.where(lane == k, rank_k, rank_out)
        gate_out = jnp.where(lane == k, exps[k] / denom, gate_out)
    idx_ref[...] = idx_out
    rank_ref[...] = rank_out
    gate_ref[...] = gate_out


def _router(h2, router_w, router_b, tm):
    t, d = h2.shape
    e = router_w.shape[1]
    w = jnp.pad(router_w, ((0, 0), (0, LANES - e)))
    b = jnp.pad(router_b, (0, LANES - e), constant_values=-jnp.inf).reshape(1, LANES)
    kern = functools.partial(_router_kernel, tm=tm)
    tok_spec = pl.BlockSpec((tm, LANES), lambda i: (i, 0))
    return pl.pallas_call(
        kern,
        grid=(t // tm,),
        in_specs=[pl.BlockSpec((tm, d), lambda i: (i, 0)),
                  pl.BlockSpec((d, LANES), lambda i: (0, 0)),
                  pl.BlockSpec((1, LANES), lambda i: (0, 0))],
        out_specs=[tok_spec, tok_spec, tok_spec, pl.BlockSpec((1, LANES), lambda i: (0, 0))],
        out_shape=[jax.ShapeDtypeStruct((t, LANES), jnp.int32), jax.ShapeDtypeStruct((t, LANES), F32),
                   jax.ShapeDtypeStruct((t, LANES), jnp.int32), jax.ShapeDtypeStruct((1, LANES), F32)],
        scratch_shapes=[pltpu.VMEM((1, LANES), F32)],
        compiler_params=_params(("arbitrary",)),
        name="router",
    )(h2, w, b)


def _dispatch_kernel(dest_ref, h_hbm, xs_in_hbm, xs_hbm, sem, *, tm):
    del xs_in_hbm
    base = pl.program_id(0) * tm

    def row_copy(t, k):
        return pltpu.make_async_copy(h_hbm.at[pl.ds(base + t, 1)],
                                     xs_hbm.at[pl.ds(dest_ref[t * TOP_K + k], 1)], sem)

    def issue(t, carry):
        for k in range(TOP_K):
            row_copy(t, k).start()
        return carry

    def drain(t, carry):
        for k in range(TOP_K):
            row_copy(t, k).wait()
        return carry

    lax.fori_loop(0, tm, issue, 0)
    lax.fori_loop(0, tm, drain, 0)


def _dispatch(h2, dest_flat, n_rows, tm):
    t, d = h2.shape
    kern = functools.partial(_dispatch_kernel, tm=tm)
    return pl.pallas_call(
        kern,
        grid=(t // tm,),
        in_specs=[pl.BlockSpec((tm * TOP_K,), lambda i: (i,), memory_space=pltpu.SMEM),
                  pl.BlockSpec(memory_space=pl.ANY),
                  pl.BlockSpec(memory_space=pl.ANY)],
        out_specs=pl.BlockSpec(memory_space=pl.ANY),
        out_shape=jax.ShapeDtypeStruct((n_rows, d), h2.dtype),
        scratch_shapes=[pltpu.SemaphoreType.DMA(())],
        input_output_aliases={2: 0},
        compiler_params=_params(("arbitrary",)),
        name="moe_dispatch",
    )(dest_flat, h2, jnp.zeros((n_rows, d), h2.dtype))


def _expert_gu_kernel(be_ref, nv_ref, x_ref, w_ref, b_ref, o_ref, *, ff):
    del be_ref
    valid = pl.program_id(0) < nv_ref[0]

    @pl.when(valid)
    def _():
        gu = jnp.dot(x_ref[...].astype(BF16), w_ref[...], preferred_element_type=F32) + b_ref[...]
        g = jnp.minimum(gu[:, :ff], SWIGLU_LIMIT)
        u = jnp.clip(gu[:, ff:], -SWIGLU_LIMIT, SWIGLU_LIMIT)
        o_ref[...] = ((u + 1.0) * (g * jax.nn.sigmoid(SWIGLU_ALPHA * g))).astype(o_ref.dtype)

    @pl.when(jnp.logical_not(valid))
    def _():
        o_ref[...] = jnp.zeros(o_ref.shape, o_ref.dtype)


def _expert_down_kernel(be_ref, nv_ref, a_ref, w_ref, b_ref, o_ref):
    del be_ref
    valid = pl.program_id(0) < nv_ref[0]

    @pl.when(valid)
    def _():
        o_ref[...] = jnp.dot(a_ref[...], w_ref[...], preferred_element_type=F32) + b_ref[...]

    @pl.when(jnp.logical_not(valid))
    def _():
        o_ref[...] = jnp.zeros(o_ref.shape, o_ref.dtype)


def _expert_ffn(xs, block_e, n_valid, w_gu, b_gu, w_dn, b_dn):
    n_rows, d = xs.shape
    bm = EXPERT_BLOCK
    n_blocks = n_rows // bm
    ne, _, ff2 = w_gu.shape
    ff = ff2 // 2
    act = pl.pallas_call(
        functools.partial(_expert_gu_kernel, ff=ff),
        grid_spec=pltpu.PrefetchScalarGridSpec(
            num_scalar_prefetch=2, grid=(n_blocks,),
            in_specs=[pl.BlockSpec((bm, d), lambda i, be, nv: (i, 0)),
                      pl.BlockSpec((None, d, ff2), lambda i, be, nv: (be[i], 0, 0)),
                      pl.BlockSpec((None, 1, ff2), lambda i, be, nv: (be[i], 0, 0))],
            out_specs=pl.BlockSpec((bm, ff), lambda i, be, nv: (i, 0))),
        out_shape=jax.ShapeDtypeStruct((n_rows, ff), BF16),
        compiler_params=_params(("arbitrary",)),
        name="expert_gate_up",
    )(block_e, n_valid, xs, w_gu, b_gu.reshape(ne, 1, ff2))
    return pl.pallas_call(
        _expert_down_kernel,
        grid_spec=pltpu.PrefetchScalarGridSpec(
            num_scalar_prefetch=2, grid=(n_blocks,),
            in_specs=[pl.BlockSpec((bm, ff), lambda i, be, nv: (i, 0)),
                      pl.BlockSpec((None, ff, d), lambda i, be, nv: (be[i], 0, 0)),
                      pl.BlockSpec((None, 1, d), lambda i, be, nv: (be[i], 0, 0))],
            out_specs=pl.BlockSpec((bm, d), lambda i, be, nv: (i, 0))),
        out_shape=jax.ShapeDtypeStruct((n_rows, d), F32),
        compiler_params=_params(("arbitrary",)),
        name="expert_down",
    )(block_e, n_valid, act, w_dn, b_dn.reshape(ne, 1, d))


def _combine_kernel(dest_ref, ys_hbm, x_ref, gate_ref, g2_ref, fg_ref, o_ref, buf, sem, *, tm, final):
    def row_copy(t, k):
        return pltpu.make_async_copy(ys_hbm.at[pl.ds(dest_ref[t * TOP_K + k], 1)],
                                     buf.at[k, pl.ds(t, 1)], sem)

    def issue(t, carry):
        for k in range(TOP_K):
            row_copy(t, k).start()
        return carry

    def drain(t, carry):
        for k in range(TOP_K):
            row_copy(t, k).wait()
        return carry

    lax.fori_loop(0, tm, issue, 0)
    lax.fori_loop(0, tm, drain, 0)

    gates = gate_ref[...]
    moe = gates[:, 0:1] * buf[0]
    for k in range(1, TOP_K):
        moe = moe + gates[:, k:k + 1] * buf[k]
    x = x_ref[...] + g2_ref[...] * moe
    if final:
        x = x * lax.rsqrt(jnp.mean(x * x, axis=-1, keepdims=True) + NORM_EPS) * fg_ref[...]
    o_ref[...] = x


def _combine(ys, dest_flat, xf, gates, mod_l, final_g, seq, tm, final):
    t, d = xf.shape
    tpb = seq // tm
    kern = functools.partial(_combine_kernel, tm=tm, final=final)
    return pl.pallas_call(
        kern,
        grid=(t // tm,),
        in_specs=[pl.BlockSpec((tm * TOP_K,), lambda i: (i,), memory_space=pltpu.SMEM),
                  pl.BlockSpec(memory_space=pl.ANY),
                  pl.BlockSpec((tm, d), lambda i: (i, 0)),
                  pl.BlockSpec((tm, LANES), lambda i: (i, 0)),
                  pl.BlockSpec((None, 1, d), lambda i: (i // tpb, 0, 5)),
                  pl.BlockSpec((1, d), lambda i: (0, 0))],
        out_specs=pl.BlockSpec((tm, d), lambda i: (i, 0)),
        out_shape=jax.ShapeDtypeStruct((t, d), F32),
        scratch_shapes=[pltpu.VMEM((TOP_K, tm, d), F32), pltpu.SemaphoreType.DMA(())],
        compiler_params=_params(("arbitrary",)),
        name="moe_combine",
    )(dest_flat, ys, xf, gates, mod_l, final_g.reshape(1, d))


def _moe(h2, xf, mod_l, router_w, router_b, w_gu, b_gu, w_dn, b_dn, final_g, seq, final):
    t, d = h2.shape
    ne = router_w.shape[1]
    bm = EXPERT_BLOCK
    top_i, gates, rank, counts = _router(h2, router_w, router_b, tm=256)
    cnt = counts[0, :ne].astype(jnp.int32)
    padded = (cnt + bm - 1) // bm * bm
    pad_end = jnp.cumsum(padded)
    pad_start = pad_end - padded
    dest = pad_start[top_i[:, :TOP_K]] + rank[:, :TOP_K]
    n_blocks = t * TOP_K // bm + ne
    block_e = jnp.minimum(jnp.searchsorted(pad_end, jnp.arange(n_blocks, dtype=jnp.int32) * bm, side="right"),
                          ne - 1).astype(jnp.int32)
    n_valid = (pad_end[-1:] // bm).astype(jnp.int32)
    dest_flat = dest.reshape(-1).astype(jnp.int32)

    xs = _dispatch(h2, dest_flat, n_blocks * bm, tm=256)
    ys = _expert_ffn(xs, block_e, n_valid, w_gu, b_gu, w_dn, b_dn)
    return _combine(ys, dest_flat, xf, gates, mod_l, final_g, seq, tm=128, final=final)


def kernel(x, c, ada_w, ada_b, norm1_g, w_in, rel_bias, lam_q1, lam_k1, lam_q2, lam_k2, attn_subln_g, conv_w, conv_b, dt_bias, a_log, d_skip, ssd_norm_g, w_branch, w_o, norm2_g, router_w, router_b, w_gate_up, b_gate_up, w_down, b_down, final_g):
    batch, seq, d = x.shape
    t = batch * seq
    depth = ada_w.shape[0]
    att_w = N_ATT_HEADS * ATT_V_DIM
    ssd_w = SSD_GROUPS * GROUP_W
    n_heads_ssd = SSD_GROUPS * SSD_HEADS_PER_GROUP
    conv_ch = conv_w.shape[2]
    z_col = 2 * (2 * N_ATT_HEADS * ATT_HEAD_DIM) + att_w
    xbc_col = z_col + ssd_w
    dt_col = xbc_col + conv_ch
    gate_col = dt_col

    mod = _adaln(c, ada_w, ada_b)
    bias_tiles, far = _attn_bias_tiles(rel_bias, seq, ATT_BLOCK)
    expand = _head_expand()
    xf = x.reshape(t, d)
    tm_big = min(1024, seq)

    for layer in range(depth):
        mod_l = mod[layer].reshape(batch, 1, 6 * d)
        w_l = w_in[layer]
        w_main = jnp.concatenate([w_l[:, :dt_col], w_l[:, dt_col + n_heads_ssd:]], axis=1).astype(BF16)
        w_dt = jnp.pad(w_l[:, dt_col:dt_col + n_heads_ssd], ((0, 0), (0, LANES - n_heads_ssd))).astype(BF16)

        proj = _normmod_matmul(xf, norm1_g[layer], mod_l, 1, 0, w_main, BF16, seq, tm=tm_big, tn=512)
        dt_raw = _normmod_matmul(xf, norm1_g[layer], mod_l, 1, 0, w_dt, F32, seq, tm=tm_big, tn=LANES)

        lam_init = 0.8 - 0.6 * math.exp(-0.3 * layer)
        lamv = jnp.stack([lam_q1[layer], lam_k1[layer], lam_q2[layer], lam_k2[layer]], axis=0)
        o_att = _attention(proj, bias_tiles, far, lamv, attn_subln_g[layer], batch, seq, lam_init)
        o_ssd = _ssd(proj, dt_raw, expand, conv_w[layer], conv_b[layer], dt_bias[layer], a_log[layer],
                     d_skip[layer], ssd_norm_g[layer], batch, seq, z_col, xbc_col)

        wa = w_branch[layer, :att_w].astype(BF16)
        ws = w_branch[layer, att_w:].astype(BF16)
        merged = _merge(o_att, o_ssd, proj, wa, ws, gate_col, tm=tm_big, tn=256)
        xf, h2 = _wo_residual(merged, xf, w_o[layer].astype(BF16), mod_l, norm2_g[layer], seq, tm=256)

        xf = _moe(h2, xf, mod_l, router_w[layer], router_b[layer], w_gate_up[layer].astype(BF16),
                  b_gate_up[layer], w_down[layer].astype(BF16), b_down[layer], final_g, seq,
                  final=(layer == depth - 1))
    return xf.reshape(batch, seq, d)
```

```python
import functools
import math

import numpy as np
import jax
import jax.numpy as jnp
from jax import lax
from jax.experimental import pallas as pl
from jax.experimental.pallas import tpu as pltpu

F32 = jnp.float32
BF16 = jnp.bfloat16
HIGHEST = lax.Precision.HIGHEST

DEPTH = 2
CHUNK = 64
N_ATT_HEADS = 8
ATT_HEAD_DIM = 128
ATT_V_DIM = 2 * ATT_HEAD_DIM
REL_BUCKETS = 32
REL_MAX_DIST = 128
SSD_HEAD_DIM = 64
SSD_GROUPS = 8
SSD_HEADS_PER_GROUP = 8
SSD_STATE = 128
SSD_CONV = 4
N_EXPERTS = 32
TOP_K = 4
SWIGLU_LIMIT = 7.0
SWIGLU_ALPHA = 1.702
NORM_EPS = 1e-6
SUBLN_EPS = 1e-5

LANES = 128
SUBLANES = 8
VMEM_LIMIT = 56 * 1024 * 1024

ATT_Q_BLOCK = 512
ATT_KV_BLOCK = 256
SSD_BLOCK = 256
EXPERT_BLOCK = 256
GROUP_W = SSD_HEADS_PER_GROUP * SSD_HEAD_DIM


def _params(semantics):
    return pltpu.CompilerParams(dimension_semantics=semantics, vmem_limit_bytes=VMEM_LIMIT)


def _silu(v):
    return v * jax.nn.sigmoid(v)


def _split3(v):
    hi = v.astype(BF16)
    rest = v - hi.astype(F32)
    mid = rest.astype(BF16)
    lo = (rest - mid.astype(F32)).astype(BF16)
    return hi, mid, lo


def _adaln_kernel(c_ref, w_ref, b_ref, o_ref):
    ca = _silu(c_ref[...])
    o_ref[...] = jnp.dot(ca, w_ref[...], precision=HIGHEST, preferred_element_type=F32) + b_ref[...]


def _adaln(c, ada_w, ada_b):
    nl, d, n = ada_w.shape
    b = c.shape[0]
    tn = 1024
    return pl.pallas_call(
        _adaln_kernel,
        grid=(nl, n // tn),
        in_specs=[pl.BlockSpec((b, d), lambda l, j: (0, 0)),
                  pl.BlockSpec((None, d, tn), lambda l, j: (l, 0, j)),
                  pl.BlockSpec((None, 1, tn), lambda l, j: (l, 0, j))],
        out_specs=pl.BlockSpec((None, b, tn), lambda l, j: (l, 0, j)),
        out_shape=jax.ShapeDtypeStruct((nl, b, n), F32),
        compiler_params=_params(("parallel", "parallel")),
        name="adaln",
    )(c, ada_w, ada_b.reshape(nl, 1, n))


def _normmod_matmul_kernel(x_ref, g_ref, sc_ref, sh_ref, w_ref, o_ref, h_ref):
    @pl.when(pl.program_id(1) == 0)
    def _():
        x = x_ref[...]
        y = x * lax.rsqrt(jnp.mean(x * x, axis=-1, keepdims=True) + NORM_EPS) * g_ref[...]
        h_ref[...] = (y * (1.0 + sc_ref[...]) + sh_ref[...]).astype(BF16)

    o_ref[...] = jnp.dot(h_ref[...], w_ref[...], preferred_element_type=F32).astype(o_ref.dtype)


def _normmod_matmul(xf, g, mod_l, sc_idx, sh_idx, w, out_dtype, seq, tm, tn):
    t, d = xf.shape
    n = w.shape[1]
    tpb = seq // tm
    return pl.pallas_call(
        _normmod_matmul_kernel,
        grid=(t // tm, n // tn),
        in_specs=[pl.BlockSpec((tm, d), lambda i, j: (i, 0)),
                  pl.BlockSpec((1, d), lambda i, j: (0, 0)),
                  pl.BlockSpec((None, 1, d), lambda i, j: (i // tpb, 0, sc_idx)),
                  pl.BlockSpec((None, 1, d), lambda i, j: (i // tpb, 0, sh_idx)),
                  pl.BlockSpec((d, tn), lambda i, j: (0, j))],
        out_specs=pl.BlockSpec((tm, tn), lambda i, j: (i, j)),
        out_shape=jax.ShapeDtypeStruct((t, n), out_dtype),
        scratch_shapes=[pltpu.VMEM((tm, d), BF16)],
        compiler_params=_params(("parallel", "arbitrary")),
        name="inproj",
    )(xf, g.reshape(1, d), mod_l, mod_l, w)


def _t5_bucket(rel):
    half = REL_BUCKETS // 2
    exact = half // 2
    ret = jnp.where(rel > 0, half, 0)
    n = jnp.abs(rel)
    nf = jnp.maximum(n, 1).astype(jnp.float32)
    large = exact + (jnp.log(nf / exact) / math.log(REL_MAX_DIST / exact) * (half - exact)).astype(jnp.int32)
    large = jnp.minimum(large, half - 1)
    return ret + jnp.where(n < exact, n, large)


def _attn_bias_tiles(rel_bias, seq, qb, kb, scale):
    assert kb >= REL_MAX_DIST and kb % CHUNK == 0 and qb % kb == 0
    lo = kb + qb - 1
    rel = jnp.arange(-lo, qb, dtype=jnp.int32)
    vec = rel_bias[_t5_bucket(rel)].astype(F32) * (1.0 / scale)
    qq = np.arange(qb)[None, :]
    tiles = []
    for off in range(-1, qb // kb):
        kk = off * kb + np.arange(kb)[:, None]
        visible = (kk // CHUNK) <= (qq // CHUNK)
        tiles.append(jnp.where(visible[:, :, None], vec[kk - qq + lo], -jnp.inf))
    far = rel_bias[_t5_bucket(jnp.full((1,), -(seq - 1), jnp.int32))][0].astype(F32)
    return jnp.transpose(jnp.stack(tiles, axis=0), (3, 0, 1, 2)), far


def _attn_kernel(far_ref, q1_ref, q2_ref, k1_ref, k2_ref, v_ref, bias_ref, lam_ref, g_ref, o_ref,
                 vt_ref, acc1_ref, acc2_ref, *, qb, kb, scale, lam_init):
    h = pl.program_id(1)
    i = pl.program_id(2)
    far = far_ref[h] * (1.0 / scale)
    exp_scale = scale * math.log2(math.e)
    nt_dims = (((1,), (1,)), ((), ()))
    per_tile = qb // kb

    @pl.when(i == 0)
    def _():
        rr = lax.broadcasted_iota(jnp.int32, (v_ref.shape[1], v_ref.shape[1]), 0)
        cc = lax.broadcasted_iota(jnp.int32, (v_ref.shape[1], v_ref.shape[1]), 1)
        eye = jnp.where(rr == cc, 1.0, 0.0).astype(BF16)

        def tr_body(j, carry):
            vj = v_ref[pl.ds(pl.multiple_of(j * kb, kb), kb), :]
            vt_ref[j] = lax.dot_general(eye, vj, nt_dims, preferred_element_type=F32).astype(BF16)
            return carry

        lax.fori_loop(0, vt_ref.shape[0], tr_body, 0)

    acc1_ref[...] = jnp.zeros(acc1_ref.shape, F32)
    acc2_ref[...] = jnp.zeros(acc2_ref.shape, F32)

    def update(q_ref, k_ref, acc_ref, m, l, j, bias, shift):
        k = k_ref[pl.ds(pl.multiple_of(j * kb, kb), kb), :]
        s = lax.dot_general(k, q_ref[...], nt_dims, preferred_element_type=F32)
        if bias is not None:
            s = s + bias
        m_new = jnp.maximum(m, jnp.max(s, axis=0, keepdims=True) + shift)
        p = jnp.exp2((s - (m_new - shift)) * exp_scale)
        alpha = jnp.exp2((m - m_new) * exp_scale)
        l_new = alpha * l + jnp.sum(p, axis=0, keepdims=True)
        acc_ref[...] = acc_ref[...] * alpha + jnp.dot(vt_ref[j], p.astype(BF16), preferred_element_type=F32)
        return m_new, l_new

    def both(carry, j, bias, shift):
        m1, l1, m2, l2 = carry
        m1, l1 = update(q1_ref, k1_ref, acc1_ref, m1, l1, j, bias, shift)
        m2, l2 = update(q2_ref, k2_ref, acc2_ref, m2, l2, j, bias, shift)
        return m1, l1, m2, l2

    neg = jnp.full((1, qb), -jnp.inf, F32)
    zero = jnp.zeros((1, qb), F32)
    carry = (neg, zero, neg, zero)
    first = i * per_tile
    n_far = jnp.maximum(first - 1, 0)
    carry = lax.fori_loop(0, n_far, lambda j, c: both(c, j, None, far), carry)
    carry = lax.fori_loop(n_far, first, lambda j, c: both(c, j, bias_ref[0], 0.0), carry)
    for off in range(per_tile):
        carry = both(carry, first + off, bias_ref[off + 1], 0.0)
    _, l1, _, l2 = carry

    lamv = lam_ref[...]
    lam = (jnp.exp(jnp.sum(lamv[0:1] * lamv[1:2], axis=-1, keepdims=True))
           - jnp.exp(jnp.sum(lamv[2:3] * lamv[3:4], axis=-1, keepdims=True)) + lam_init)
    o = (acc1_ref[...] / l1 - lam * (acc2_ref[...] / l2)).T
    y = o * lax.rsqrt(jnp.mean(o * o, axis=-1, keepdims=True) + SUBLN_EPS) * g_ref[...]
    o_ref[...] = (y * (1.0 - lam_init)).astype(o_ref.dtype)


def _attention(proj, bias_tiles, far, lamv, subln_g, batch, seq, lam_init):
    qb, kb = ATT_Q_BLOCK, ATT_KV_BLOCK
    nq = seq // qb
    nh = N_ATT_HEADS
    dk, dv = ATT_HEAD_DIM, ATT_V_DIM
    t = batch * seq
    k_off = 2 * nh
    v_off = 4 * nh * dk // dv
    kern = functools.partial(_attn_kernel, qb=qb, kb=kb, scale=dk ** -0.5, lam_init=lam_init)
    return pl.pallas_call(
        kern,
        grid=(batch, nh, nq),
        in_specs=[pl.BlockSpec(memory_space=pltpu.SMEM),
                  pl.BlockSpec((qb, dk), lambda b, h, i: (b * nq + i, h)),
                  pl.BlockSpec((qb, dk), lambda b, h, i: (b * nq + i, nh + h)),
                  pl.BlockSpec((seq, dk), lambda b, h, i: (b, k_off + h)),
                  pl.BlockSpec((seq, dk), lambda b, h, i: (b, k_off + nh + h)),
                  pl.BlockSpec((seq, dv), lambda b, h, i: (b, v_off + h)),
                  pl.BlockSpec((None, qb // kb + 1, kb, qb), lambda b, h, i: (h, 0, 0, 0)),
                  pl.BlockSpec((4, dk), lambda b, h, i: (0, 0)),
                  pl.BlockSpec((1, dv), lambda b, h, i: (0, 0))],
        out_specs=pl.BlockSpec((qb, dv), lambda b, h, i: (b * nq + i, h)),
        out_shape=jax.ShapeDtypeStruct((t, nh * dv), BF16),
        scratch_shapes=[pltpu.VMEM((seq // kb, dv, kb), BF16), pltpu.VMEM((dv, qb), F32), pltpu.VMEM((dv, qb), F32)],
        compiler_params=_params(("parallel", "parallel", "arbitrary")),
        name="diff_attention",
    )(far, proj, proj, proj, proj, proj, bias_tiles, lamv, subln_g.reshape(1, dv))


def _conv_silu(x_ref, tail_ref, w_ref, b_ref, blk):
    x = x_ref[...].astype(F32)
    width = x.shape[1]
    tail = tail_ref[...]
    w = w_ref[...]
    row = lax.broadcasted_iota(jnp.int32, (SUBLANES, width), 0)
    acc = x * w[SSD_CONV - 1:SSD_CONV] + b_ref[...]
    for j in range(1, SSD_CONV):
        sh = pltpu.roll(x, j, 0)
        top = jnp.where(row < j, pltpu.roll(tail, j, 0), sh[:SUBLANES])
        sh = jnp.concatenate([top, sh[SUBLANES:]], axis=0)
        acc = acc + sh * w[SSD_CONV - 1 - j:SSD_CONV - j]
    tail_ref[...] = x[blk - SUBLANES:]
    return _silu(acc)


def _ssd_kernel(xs_ref, bm_ref, cm_ref, z_ref, dt_ref, e_ref, cwx_ref, cwb_ref, cwc_ref, cbx_ref, cbb_ref,
                cbc_ref, dtb_ref, alog_ref, dskip_ref, ng_ref, o_ref,
                state_ref, tx_ref, tb_ref, tc_ref, act_ref, *, blk):
    g = pl.program_id(1)

    @pl.when(pl.program_id(2) == 0)
    def _():
        state_ref[...] = jnp.zeros(state_ref.shape, F32)
        tx_ref[...] = jnp.zeros(tx_ref.shape, F32)
        tb_ref[...] = jnp.zeros(tb_ref.shape, F32)
        tc_ref[...] = jnp.zeros(tc_ref.shape, F32)

    xs = _conv_silu(xs_ref, tx_ref, cwx_ref, cbx_ref, blk)
    bm = _conv_silu(bm_ref, tb_ref, cwb_ref, cbb_ref, blk)
    cm = _conv_silu(cm_ref, tc_ref, cwc_ref, cbc_ref, blk)

    dt = jax.nn.softplus(dt_ref[...] + dtb_ref[...])
    da = dt * (-jnp.exp(alog_ref[...]))
    row = lax.broadcasted_iota(jnp.int32, (blk, blk), 0)
    col = lax.broadcasted_iota(jnp.int32, (blk, blk), 1)
    causal = col <= row
    tri = jnp.where(causal, 1.0, 0.0).astype(BF16)
    acum = sum(jnp.dot(tri, part, preferred_element_type=F32) for part in _split3(da))
    both_e = jnp.dot(jnp.concatenate([jnp.concatenate(_split3(dt), axis=1),
                                      jnp.concatenate(_split3(acum), axis=1)], axis=0),
                     e_ref[...], preferred_element_type=F32)
    dt_e = both_e[:blk]
    acum_e = both_e[blk:]
    act_ref[...] = acum.T

    xdt = xs * dt_e
    xdt_b = xdt.astype(BF16)
    cm_b = cm.astype(BF16)
    cb = lax.dot_general(cm_b, bm.astype(BF16), (((1,), (1,)), ((), ())), preferred_element_type=F32)
    lane = lax.broadcasted_iota(jnp.int32, (blk, LANES), 1)
    halves = (lane < SSD_HEAD_DIM, lane >= SSD_HEAD_DIM)
    pieces = []
    for pair in range(GROUP_W // LANES):
        xp = xdt_b[:, pair * LANES:(pair + 1) * LANES]
        yp = jnp.zeros((blk, LANES), F32)
        for hh in range(2):
            r = 2 * pair + hh
            a_col = acum_e[:, r * SSD_HEAD_DIM:r * SSD_HEAD_DIM + 1]
            a_row = act_ref[pl.ds(g * SSD_HEADS_PER_GROUP + r, 1), :]
            decay = jnp.exp(jnp.where(causal, a_col - a_row, -jnp.inf))
            mat = (cb * decay).astype(BF16)
            yp = yp + jnp.dot(mat, jnp.where(halves[hh], xp, jnp.zeros_like(xp)), preferred_element_type=F32)
        pieces.append(yp)
    y = jnp.concatenate(pieces, axis=1)

    st = state_ref[...]
    y = y + jnp.exp(acum_e) * jnp.dot(cm_b, st.astype(BF16), preferred_element_type=F32)
    a_last = acum_e[blk - 1:blk, :]
    wgt = (xdt * jnp.exp(a_last - acum_e)).astype(BF16)
    state_ref[...] = st * jnp.exp(a_last) + jnp.dot(bm.T.astype(BF16), wgt, preferred_element_type=F32)

    y = y + dskip_ref[...] * xs
    y = y * _silu(z_ref[...].astype(F32))
    y = y * lax.rsqrt(jnp.mean(y * y, axis=-1, keepdims=True) + SUBLN_EPS)
    o_ref[...] = (y * ng_ref[...]).astype(o_ref.dtype)


def _ssd(proj, dt_raw, expand, conv_w, conv_b, dt_bias, a_log, d_skip, norm_g, batch, seq, z_col, xbc_col):
    blk = SSD_BLOCK
    nc = seq // blk
    ng = SSD_GROUPS
    t = batch * seq
    width = ng * GROUP_W
    heads = ng * SSD_HEADS_PER_GROUP
    z_blk = z_col // GROUP_W
    xs_blk = xbc_col // GROUP_W
    b_blk = (xbc_col + width) // SSD_STATE
    c_blk = b_blk + ng
    cw_b_blk = width // SSD_STATE
    pad = LANES - heads
    row = lambda b, g, c: b * nc + c
    dtb = jnp.pad(dt_bias, (0, pad)).reshape(1, LANES)
    alog = jnp.pad(a_log, (0, pad)).reshape(1, LANES)
    dskip = jnp.repeat(d_skip, SSD_HEAD_DIM).reshape(1, width)
    cb2 = conv_b.reshape(1, -1)
    kern = functools.partial(_ssd_kernel, blk=blk)
    return pl.pallas_call(
        kern,
        grid=(batch, ng, nc),
        in_specs=[pl.BlockSpec((blk, GROUP_W), lambda b, g, c: (row(b, g, c), xs_blk + g)),
                  pl.BlockSpec((blk, SSD_STATE), lambda b, g, c: (row(b, g, c), b_blk + g)),
                  pl.BlockSpec((blk, SSD_STATE), lambda b, g, c: (row(b, g, c), c_blk + g)),
                  pl.BlockSpec((blk, GROUP_W), lambda b, g, c: (row(b, g, c), z_blk + g)),
                  pl.BlockSpec((blk, LANES), lambda b, g, c: (row(b, g, c), 0)),
                  pl.BlockSpec((None, 3 * LANES, GROUP_W), lambda b, g, c: (g, 0, 0)),
                  pl.BlockSpec((SSD_CONV, GROUP_W), lambda b, g, c: (0, g)),
                  pl.BlockSpec((SSD_CONV, SSD_STATE), lambda b, g, c: (0, cw_b_blk + g)),
                  pl.BlockSpec((SSD_CONV, SSD_STATE), lambda b, g, c: (0, cw_b_blk + ng + g)),
                  pl.BlockSpec((1, GROUP_W), lambda b, g, c: (0, g)),
                  pl.BlockSpec((1, SSD_STATE), lambda b, g, c: (0, cw_b_blk + g)),
                  pl.BlockSpec((1, SSD_STATE), lambda b, g, c: (0, cw_b_blk + ng + g)),
                  pl.BlockSpec((1, LANES), lambda b, g, c: (0, 0)),
                  pl.BlockSpec((1, LANES), lambda b, g, c: (0, 0)),
                  pl.BlockSpec((1, GROUP_W), lambda b, g, c: (0, g)),
                  pl.BlockSpec((1, GROUP_W), lambda b, g, c: (0, g))],
        out_specs=pl.BlockSpec((blk, GROUP_W), lambda b, g, c: (row(b, g, c), g)),
        out_shape=jax.ShapeDtypeStruct((t, width), BF16),
        scratch_shapes=[pltpu.VMEM((SSD_STATE, GROUP_W), F32),
                        pltpu.VMEM((SUBLANES, GROUP_W), F32),
                        pltpu.VMEM((SUBLANES, SSD_STATE), F32),
                        pltpu.VMEM((SUBLANES, SSD_STATE), F32),
                        pltpu.VMEM((LANES, blk), F32)],
        compiler_params=_params(("parallel", "parallel", "arbitrary")),
        name="ssd",
    )(proj, proj, proj, proj, dt_raw, expand, conv_w, conv_w, conv_w, cb2, cb2, cb2, dtb, alog, dskip,
      norm_g.reshape(1, width))


def _head_expand():
    e = np.zeros((SSD_GROUPS, LANES, GROUP_W), np.float32)
    for g in range(SSD_GROUPS):
        for r in range(SSD_HEADS_PER_GROUP):
            e[g, g * SSD_HEADS_PER_GROUP + r, r * SSD_HEAD_DIM:(r + 1) * SSD_HEAD_DIM] = 1.0
    return jnp.asarray(np.tile(e, (1, 3, 1)), dtype=BF16)


def _merge_kernel(oa_ref, os_ref, wa_ref, ws_ref, ga_ref, gs_ref, o_ref):
    ya = jnp.dot(oa_ref[...], wa_ref[...], preferred_element_type=F32)
    ys = jnp.dot(os_ref[...], ws_ref[...], preferred_element_type=F32)
    merged = jax.nn.sigmoid(ga_ref[...].astype(F32)) * ya + jax.nn.sigmoid(gs_ref[...].astype(F32)) * ys
    o_ref[...] = merged.astype(o_ref.dtype)


def _merge(o_att, o_ssd, proj, wa, ws, gate_col, tm, tn):
    t, ka = o_att.shape
    ks = o_ssd.shape[1]
    d = wa.shape[1]
    ga_blk = gate_col // tn
    gs_blk = (gate_col + d) // tn
    return pl.pallas_call(
        _merge_kernel,
        grid=(t // tm, d // tn),
        in_specs=[pl.BlockSpec((tm, ka), lambda i, j: (i, 0)),
                  pl.BlockSpec((tm, ks), lambda i, j: (i, 0)),
                  pl.BlockSpec((ka, tn), lambda i, j: (0, j)),
                  pl.BlockSpec((ks, tn), lambda i, j: (0, j)),
                  pl.BlockSpec((tm, tn), lambda i, j: (i, ga_blk + j)),
                  pl.BlockSpec((tm, tn), lambda i, j: (i, gs_blk + j))],
        out_specs=pl.BlockSpec((tm, tn), lambda i, j: (i, j)),
        out_shape=jax.ShapeDtypeStruct((t, d), BF16),
        compiler_params=_params(("parallel", "arbitrary")),
        name="branch_merge",
    )(o_att, o_ssd, wa, ws, proj, proj)


def _wo_kernel(m_ref, x_ref, w_ref, g1_ref, ng_ref, sc_ref, sh_ref, xo_ref, h_ref):
    y = jnp.dot(m_ref[...], w_ref[...], preferred_element_type=F32)
    x = x_ref[...] + g1_ref[...] * y
    xo_ref[...] = x
    hn = x * lax.rsqrt(jnp.mean(x * x, axis=-1, keepdims=True) + NORM_EPS) * ng_ref[...]
    h_ref[...] = hn * (1.0 + sc_ref[...]) + sh_ref[...]


def _wo_residual(merged, xf, w_o, mod_l, norm2_g, seq, tm):
    t, d = xf.shape
    tpb = seq // tm
    mod_spec = lambda idx: pl.BlockSpec((None, 1, d), lambda i: (i // tpb, 0, idx))
    return pl.pallas_call(
        _wo_kernel,
        grid=(t // tm,),
        in_specs=[pl.BlockSpec((tm, d), lambda i: (i, 0)),
                  pl.BlockSpec((tm, d), lambda i: (i, 0)),
                  pl.BlockSpec((d, d), lambda i: (0, 0)),
                  mod_spec(2),
                  pl.BlockSpec((1, d), lambda i: (0, 0)),
                  mod_spec(4), mod_spec(3)],
        out_specs=[pl.BlockSpec((tm, d), lambda i: (i, 0)), pl.BlockSpec((tm, d), lambda i: (i, 0))],
        out_shape=[jax.ShapeDtypeStruct((t, d), F32), jax.ShapeDtypeStruct((t, d), F32)],
        compiler_params=_params(("parallel",)),
        name="wo_residual",
    )(merged, xf, w_o, mod_l, norm2_g.reshape(1, d), mod_l, mod_l)


def _router_kernel(h_ref, w_ref, b_ref, idx_ref, gate_ref, rank_ref, cnt_ref, run_ref, *, tm):
    @pl.when(pl.program_id(0) == 0)
    def _():
        run_ref[...] = jnp.zeros(run_ref.shape, F32)

    logits = jnp.dot(h_ref[...], w_ref[...], precision=HIGHEST, preferred_element_type=F32) + b_ref[...]
    lane = lax.broadcasted_iota(jnp.int32, (tm, LANES), 1)
    vals = logits
    picked = jnp.zeros((tm, LANES), F32)
    top_v, top_sel, top_i = [], [], []
    for _ in range(TOP_K):
        m = jnp.max(vals, axis=-1, keepdims=True)
        idx = jnp.min(jnp.where(vals == m, lane, LANES), axis=-1, keepdims=True)
        sel = lane == idx
        top_v.append(m)
        top_i.append(idx)
        top_sel.append(sel)
        vals = jnp.where(sel, -jnp.inf, vals)
        picked = picked + sel.astype(F32)

    row = lax.broadcasted_iota(jnp.int32, (tm, tm), 0)
    col = lax.broadcasted_iota(jnp.int32, (tm, tm), 1)
    before = jnp.dot((col < row).astype(BF16), picked.astype(BF16), preferred_element_type=F32) + run_ref[...]
    run_ref[...] = run_ref[...] + jnp.sum(picked, axis=0, keepdims=True)
    cnt_ref[...] = run_ref[...]

    exps = [jnp.exp(v - top_v[0]) for v in top_v]
    denom = exps[0] + exps[1] + exps[2] + exps[3]
    idx_out = jnp.zeros((tm, LANES), jnp.int32)
    rank_out = jnp.zeros((tm, LANES), jnp.int32)
    gate_out = jnp.zeros((tm, LANES), F32)
    for k in range(TOP_K):
        rank_k = jnp.sum(jnp.where(top_sel[k], before, 0.0), axis=-1, keepdims=True).astype(jnp.int32)
        idx_out = jnp.where(lane == k, top_i[k], idx_out)
        rank_out = jnp.where(lane == k, rank_k, rank_out)
        gate_out = jnp.where(lane == k, exps[k] / denom, gate_out)
    idx_ref[...] = idx_out
    rank_ref[...] = rank_out
    gate_ref[...] = gate_out


def _router(h2, router_w, router_b, tm):
    t, d = h2.shape
    e = router_w.shape[1]
    w = jnp.pad(router_w, ((0, 0), (0, LANES - e)))
    b = jnp.pad(router_b, (0, LANES - e), constant_values=-jnp.inf).reshape(1, LANES)
    kern = functools.partial(_router_kernel, tm=tm)
    tok_spec = pl.BlockSpec((tm, LANES), lambda i: (i, 0))
    return pl.pallas_call(
        kern,
        grid=(t // tm,),
        in_specs=[pl.BlockSpec((tm, d), lambda i: (i, 0)),
                  pl.BlockSpec((d, LANES), lambda i: (0, 0)),
                  pl.BlockSpec((1, LANES), lambda i: (0, 0))],
        out_specs=[tok_spec, tok_spec, tok_spec, pl.BlockSpec((1, LANES), lambda i: (0, 0))],
        out_shape=[jax.ShapeDtypeStruct((t, LANES), jnp.int32), jax.ShapeDtypeStruct((t, LANES), F32),
                   jax.ShapeDtypeStruct((t, LANES), jnp.int32), jax.ShapeDtypeStruct((1, LANES), F32)],
        scratch_shapes=[pltpu.VMEM((1, LANES), F32)],
        compiler_params=_params(("arbitrary",)),
        name="router",
    )(h2, w, b)


def _dispatch_kernel(dest_ref, h_ref, xs_in_hbm, xs_hbm, sem, *, tm):
    del xs_in_hbm

    def row_copy(t, k):
        return pltpu.make_async_copy(h_ref.at[pl.ds(t, 1)],
                                     xs_hbm.at[pl.ds(dest_ref[t * TOP_K + k], 1)], sem)

    def issue(t, carry):
        for k in range(TOP_K):
            row_copy(t, k).start()
        return carry

    def drain(t, carry):
        for k in range(TOP_K):
            row_copy(t, k).wait()
        return carry

    lax.fori_loop(0, tm, issue, 0)
    lax.fori_loop(0, tm, drain, 0)


def _dispatch(h2, dest_flat, n_rows, tm):
    t, d = h2.shape
    kern = functools.partial(_dispatch_kernel, tm=tm)
    return pl.pallas_call(
        kern,
        grid=(t // tm,),
        in_specs=[pl.BlockSpec((tm * TOP_K,), lambda i: (i,), memory_space=pltpu.SMEM),
                  pl.BlockSpec((tm, d), lambda i: (i, 0)),
                  pl.BlockSpec(memory_space=pl.ANY)],
        out_specs=pl.BlockSpec(memory_space=pl.ANY),
        out_shape=jax.ShapeDtypeStruct((n_rows, d), h2.dtype),
        scratch_shapes=[pltpu.SemaphoreType.DMA(())],
        input_output_aliases={2: 0},
        compiler_params=_params(("arbitrary",)),
        name="moe_dispatch",
    )(dest_flat, h2, jnp.zeros((n_rows, d), h2.dtype))


def _expert_gu_kernel(be_ref, nv_ref, x_ref, w_ref, b_ref, o_ref, *, ff):
    del be_ref
    valid = pl.program_id(0) < nv_ref[0]

    @pl.when(valid)
    def _():
        gu = jnp.dot(x_ref[...].astype(BF16), w_ref[...], preferred_element_type=F32) + b_ref[...]
        g = jnp.minimum(gu[:, :ff], SWIGLU_LIMIT)
        u = jnp.clip(gu[:, ff:], -SWIGLU_LIMIT, SWIGLU_LIMIT)
        o_ref[...] = ((u + 1.0) * (g * jax.nn.sigmoid(SWIGLU_ALPHA * g))).astype(o_ref.dtype)

    @pl.when(jnp.logical_not(valid))
    def _():
        o_ref[...] = jnp.zeros(o_ref.shape, o_ref.dtype)


def _expert_down_kernel(be_ref, nv_ref, a_ref, w_ref, b_ref, o_ref):
    del be_ref
    valid = pl.program_id(0) < nv_ref[0]

    @pl.when(valid)
    def _():
        o_ref[...] = jnp.dot(a_ref[...], w_ref[...], preferred_element_type=F32) + b_ref[...]

    @pl.when(jnp.logical_not(valid))
    def _():
        o_ref[...] = jnp.zeros(o_ref.shape, o_ref.dtype)


def _expert_ffn(xs, block_e, n_valid, w_gu, b_gu, w_dn, b_dn):
    n_rows, d = xs.shape
    bm = EXPERT_BLOCK
    n_blocks = n_rows // bm
    ne, _, ff2 = w_gu.shape
    ff = ff2 // 2
    act = pl.pallas_call(
        functools.partial(_expert_gu_kernel, ff=ff),
        grid_spec=pltpu.PrefetchScalarGridSpec(
            num_scalar_prefetch=2, grid=(n_blocks,),
            in_specs=[pl.BlockSpec((bm, d), lambda i, be, nv: (i, 0)),
                      pl.BlockSpec((None, d, ff2), lambda i, be, nv: (be[i], 0, 0)),
                      pl.BlockSpec((None, 1, ff2), lambda i, be, nv: (be[i], 0, 0))],
            out_specs=pl.BlockSpec((bm, ff), lambda i, be, nv: (i, 0))),
        out_shape=jax.ShapeDtypeStruct((n_rows, ff), BF16),
        compiler_params=_params(("arbitrary",)),
        name="expert_gate_up",
    )(block_e, n_valid, xs, w_gu, b_gu.reshape(ne, 1, ff2))
    return pl.pallas_call(
        _expert_down_kernel,
        grid_spec=pltpu.PrefetchScalarGridSpec(
            num_scalar_prefetch=2, grid=(n_blocks,),
            in_specs=[pl.BlockSpec((bm, ff), lambda i, be, nv: (i, 0)),
                      pl.BlockSpec((None, ff, d), lambda i, be, nv: (be[i], 0, 0)),
                      pl.BlockSpec((None, 1, d), lambda i, be, nv: (be[i], 0, 0))],
            out_specs=pl.BlockSpec((bm, d), lambda i, be, nv: (i, 0))),
        out_shape=jax.ShapeDtypeStruct((n_rows, d), F32),
        compiler_params=_params(("arbitrary",)),
        name="expert_down",
    )(block_e, n_valid, act, w_dn, b_dn.reshape(ne, 1, d))


def _combine_kernel(dest_ref, ys_hbm, x_ref, gate_ref, g2_ref, fg_ref, o_ref, buf, sem, *, tm, final):
    def row_copy(t, k):
        return pltpu.make_async_copy(ys_hbm.at[pl.ds(dest_ref[t * TOP_K + k], 1)],
                                     buf.at[k, pl.ds(t, 1)], sem)

    def issue(t, carry):
        for k in range(TOP_K):
            row_copy(t, k).start()
        return carry

    def drain(t, carry):
        for k in range(TOP_K):
            row_copy(t, k).wait()
        return carry

    lax.fori_loop(0, tm, issue, 0)
    lax.fori_loop(0, tm, drain, 0)

    gates = gate_ref[...]
    moe = gates[:, 0:1] * buf[0]
    for k in range(1, TOP_K):
        moe = moe + gates[:, k:k + 1] * buf[k]
    x = x_ref[...] + g2_ref[...] * moe
    if final:
        x = x * lax.rsqrt(jnp.mean(x * x, axis=-1, keepdims=True) + NORM_EPS) * fg_ref[...]
    o_ref[...] = x


def _combine(ys, dest_flat, xf, gates, mod_l, final_g, seq, tm, final):
    t, d = xf.shape
    tpb = seq // tm
    kern = functools.partial(_combine_kernel, tm=tm, final=final)
    return pl.pallas_call(
        kern,
        grid=(t // tm,),
        in_specs=[pl.BlockSpec((tm * TOP_K,), lambda i: (i,), memory_space=pltpu.SMEM),
                  pl.BlockSpec(memory_space=pl.ANY),
                  pl.BlockSpec((tm, d), lambda i: (i, 0)),
                  pl.BlockSpec((tm, LANES), lambda i: (i, 0)),
                  pl.BlockSpec((None, 1, d), lambda i: (i // tpb, 0, 5)),
                  pl.BlockSpec((1, d), lambda i: (0, 0))],
        out_specs=pl.BlockSpec((tm, d), lambda i: (i, 0)),
        out_shape=jax.ShapeDtypeStruct((t, d), F32),
        scratch_shapes=[pltpu.VMEM((TOP_K, tm, d), F32), pltpu.SemaphoreType.DMA(())],
        compiler_params=_params(("arbitrary",)),
        name="moe_combine",
    )(dest_flat, ys, xf, gates, mod_l, final_g.reshape(1, d))


def _moe(h2, xf, mod_l, router_w, router_b, w_gu, b_gu, w_dn, b_dn, final_g, seq, final):
    t, d = h2.shape
    ne = router_w.shape[1]
    bm = EXPERT_BLOCK
    top_i, gates, rank, counts = _router(h2, router_w, router_b, tm=256)
    cnt = counts[0, :ne].astype(jnp.int32)
    padded = (cnt + bm - 1) // bm * bm
    pad_end = jnp.cumsum(padded)
    pad_start = pad_end - padded
    dest = pad_start[top_i[:, :TOP_K]] + rank[:, :TOP_K]
    n_blocks = t * TOP_K // bm + ne
    block_e = jnp.minimum(jnp.searchsorted(pad_end, jnp.arange(n_blocks, dtype=jnp.int32) * bm, side="right"),
                          ne - 1).astype(jnp.int32)
    n_valid = (pad_end[-1:] // bm).astype(jnp.int32)
    dest_flat = dest.reshape(-1).astype(jnp.int32)

    xs = _dispatch(h2, dest_flat, n_blocks * bm, tm=256)
    ys = _expert_ffn(xs, block_e, n_valid, w_gu, b_gu, w_dn, b_dn)
    return _combine(ys, dest_flat, xf, gates, mod_l, final_g, seq, tm=128, final=final)


def kernel(x, c, ada_w, ada_b, norm1_g, w_in, rel_bias, lam_q1, lam_k1, lam_q2, lam_k2, attn_subln_g, conv_w, conv_b, dt_bias, a_log, d_skip, ssd_norm_g, w_branch, w_o, norm2_g, router_w, router_b, w_gate_up, b_gate_up, w_down, b_down, final_g):
    batch, seq, d = x.shape
    t = batch * seq
    depth = ada_w.shape[0]
    att_w = N_ATT_HEADS * ATT_V_DIM
    ssd_w = SSD_GROUPS * GROUP_W
    n_heads_ssd = SSD_GROUPS * SSD_HEADS_PER_GROUP
    conv_ch = conv_w.shape[2]
    z_col = 2 * (2 * N_ATT_HEADS * ATT_HEAD_DIM) + att_w
    xbc_col = z_col + ssd_w
    dt_col = xbc_col + conv_ch
    gate_col = dt_col

    mod = _adaln(c, ada_w, ada_b)
    bias_tiles, far = _attn_bias_tiles(rel_bias, seq, ATT_Q_BLOCK, ATT_KV_BLOCK, ATT_HEAD_DIM ** -0.5)
    expand = _head_expand()
    xf = x.reshape(t, d)
    tm_big = min(1024, seq)

    for layer in range(depth):
        mod_l = mod[layer].reshape(batch, 1, 6 * d)
        w_l = w_in[layer]
        w_main = jnp.concatenate([w_l[:, :dt_col], w_l[:, dt_col + n_heads_ssd:]], axis=1).astype(BF16)
        w_dt = jnp.pad(w_l[:, dt_col:dt_col + n_heads_ssd], ((0, 0), (0, LANES - n_heads_ssd))).astype(BF16)

        proj = _normmod_matmul(xf, norm1_g[layer], mod_l, 1, 0, w_main, BF16, seq, tm=tm_big, tn=512)
        dt_raw = _normmod_matmul(xf, norm1_g[layer], mod_l, 1, 0, w_dt, F32, seq, tm=tm_big, tn=LANES)

        lam_init = 0.8 - 0.6 * math.exp(-0.3 * layer)
        lamv = jnp.stack([lam_q1[layer], lam_k1[layer], lam_q2[layer], lam_k2[layer]], axis=0)
        o_att = _attention(proj, bias_tiles, far, lamv, attn_subln_g[layer], batch, seq, lam_init)
        o_ssd = _ssd(proj, dt_raw, expand, conv_w[layer], conv_b[layer], dt_bias[layer], a_log[layer],
                     d_skip[layer], ssd_norm_g[layer], batch, seq, z_col, xbc_col)

        wa = w_branch[layer, :att_w].astype(BF16)
        ws = w_branch[layer, att_w:].astype(BF16)
        merged = _merge(o_att, o_ssd, proj, wa, ws, gate_col, tm=tm_big, tn=256)
        xf, h2 = _wo_residual(merged, xf, w_o[layer].astype(BF16), mod_l, norm2_g[layer], seq, tm=256)

        xf = _moe(h2, xf, mod_l, router_w[layer], router_b[layer], w_gate_up[layer].astype(BF16),
                  b_gate_up[layer], w_down[layer].astype(BF16), b_down[layer], final_g, seq,
                  final=(layer == depth - 1))
    return xf.reshape(batch, seq, d)
```

```python
import functools
import math

import numpy as np
import jax
import jax.numpy as jnp
from jax import lax
from jax.experimental import pallas as pl
from jax.experimental.pallas import tpu as pltpu

F32 = jnp.float32
BF16 = jnp.bfloat16
HIGHEST = lax.Precision.HIGHEST

DEPTH = 2
CHUNK = 64
N_ATT_HEADS = 8
ATT_HEAD_DIM = 128
ATT_V_DIM = 2 * ATT_HEAD_DIM
REL_BUCKETS = 32
REL_MAX_DIST = 128
SSD_HEAD_DIM = 64
SSD_GROUPS = 8
SSD_HEADS_PER_GROUP = 8
SSD_STATE = 128
SSD_CONV = 4
N_EXPERTS = 32
TOP_K = 4
SWIGLU_LIMIT = 7.0
SWIGLU_ALPHA = 1.702
NORM_EPS = 1e-6
SUBLN_EPS = 1e-5

LANES = 128
SUBLANES = 8
VMEM_LIMIT = 56 * 1024 * 1024

ATT_Q_BLOCK = 512
ATT_KV_BLOCK = 256
SSD_BLOCK = 256
EXPERT_BLOCK = 256
GROUP_W = SSD_HEADS_PER_GROUP * SSD_HEAD_DIM


def _params(semantics):
    return pltpu.CompilerParams(dimension_semantics=semantics, vmem_limit_bytes=VMEM_LIMIT)


def _silu(v):
    half = 0.5 * v
    return half + half * jnp.tanh(half)


def _split3(v):
    hi = v.astype(BF16)
    rest = v - hi.astype(F32)
    mid = rest.astype(BF16)
    lo = (rest - mid.astype(F32)).astype(BF16)
    return hi, mid, lo


def _adaln_kernel(c_ref, w_ref, b_ref, o_ref):
    ca = _silu(c_ref[...])
    o_ref[...] = jnp.dot(ca, w_ref[...], precision=HIGHEST, preferred_element_type=F32) + b_ref[...]


def _adaln(c, ada_w, ada_b):
    nl, d, n = ada_w.shape
    b = c.shape[0]
    tn = 1024
    return pl.pallas_call(
        _adaln_kernel,
        grid=(nl, n // tn),
        in_specs=[pl.BlockSpec((b, d), lambda l, j: (0, 0)),
                  pl.BlockSpec((None, d, tn), lambda l, j: (l, 0, j)),
                  pl.BlockSpec((None, 1, tn), lambda l, j: (l, 0, j))],
        out_specs=pl.BlockSpec((None, b, tn), lambda l, j: (l, 0, j)),
        out_shape=jax.ShapeDtypeStruct((nl, b, n), F32),
        compiler_params=_params(("parallel", "parallel")),
        name="adaln",
    )(c, ada_w, ada_b.reshape(nl, 1, n))


def _normmod_matmul_kernel(x_ref, g_ref, sc_ref, sh_ref, w_ref, o_ref, h_ref):
    @pl.when(pl.program_id(1) == 0)
    def _():
        x = x_ref[...]
        y = x * lax.rsqrt(jnp.mean(x * x, axis=-1, keepdims=True) + NORM_EPS) * g_ref[...]
        h_ref[...] = (y * (1.0 + sc_ref[...]) + sh_ref[...]).astype(BF16)

    o_ref[...] = jnp.dot(h_ref[...], w_ref[...], preferred_element_type=F32).astype(o_ref.dtype)


def _normmod_matmul(xf, g, mod_l, sc_idx, sh_idx, w, out_dtype, seq, tm, tn):
    t, d = xf.shape
    n = w.shape[1]
    tpb = seq // tm
    return pl.pallas_call(
        _normmod_matmul_kernel,
        grid=(t // tm, n // tn),
        in_specs=[pl.BlockSpec((tm, d), lambda i, j: (i, 0)),
                  pl.BlockSpec((1, d), lambda i, j: (0, 0)),
                  pl.BlockSpec((None, 1, d), lambda i, j: (i // tpb, 0, sc_idx)),
                  pl.BlockSpec((None, 1, d), lambda i, j: (i // tpb, 0, sh_idx)),
                  pl.BlockSpec((d, tn), lambda i, j: (0, j))],
        out_specs=pl.BlockSpec((tm, tn), lambda i, j: (i, j)),
        out_shape=jax.ShapeDtypeStruct((t, n), out_dtype),
        scratch_shapes=[pltpu.VMEM((tm, d), BF16)],
        compiler_params=_params(("parallel", "arbitrary")),
        name="inproj",
    )(xf, g.reshape(1, d), mod_l, mod_l, w)


def _t5_bucket(rel):
    half = REL_BUCKETS // 2
    exact = half // 2
    ret = jnp.where(rel > 0, half, 0)
    n = jnp.abs(rel)
    nf = jnp.maximum(n, 1).astype(jnp.float32)
    large = exact + (jnp.log(nf / exact) / math.log(REL_MAX_DIST / exact) * (half - exact)).astype(jnp.int32)
    large = jnp.minimum(large, half - 1)
    return ret + jnp.where(n < exact, n, large)


def _attn_bias_tiles(rel_bias, seq, qb, kb, scale):
    assert kb >= REL_MAX_DIST and kb % CHUNK == 0 and qb % kb == 0
    nh = rel_bias.shape[1]
    lo = kb + qb - 1
    rel = jnp.arange(-lo, qb, dtype=jnp.int32)
    vec = (rel_bias[_t5_bucket(rel)].astype(F32) * (1.0 / scale)).T
    period = qb + kb
    jj = np.arange(period)
    jj = np.where(jj < qb, jj, jj - period)
    qq = np.arange(qb)[None, :]
    tiles = []
    for off in range(-1, qb // kb):
        row = vec[:, np.clip(lo + off * kb - jj, 0, lo + qb - 1)]
        skew = jnp.tile(row, (1, kb))[:, :kb * (period - 1)].reshape(nh, kb, period - 1)[:, :, :qb]
        kk = off * kb + np.arange(kb)[:, None]
        visible = (kk // CHUNK) <= (qq // CHUNK)
        tiles.append(jnp.where(visible[None], skew, -jnp.inf))
    far = rel_bias[_t5_bucket(jnp.full((1,), -(seq - 1), jnp.int32))][0].astype(F32)
    return jnp.stack(tiles, axis=1), far


def _attn_kernel(far_ref, q1_ref, q2_ref, k1_ref, k2_ref, v_ref, bias_ref, lam_ref, g_ref, o_ref,
                 vt_ref, acc1_ref, acc2_ref, *, qb, kb, scale, lam_init):
    h = pl.program_id(1)
    i = pl.program_id(2)
    far = far_ref[h] * (1.0 / scale)
    exp_scale = scale * math.log2(math.e)
    nt_dims = (((1,), (1,)), ((), ()))
    per_tile = qb // kb

    @pl.when(i == 0)
    def _():
        rr = lax.broadcasted_iota(jnp.int32, (v_ref.shape[1], v_ref.shape[1]), 0)
        cc = lax.broadcasted_iota(jnp.int32, (v_ref.shape[1], v_ref.shape[1]), 1)
        eye = jnp.where(rr == cc, 1.0, 0.0).astype(BF16)

        def tr_body(j, carry):
            vj = v_ref[pl.ds(pl.multiple_of(j * kb, kb), kb), :]
            vt_ref[j] = lax.dot_general(eye, vj, nt_dims, preferred_element_type=F32).astype(BF16)
            return carry

        lax.fori_loop(0, vt_ref.shape[0], tr_body, 0)

    acc1_ref[...] = jnp.zeros(acc1_ref.shape, F32)
    acc2_ref[...] = jnp.zeros(acc2_ref.shape, F32)

    def scores(q_ref, k_ref, j):
        k = k_ref[pl.ds(pl.multiple_of(j * kb, kb), kb), :]
        return lax.dot_general(k, q_ref[...], nt_dims, preferred_element_type=F32)

    def softmax_step(s, m, l, bias, shift):
        if bias is not None:
            s = s + bias
        m_new = jnp.maximum(m, jnp.max(s, axis=0, keepdims=True) + shift)
        p = jnp.exp2((s - (m_new - shift)) * exp_scale)
        alpha = jnp.exp2((m - m_new) * exp_scale)
        return m_new, alpha * l + jnp.sum(p, axis=0, keepdims=True), alpha, p.astype(BF16)

    def both(carry, j, bias, shift):
        m1, l1, m2, l2 = carry
        s1 = scores(q1_ref, k1_ref, j)
        s2 = scores(q2_ref, k2_ref, j)
        m1, l1, alpha1, p1 = softmax_step(s1, m1, l1, bias, shift)
        pv1 = jnp.dot(vt_ref[j], p1, preferred_element_type=F32)
        m2, l2, alpha2, p2 = softmax_step(s2, m2, l2, bias, shift)
        pv2 = jnp.dot(vt_ref[j], p2, preferred_element_type=F32)
        acc1_ref[...] = acc1_ref[...] * alpha1 + pv1
        acc2_ref[...] = acc2_ref[...] * alpha2 + pv2
        return m1, l1, m2, l2

    neg = jnp.full((1, qb), -jnp.inf, F32)
    zero = jnp.zeros((1, qb), F32)
    carry = (neg, zero, neg, zero)
    first = i * per_tile
    n_far = jnp.maximum(first - 1, 0)
    carry = lax.fori_loop(0, n_far, lambda j, c: both(c, j, None, far), carry)
    carry = lax.fori_loop(n_far, first, lambda j, c: both(c, j, bias_ref[0], 0.0), carry)
    for off in range(per_tile):
        carry = both(carry, first + off, bias_ref[off + 1], 0.0)
    _, l1, _, l2 = carry

    lamv = lam_ref[...]
    lam = (jnp.exp(jnp.sum(lamv[0:1] * lamv[1:2], axis=-1, keepdims=True))
           - jnp.exp(jnp.sum(lamv[2:3] * lamv[3:4], axis=-1, keepdims=True)) + lam_init)
    o = (acc1_ref[...] / l1 - lam * (acc2_ref[...] / l2)).T
    y = o * lax.rsqrt(jnp.mean(o * o, axis=-1, keepdims=True) + SUBLN_EPS) * g_ref[...]
    o_ref[...] = (y * (1.0 - lam_init)).astype(o_ref.dtype)


def _attention(proj, bias_tiles, far, lamv, subln_g, batch, seq, lam_init):
    qb, kb = ATT_Q_BLOCK, ATT_KV_BLOCK
    nq = seq // qb
    nh = N_ATT_HEADS
    dk, dv = ATT_HEAD_DIM, ATT_V_DIM
    t = batch * seq
    k_off = 2 * nh
    v_off = 4 * nh * dk // dv
    kern = functools.partial(_attn_kernel, qb=qb, kb=kb, scale=dk ** -0.5, lam_init=lam_init)
    return pl.pallas_call(
        kern,
        grid=(batch, nh, nq),
        in_specs=[pl.BlockSpec(memory_space=pltpu.SMEM),
                  pl.BlockSpec((qb, dk), lambda b, h, i: (b * nq + i, h)),
                  pl.BlockSpec((qb, dk), lambda b, h, i: (b * nq + i, nh + h)),
                  pl.BlockSpec((seq, dk), lambda b, h, i: (b, k_off + h)),
                  pl.BlockSpec((seq, dk), lambda b, h, i: (b, k_off + nh + h)),
                  pl.BlockSpec((seq, dv), lambda b, h, i: (b, v_off + h)),
                  pl.BlockSpec((None, qb // kb + 1, kb, qb), lambda b, h, i: (h, 0, 0, 0)),
                  pl.BlockSpec((4, dk), lambda b, h, i: (0, 0)),
                  pl.BlockSpec((1, dv), lambda b, h, i: (0, 0))],
        out_specs=pl.BlockSpec((qb, dv), lambda b, h, i: (b * nq + i, h)),
        out_shape=jax.ShapeDtypeStruct((t, nh * dv), BF16),
        scratch_shapes=[pltpu.VMEM((seq // kb, dv, kb), BF16), pltpu.VMEM((dv, qb), F32), pltpu.VMEM((dv, qb), F32)],
        compiler_params=_params(("parallel", "parallel", "arbitrary")),
        name="diff_attention",
    )(far, proj, proj, proj, proj, proj, bias_tiles, lamv, subln_g.reshape(1, dv))


def _conv_silu(x_ref, hist_ref, w_ref, b_ref, blk):
    x = x_ref[...].astype(F32)
    hist_ref[pl.ds(SUBLANES, blk), :] = x
    w = w_ref[...]
    acc = x * w[SSD_CONV - 1:SSD_CONV] + b_ref[...]
    for j in range(1, SSD_CONV):
        acc = acc + hist_ref[pl.ds(SUBLANES - j, blk), :] * w[SSD_CONV - 1 - j:SSD_CONV - j]
    hist_ref[pl.ds(0, SUBLANES), :] = x[blk - SUBLANES:]
    return _silu(acc)


def _ssd_kernel(xs_ref, bm_ref, cm_ref, z_ref, dt_ref, e_ref, cwx_ref, cwb_ref, cwc_ref, cbx_ref, cbb_ref,
                cbc_ref, dtb_ref, alog_ref, dskip_ref, ng_ref, o_ref,
                state_ref, tx_ref, tb_ref, tc_ref, act_ref, *, blk):
    g = pl.program_id(1)

    @pl.when(pl.program_id(2) == 0)
    def _():
        state_ref[...] = jnp.zeros(state_ref.shape, F32)
        for hist_ref in (tx_ref, tb_ref, tc_ref):
            hist_ref[pl.ds(0, SUBLANES), :] = jnp.zeros((SUBLANES, hist_ref.shape[1]), F32)

    xs = _conv_silu(xs_ref, tx_ref, cwx_ref, cbx_ref, blk)
    bm = _conv_silu(bm_ref, tb_ref, cwb_ref, cbb_ref, blk)
    cm = _conv_silu(cm_ref, tc_ref, cwc_ref, cbc_ref, blk)

    dt = jax.nn.softplus(dt_ref[...] + dtb_ref[...])
    da = dt * (-jnp.exp(alog_ref[...]))
    row = lax.broadcasted_iota(jnp.int32, (blk, blk), 0)
    col = lax.broadcasted_iota(jnp.int32, (blk, blk), 1)
    causal = col <= row
    tri = jnp.where(causal, 1.0, 0.0).astype(BF16)
    acum = sum(jnp.dot(tri, part, preferred_element_type=F32) for part in _split3(da))
    both_e = jnp.dot(jnp.concatenate([jnp.concatenate(_split3(dt), axis=1),
                                      jnp.concatenate(_split3(acum), axis=1)], axis=0),
                     e_ref[...], preferred_element_type=F32)
    dt_e = both_e[:blk]
    acum_e = both_e[blk:]
    act_ref[...] = acum.T

    xdt = xs * dt_e
    xdt_b = xdt.astype(BF16)
    cm_b = cm.astype(BF16)
    cb = lax.dot_general(cm_b, bm.astype(BF16), (((1,), (1,)), ((), ())), preferred_element_type=F32)
    lane = lax.broadcasted_iota(jnp.int32, (blk, LANES), 1)
    halves = (lane < SSD_HEAD_DIM, lane >= SSD_HEAD_DIM)
    pieces = []
    for pair in range(GROUP_W // LANES):
        xp = xdt_b[:, pair * LANES:(pair + 1) * LANES]
        yp = jnp.zeros((blk, LANES), F32)
        for hh in range(2):
            r = 2 * pair + hh
            a_col = acum_e[:, r * SSD_HEAD_DIM:r * SSD_HEAD_DIM + 1]
            a_row = act_ref[pl.ds(g * SSD_HEADS_PER_GROUP + r, 1), :]
            decay = jnp.exp(jnp.where(causal, a_col - a_row, -jnp.inf))
            mat = (cb * decay).astype(BF16)
            yp = yp + jnp.dot(mat, jnp.where(halves[hh], xp, jnp.zeros_like(xp)), preferred_element_type=F32)
        pieces.append(yp)
    y = jnp.concatenate(pieces, axis=1)

    st = state_ref[...]
    y = y + jnp.exp(acum_e) * jnp.dot(cm_b, st.astype(BF16), preferred_element_type=F32)
    a_last = acum_e[blk - 1:blk, :]
    wgt = (xdt * jnp.exp(a_last - acum_e)).astype(BF16)
    state_ref[...] = st * jnp.exp(a_last) + jnp.dot(bm.T.astype(BF16), wgt, preferred_element_type=F32)

    y = y + dskip_ref[...] * xs
    y = y * _silu(z_ref[...].astype(F32))
    y = y * lax.rsqrt(jnp.mean(y * y, axis=-1, keepdims=True) + SUBLN_EPS)
    o_ref[...] = (y * ng_ref[...]).astype(o_ref.dtype)


def _ssd(proj, dt_raw, expand, conv_w, conv_b, dt_bias, a_log, d_skip, norm_g, batch, seq, z_col, xbc_col):
    blk = SSD_BLOCK
    nc = seq // blk
    ng = SSD_GROUPS
    t = batch * seq
    width = ng * GROUP_W
    heads = ng * SSD_HEADS_PER_GROUP
    z_blk = z_col // GROUP_W
    xs_blk = xbc_col // GROUP_W
    b_blk = (xbc_col + width) // SSD_STATE
    c_blk = b_blk + ng
    cw_b_blk = width // SSD_STATE
    pad = LANES - heads
    row = lambda b, g, c: b * nc + c
    dtb = jnp.pad(dt_bias, (0, pad)).reshape(1, LANES)
    alog = jnp.pad(a_log, (0, pad)).reshape(1, LANES)
    dskip = jnp.repeat(d_skip, SSD_HEAD_DIM).reshape(1, width)
    cb2 = conv_b.reshape(1, -1)
    kern = functools.partial(_ssd_kernel, blk=blk)
    return pl.pallas_call(
        kern,
        grid=(batch, ng, nc),
        in_specs=[pl.BlockSpec((blk, GROUP_W), lambda b, g, c: (row(b, g, c), xs_blk + g)),
                  pl.BlockSpec((blk, SSD_STATE), lambda b, g, c: (row(b, g, c), b_blk + g)),
                  pl.BlockSpec((blk, SSD_STATE), lambda b, g, c: (row(b, g, c), c_blk + g)),
                  pl.BlockSpec((blk, GROUP_W), lambda b, g, c: (row(b, g, c), z_blk + g)),
                  pl.BlockSpec((blk, LANES), lambda b, g, c: (row(b, g, c), 0)),
                  pl.BlockSpec((None, 3 * LANES, GROUP_W), lambda b, g, c: (g, 0, 0)),
                  pl.BlockSpec((SSD_CONV, GROUP_W), lambda b, g, c: (0, g)),
                  pl.BlockSpec((SSD_CONV, SSD_STATE), lambda b, g, c: (0, cw_b_blk + g)),
                  pl.BlockSpec((SSD_CONV, SSD_STATE), lambda b, g, c: (0, cw_b_blk + ng + g)),
                  pl.BlockSpec((1, GROUP_W), lambda b, g, c: (0, g)),
                  pl.BlockSpec((1, SSD_STATE), lambda b, g, c: (0, cw_b_blk + g)),
                  pl.BlockSpec((1, SSD_STATE), lambda b, g, c: (0, cw_b_blk + ng + g)),
                  pl.BlockSpec((1, LANES), lambda b, g, c: (0, 0)),
                  pl.BlockSpec((1, LANES), lambda b, g, c: (0, 0)),
                  pl.BlockSpec((1, GROUP_W), lambda b, g, c: (0, g)),
                  pl.BlockSpec((1, GROUP_W), lambda b, g, c: (0, g))],
        out_specs=pl.BlockSpec((blk, GROUP_W), lambda b, g, c: (row(b, g, c), g)),
        out_shape=jax.ShapeDtypeStruct((t, width), BF16),
        scratch_shapes=[pltpu.VMEM((SSD_STATE, GROUP_W), F32),
                        pltpu.VMEM((SUBLANES + blk, GROUP_W), F32),
                        pltpu.VMEM((SUBLANES + blk, SSD_STATE), F32),
                        pltpu.VMEM((SUBLANES + blk, SSD_STATE), F32),
                        pltpu.VMEM((LANES, blk), F32)],
        compiler_params=_params(("parallel", "parallel", "arbitrary")),
        name="ssd",
    )(proj, proj, proj, proj, dt_raw, expand, conv_w, conv_w, conv_w, cb2, cb2, cb2, dtb, alog, dskip,
      norm_g.reshape(1, width))


def _head_expand():
    e = np.zeros((SSD_GROUPS, LANES, GROUP_W), np.float32)
    for g in range(SSD_GROUPS):
        for r in range(SSD_HEADS_PER_GROUP):
            e[g, g * SSD_HEADS_PER_GROUP + r, r * SSD_HEAD_DIM:(r + 1) * SSD_HEAD_DIM] = 1.0
    return jnp.asarray(np.tile(e, (1, 3, 1)), dtype=BF16)


def _merge_kernel(oa_ref, os_ref, wa_ref, ws_ref, ga_ref, gs_ref, o_ref):
    ya = jnp.dot(oa_ref[...], wa_ref[...], preferred_element_type=F32)
    ys = jnp.dot(os_ref[...], ws_ref[...], preferred_element_type=F32)
    merged = jax.nn.sigmoid(ga_ref[...].astype(F32)) * ya + jax.nn.sigmoid(gs_ref[...].astype(F32)) * ys
    o_ref[...] = merged.astype(o_ref.dtype)


def _merge(o_att, o_ssd, proj, wa, ws, gate_col, tm, tn):
    t, ka = o_att.shape
    ks = o_ssd.shape[1]
    d = wa.shape[1]
    ga_blk = gate_col // tn
    gs_blk = (gate_col + d) // tn
    return pl.pallas_call(
        _merge_kernel,
        grid=(t // tm, d // tn),
        in_specs=[pl.BlockSpec((tm, ka), lambda i, j: (i, 0)),
                  pl.BlockSpec((tm, ks), lambda i, j: (i, 0)),
                  pl.BlockSpec((ka, tn), lambda i, j: (0, j)),
                  pl.BlockSpec((ks, tn), lambda i, j: (0, j)),
                  pl.BlockSpec((tm, tn), lambda i, j: (i, ga_blk + j)),
                  pl.BlockSpec((tm, tn), lambda i, j: (i, gs_blk + j))],
        out_specs=pl.BlockSpec((tm, tn), lambda i, j: (i, j)),
        out_shape=jax.ShapeDtypeStruct((t, d), BF16),
        compiler_params=_params(("parallel", "arbitrary")),
        name="branch_merge",
    )(o_att, o_ssd, wa, ws, proj, proj)


def _pack_bf16_pair(v):
    n = v.shape[1] // 2
    lo = lax.bitcast_convert_type(v[:, :n].astype(BF16).astype(F32), jnp.uint32)
    hi = lax.bitcast_convert_type(v[:, n:].astype(BF16).astype(F32), jnp.uint32)
    return (lo >> 16) | (hi & jnp.uint32(0xFFFF0000))


def _unpack_bf16_pair(p):
    lo = lax.bitcast_convert_type(p << 16, F32)
    hi = lax.bitcast_convert_type(p & jnp.uint32(0xFFFF0000), F32)
    return lo, hi


def _wo_kernel(m_ref, x_ref, w_ref, g1_ref, ng_ref, sc_ref, sh_ref, rw_ref, rb_ref, xo_ref, hp_ref, lg_ref):
    y = jnp.dot(m_ref[...], w_ref[...], preferred_element_type=F32)
    x = x_ref[...] + g1_ref[...] * y
    xo_ref[...] = x
    hn = x * lax.rsqrt(jnp.mean(x * x, axis=-1, keepdims=True) + NORM_EPS) * ng_ref[...]
    h2 = hn * (1.0 + sc_ref[...]) + sh_ref[...]
    hp_ref[...] = _pack_bf16_pair(h2)
    lg_ref[...] = jnp.dot(h2, rw_ref[...], precision=HIGHEST, preferred_element_type=F32) + rb_ref[...]


def _wo_residual(merged, xf, w_o, mod_l, norm2_g, router_w, router_b, seq, tm):
    t, d = xf.shape
    ne = router_w.shape[1]
    tpb = seq // tm
    mod_spec = lambda idx: pl.BlockSpec((None, 1, d), lambda i: (i // tpb, 0, idx))
    rw = jnp.pad(router_w, ((0, 0), (0, LANES - ne)))
    rb = jnp.pad(router_b, (0, LANES - ne), constant_values=-jnp.inf).reshape(1, LANES)
    return pl.pallas_call(
        _wo_kernel,
        grid=(t // tm,),
        in_specs=[pl.BlockSpec((tm, d), lambda i: (i, 0)),
                  pl.BlockSpec((tm, d), lambda i: (i, 0)),
                  pl.BlockSpec((d, d), lambda i: (0, 0)),
                  mod_spec(2),
                  pl.BlockSpec((1, d), lambda i: (0, 0)),
                  mod_spec(4), mod_spec(3),
                  pl.BlockSpec((d, LANES), lambda i: (0, 0)),
                  pl.BlockSpec((1, LANES), lambda i: (0, 0))],
        out_specs=[pl.BlockSpec((tm, d), lambda i: (i, 0)), pl.BlockSpec((tm, d // 2), lambda i: (i, 0)),
                   pl.BlockSpec((tm, LANES), lambda i: (i, 0))],
        out_shape=[jax.ShapeDtypeStruct((t, d), F32), jax.ShapeDtypeStruct((t, d // 2), jnp.uint32),
                   jax.ShapeDtypeStruct((t, LANES), F32)],
        compiler_params=_params(("parallel",)),
        name="wo_residual",
    )(merged, xf, w_o, mod_l, norm2_g.reshape(1, d), mod_l, mod_l, rw, rb)


def _router_kernel(lg_ref, idx_ref, gate_ref, rank_ref, cnt_ref, run_ref, *, tm):
    @pl.when(pl.program_id(0) == 0)
    def _():
        run_ref[...] = jnp.zeros(run_ref.shape, F32)

    logits = lg_ref[...]
    lane = lax.broadcasted_iota(jnp.int32, (tm, LANES), 1)
    vals = logits
    picked = jnp.zeros((tm, LANES), F32)
    top_v, top_sel, top_i = [], [], []
    for _ in range(TOP_K):
        m = jnp.max(vals, axis=-1, keepdims=True)
        idx = jnp.min(jnp.where(vals == m, lane, LANES), axis=-1, keepdims=True)
        sel = lane == idx
        top_v.append(m)
        top_i.append(idx)
        top_sel.append(sel)
        vals = jnp.where(sel, -jnp.inf, vals)
        picked = picked + sel.astype(F32)

    row = lax.broadcasted_iota(jnp.int32, (tm, tm), 0)
    col = lax.broadcasted_iota(jnp.int32, (tm, tm), 1)
    before = jnp.dot((col < row).astype(BF16), picked.astype(BF16), preferred_element_type=F32) + run_ref[...]
    run_ref[...] = run_ref[...] + jnp.sum(picked, axis=0, keepdims=True)
    cnt_ref[...] = run_ref[...]

    exps = [jnp.exp(v - top_v[0]) for v in top_v]
    denom = exps[0] + exps[1] + exps[2] + exps[3]
    idx_out = jnp.zeros((tm, LANES), jnp.int32)
    rank_out = jnp.zeros((tm, LANES), jnp.int32)
    gate_out = jnp.zeros((tm, LANES), F32)
    for k in range(TOP_K):
        rank_k = jnp.sum(jnp.where(top_sel[k], before, 0.0), axis=-1, keepdims=True).astype(jnp.int32)
        idx_out = jnp.where(lane == k, top_i[k], idx_out)
        rank_out = jnp.where(lane == k, rank_k, rank_out)
        gate_out = jnp.where(lane == k, exps[k] / denom, gate_out)
    idx_ref[...] = idx_out
    rank_ref[...] = rank_out
    gate_ref[...] = gate_out


def _router(logits, tm):
    t = logits.shape[0]
    kern = functools.partial(_router_kernel, tm=tm)
    tok_spec = pl.BlockSpec((tm, LANES), lambda i: (i, 0))
    return pl.pallas_call(
        kern,
        grid=(t // tm,),
        in_specs=[tok_spec],
        out_specs=[tok_spec, tok_spec, tok_spec, pl.BlockSpec((1, LANES), lambda i: (0, 0))],
        out_shape=[jax.ShapeDtypeStruct((t, LANES), jnp.int32), jax.ShapeDtypeStruct((t, LANES), F32),
                   jax.ShapeDtypeStruct((t, LANES), jnp.int32), jax.ShapeDtypeStruct((1, LANES), F32)],
        scratch_shapes=[pltpu.VMEM((1, LANES), F32)],
        compiler_params=_params(("arbitrary",)),
        name="router",
    )(logits)


def _dispatch_kernel(dest_ref, h_ref, xs_in_hbm, xs_hbm, sem, *, tm):
    del xs_in_hbm

    def row_copy(t, k):
        return pltpu.make_async_copy(h_ref.at[pl.ds(t, 1)],
                                     xs_hbm.at[pl.ds(dest_ref[t * TOP_K + k], 1)], sem)

    def issue(t, carry):
        for k in range(TOP_K):
            row_copy(t, k).start()
        return carry

    def drain(t, carry):
        for k in range(TOP_K):
            row_copy(t, k).wait()
        return carry

    lax.fori_loop(0, tm, issue, 0)
    lax.fori_loop(0, tm, drain, 0)


def _dispatch(h2, dest_flat, n_rows, tm):
    t, d = h2.shape
    kern = functools.partial(_dispatch_kernel, tm=tm)
    return pl.pallas_call(
        kern,
        grid=(t // tm,),
        in_specs=[pl.BlockSpec((tm * TOP_K,), lambda i: (i,), memory_space=pltpu.SMEM),
                  pl.BlockSpec((tm, d), lambda i: (i, 0)),
                  pl.BlockSpec(memory_space=pl.ANY)],
        out_specs=pl.BlockSpec(memory_space=pl.ANY),
        out_shape=jax.ShapeDtypeStruct((n_rows, d), h2.dtype),
        scratch_shapes=[pltpu.SemaphoreType.DMA(())],
        input_output_aliases={2: 0},
        compiler_params=_params(("arbitrary",)),
        name="moe_dispatch",
    )(dest_flat, h2, jnp.zeros((n_rows, d), h2.dtype))


def _expert_gu_kernel(be_ref, nv_ref, x_ref, w_ref, b_ref, o_ref, *, ff):
    del be_ref
    valid = pl.program_id(0) < nv_ref[0]

    @pl.when(valid)
    def _():
        x_lo, x_hi = _unpack_bf16_pair(x_ref[...])
        half = x_lo.shape[1]
        gu = (jnp.dot(x_lo.astype(BF16), w_ref[pl.ds(0, half), :], preferred_element_type=F32)
              + jnp.dot(x_hi.astype(BF16), w_ref[pl.ds(half, half), :], preferred_element_type=F32) + b_ref[...])
        g = jnp.minimum(gu[:, :ff], SWIGLU_LIMIT)
        u = jnp.clip(gu[:, ff:], -SWIGLU_LIMIT, SWIGLU_LIMIT)
        o_ref[...] = ((u + 1.0) * (g * jax.nn.sigmoid(SWIGLU_ALPHA * g))).astype(o_ref.dtype)

    @pl.when(jnp.logical_not(valid))
    def _():
        o_ref[...] = jnp.zeros(o_ref.shape, o_ref.dtype)


def _expert_down_kernel(be_ref, nv_ref, a_ref, w_ref, b_ref, o_ref):
    del be_ref
    valid = pl.program_id(0) < nv_ref[0]

    @pl.when(valid)
    def _():
        o_ref[...] = _pack_bf16_pair(jnp.dot(a_ref[...], w_ref[...], preferred_element_type=F32) + b_ref[...])

    @pl.when(jnp.logical_not(valid))
    def _():
        o_ref[...] = jnp.zeros(o_ref.shape, o_ref.dtype)


def _expert_ffn(xs, block_e, n_valid, w_gu, b_gu, w_dn, b_dn):
    n_rows, dp = xs.shape
    bm = EXPERT_BLOCK
    n_blocks = n_rows // bm
    ne, d, ff2 = w_gu.shape
    ff = ff2 // 2
    act = pl.pallas_call(
        functools.partial(_expert_gu_kernel, ff=ff),
        grid_spec=pltpu.PrefetchScalarGridSpec(
            num_scalar_prefetch=2, grid=(n_blocks,),
            in_specs=[pl.BlockSpec((bm, dp), lambda i, be, nv: (i, 0)),
                      pl.BlockSpec((None, d, ff2), lambda i, be, nv: (be[i], 0, 0)),
                      pl.BlockSpec((None, 1, ff2), lambda i, be, nv: (be[i], 0, 0))],
            out_specs=pl.BlockSpec((bm, ff), lambda i, be, nv: (i, 0))),
        out_shape=jax.ShapeDtypeStruct((n_rows, ff), BF16),
        compiler_params=_params(("arbitrary",)),
        name="expert_gate_up",
    )(block_e, n_valid, xs, w_gu, b_gu.reshape(ne, 1, ff2))
    return pl.pallas_call(
        _expert_down_kernel,
        grid_spec=pltpu.PrefetchScalarGridSpec(
            num_scalar_prefetch=2, grid=(n_blocks,),
            in_specs=[pl.BlockSpec((bm, ff), lambda i, be, nv: (i, 0)),
                      pl.BlockSpec((None, ff, d), lambda i, be, nv: (be[i], 0, 0)),
                      pl.BlockSpec((None, 1, d), lambda i, be, nv: (be[i], 0, 0))],
            out_specs=pl.BlockSpec((bm, dp), lambda i, be, nv: (i, 0))),
        out_shape=jax.ShapeDtypeStruct((n_rows, dp), jnp.uint32),
        compiler_params=_params(("arbitrary",)),
        name="expert_down",
    )(block_e, n_valid, act, w_dn, b_dn.reshape(ne, 1, d))


def _combine_kernel(dest_ref, ys_hbm, x_ref, gate_ref, g2_ref, fg_ref, o_ref, buf, sem, *, tm, final):
    def row_copy(t, k):
        return pltpu.make_async_copy(ys_hbm.at[pl.ds(dest_ref[t * TOP_K + k], 1)],
                                     buf.at[k, pl.ds(t, 1)], sem)

    def issue(t, carry):
        for k in range(TOP_K):
            row_copy(t, k).start()
        return carry

    def drain(t, carry):
        for k in range(TOP_K):
            row_copy(t, k).wait()
        return carry

    lax.fori_loop(0, tm, issue, 0)
    lax.fori_loop(0, tm, drain, 0)

    gates = gate_ref[...]
    moe_lo, moe_hi = (gates[:, 0:1] * part for part in _unpack_bf16_pair(buf[0]))
    for k in range(1, TOP_K):
        lo, hi = _unpack_bf16_pair(buf[k])
        moe_lo = moe_lo + gates[:, k:k + 1] * lo
        moe_hi = moe_hi + gates[:, k:k + 1] * hi
    x = x_ref[...] + g2_ref[...] * jnp.concatenate([moe_lo, moe_hi], axis=1)
    if final:
        x = x * lax.rsqrt(jnp.mean(x * x, axis=-1, keepdims=True) + NORM_EPS) * fg_ref[...]
    o_ref[...] = x


def _combine(ys, dest_flat, xf, gates, mod_l, final_g, seq, tm, final):
    t, d = xf.shape
    tpb = seq // tm
    kern = functools.partial(_combine_kernel, tm=tm, final=final)
    return pl.pallas_call(
        kern,
        grid=(t // tm,),
        in_specs=[pl.BlockSpec((tm * TOP_K,), lambda i: (i,), memory_space=pltpu.SMEM),
                  pl.BlockSpec(memory_space=pl.ANY),
                  pl.BlockSpec((tm, d), lambda i: (i, 0)),
                  pl.BlockSpec((tm, LANES), lambda i: (i, 0)),
                  pl.BlockSpec((None, 1, d), lambda i: (i // tpb, 0, 5)),
                  pl.BlockSpec((1, d), lambda i: (0, 0))],
        out_specs=pl.BlockSpec((tm, d), lambda i: (i, 0)),
        out_shape=jax.ShapeDtypeStruct((t, d), F32),
        scratch_shapes=[pltpu.VMEM((TOP_K, tm, d // 2), jnp.uint32), pltpu.SemaphoreType.DMA(())],
        compiler_params=_params(("arbitrary",)),
        name="moe_combine",
    )(dest_flat, ys, xf, gates, mod_l, final_g.reshape(1, d))


def _moe(h2p, logits, xf, mod_l, w_gu, b_gu, w_dn, b_dn, final_g, seq, final):
    t = h2p.shape[0]
    ne = w_gu.shape[0]
    bm = EXPERT_BLOCK
    top_i, gates, rank, counts = _router(logits, tm=256)
    cnt = counts[0, :ne].astype(jnp.int32)
    padded = (cnt + bm - 1) // bm * bm
    pad_end = jnp.cumsum(padded)
    pad_start = pad_end - padded
    experts = jnp.arange(ne, dtype=jnp.int32)
    start_of = jnp.sum(jnp.where(top_i[:, :TOP_K, None] == experts, pad_start, 0), axis=-1)
    dest = start_of + rank[:, :TOP_K]
    n_blocks = t * TOP_K // bm + ne
    blk_row = jnp.arange(n_blocks, dtype=jnp.int32)[:, None] * bm
    block_e = jnp.minimum(jnp.sum((pad_end[None, :] <= blk_row).astype(jnp.int32), axis=1), ne - 1)
    n_valid = (pad_end[-1:] // bm).astype(jnp.int32)
    dest_flat = dest.reshape(-1).astype(jnp.int32)

    xs = _dispatch(h2p, dest_flat, n_blocks * bm, tm=256)
    ys = _expert_ffn(xs, block_e, n_valid, w_gu, b_gu, w_dn, b_dn)
    return _combine(ys, dest_flat, xf, gates, mod_l, final_g, seq, tm=128, final=final)


def kernel(x, c, ada_w, ada_b, norm1_g, w_in, rel_bias, lam_q1, lam_k1, lam_q2, lam_k2, attn_subln_g, conv_w, conv_b, dt_bias, a_log, d_skip, ssd_norm_g, w_branch, w_o, norm2_g, router_w, router_b, w_gate_up, b_gate_up, w_down, b_down, final_g):
    batch, seq, d = x.shape
    t = batch * seq
    depth = ada_w.shape[0]
    att_w = N_ATT_HEADS * ATT_V_DIM
    ssd_w = SSD_GROUPS * GROUP_W
    n_heads_ssd = SSD_GROUPS * SSD_HEADS_PER_GROUP
    conv_ch = conv_w.shape[2]
    z_col = 2 * (2 * N_ATT_HEADS * ATT_HEAD_DIM) + att_w
    xbc_col = z_col + ssd_w
    dt_col = xbc_col + conv_ch
    gate_col = dt_col

    mod = _adaln(c, ada_w, ada_b)
    bias_tiles, far = _attn_bias_tiles(rel_bias, seq, ATT_Q_BLOCK, ATT_KV_BLOCK, ATT_HEAD_DIM ** -0.5)
    expand = _head_expand()
    xf = x.reshape(t, d)
    tm_big = min(1024, seq)

    for layer in range(depth):
        mod_l = mod[layer].reshape(batch, 1, 6 * d)
        w_l = w_in[layer]
        w_main = jnp.concatenate([w_l[:, :dt_col], w_l[:, dt_col + n_heads_ssd:]], axis=1).astype(BF16)
        w_dt = jnp.pad(w_l[:, dt_col:dt_col + n_heads_ssd], ((0, 0), (0, LANES - n_heads_ssd))).astype(BF16)

        proj = _normmod_matmul(xf, norm1_g[layer], mod_l, 1, 0, w_main, BF16, seq, tm=tm_big, tn=512)
        dt_raw = _normmod_matmul(xf, norm1_g[layer], mod_l, 1, 0, w_dt, F32, seq, tm=tm_big, tn=LANES)

        lam_init = 0.8 - 0.6 * math.exp(-0.3 * layer)
        lamv = jnp.stack([lam_q1[layer], lam_k1[layer], lam_q2[layer], lam_k2[layer]], axis=0)
        o_att = _attention(proj, bias_tiles, far, lamv, attn_subln_g[layer], batch, seq, lam_init)
        o_ssd = _ssd(proj, dt_raw, expand, conv_w[layer], conv_b[layer], dt_bias[layer], a_log[layer],
                     d_skip[layer], ssd_norm_g[layer], batch, seq, z_col, xbc_col)

        wa = w_branch[layer, :att_w].astype(BF16)
        ws = w_branch[layer, att_w:].astype(BF16)
        merged = _merge(o_att, o_ssd, proj, wa, ws, gate_col, tm=tm_big, tn=256)
        xf, h2p, logits = _wo_residual(merged, xf, w_o[layer].astype(BF16), mod_l, norm2_g[layer],
                                       router_w[layer], router_b[layer], seq, tm=256)

        xf = _moe(h2p, logits, xf, mod_l, w_gate_up[layer].astype(BF16), b_gate_up[layer],
                  w_down[layer].astype(BF16), b_down[layer], final_g, seq, final=(layer == depth - 1))
    return xf.reshape(batch, seq, d)
```

```python
import functools
import math

import numpy as np
import jax
import jax.numpy as jnp
from jax import lax
from jax.experimental import pallas as pl
from jax.experimental.pallas import tpu as pltpu

F32 = jnp.float32
BF16 = jnp.bfloat16
HIGHEST = lax.Precision.HIGHEST

DEPTH = 2
CHUNK = 64
N_ATT_HEADS = 8
ATT_HEAD_DIM = 128
ATT_V_DIM = 2 * ATT_HEAD_DIM
REL_BUCKETS = 32
REL_MAX_DIST = 128
SSD_HEAD_DIM = 64
SSD_GROUPS = 8
SSD_HEADS_PER_GROUP = 8
SSD_STATE = 128
SSD_CONV = 4
N_EXPERTS = 32
TOP_K = 4
SWIGLU_LIMIT = 7.0
SWIGLU_ALPHA = 1.702
NORM_EPS = 1e-6
SUBLN_EPS = 1e-5

LANES = 128
SUBLANES = 8
BF16_ROWS = 16
VMEM_LIMIT = 56 * 1024 * 1024

ATT_Q_BLOCK = 512
ATT_KV_BLOCK = 256
SSD_BLOCK = 256
EXPERT_BLOCK = 256
GROUP_W = SSD_HEADS_PER_GROUP * SSD_HEAD_DIM


def _params(semantics):
    return pltpu.CompilerParams(dimension_semantics=semantics, vmem_limit_bytes=VMEM_LIMIT)


def _silu(v):
    half = 0.5 * v
    return half + half * jnp.tanh(half)


def _split3(v):
    hi = v.astype(BF16)
    rest = v - hi.astype(F32)
    mid = rest.astype(BF16)
    lo = (rest - mid.astype(F32)).astype(BF16)
    return hi, mid, lo


def _adaln_kernel(c_ref, w_ref, b_ref, o_ref):
    ca = _silu(c_ref[...])
    o_ref[...] = jnp.dot(ca, w_ref[...], precision=HIGHEST, preferred_element_type=F32) + b_ref[...]


def _adaln(c, ada_w, ada_b):
    nl, d, n = ada_w.shape
    b = c.shape[0]
    tn = 1024
    return pl.pallas_call(
        _adaln_kernel,
        grid=(nl, n // tn),
        in_specs=[pl.BlockSpec((b, d), lambda l, j: (0, 0)),
                  pl.BlockSpec((None, d, tn), lambda l, j: (l, 0, j)),
                  pl.BlockSpec((None, 1, tn), lambda l, j: (l, 0, j))],
        out_specs=pl.BlockSpec((None, b, tn), lambda l, j: (l, 0, j)),
        out_shape=jax.ShapeDtypeStruct((nl, b, n), F32),
        compiler_params=_params(("parallel", "parallel")),
        name="adaln",
    )(c, ada_w, ada_b.reshape(nl, 1, n))


def _normmod_matmul_kernel(x_ref, g_ref, sc_ref, sh_ref, w_ref, o_ref, h_ref):
    @pl.when(pl.program_id(1) == 0)
    def _():
        x = x_ref[...]
        y = x * lax.rsqrt(jnp.mean(x * x, axis=-1, keepdims=True) + NORM_EPS) * g_ref[...]
        h_ref[...] = (y * (1.0 + sc_ref[...]) + sh_ref[...]).astype(BF16)

    o_ref[...] = jnp.dot(h_ref[...], w_ref[...], preferred_element_type=F32).astype(o_ref.dtype)


def _normmod_matmul(xf, g, mod_l, sc_idx, sh_idx, w, out_dtype, seq, tm, tn):
    t, d = xf.shape
    n = w.shape[1]
    tpb = seq // tm
    return pl.pallas_call(
        _normmod_matmul_kernel,
        grid=(t // tm, n // tn),
        in_specs=[pl.BlockSpec((tm, d), lambda i, j: (i, 0)),
                  pl.BlockSpec((1, d), lambda i, j: (0, 0)),
                  pl.BlockSpec((None, 1, d), lambda i, j: (i // tpb, 0, sc_idx)),
                  pl.BlockSpec((None, 1, d), lambda i, j: (i // tpb, 0, sh_idx)),
                  pl.BlockSpec((d, tn), lambda i, j: (0, j))],
        out_specs=pl.BlockSpec((tm, tn), lambda i, j: (i, j)),
        out_shape=jax.ShapeDtypeStruct((t, n), out_dtype),
        scratch_shapes=[pltpu.VMEM((tm, d), BF16)],
        compiler_params=_params(("parallel", "arbitrary")),
        name="inproj",
    )(xf, g.reshape(1, d), mod_l, mod_l, w)


def _t5_bucket(rel):
    half = REL_BUCKETS // 2
    exact = half // 2
    ret = jnp.where(rel > 0, half, 0)
    n = jnp.abs(rel)
    nf = jnp.maximum(n, 1).astype(jnp.float32)
    large = exact + (jnp.log(nf / exact) / math.log(REL_MAX_DIST / exact) * (half - exact)).astype(jnp.int32)
    large = jnp.minimum(large, half - 1)
    return ret + jnp.where(n < exact, n, large)


def _attn_bias_tiles(rel_bias, seq, qb, kb):
    assert kb >= REL_MAX_DIST and kb % CHUNK == 0 and qb % kb == 0
    nh = rel_bias.shape[1]
    lo = kb + qb - 1
    rel = jnp.arange(-lo, qb, dtype=jnp.int32)
    vec = (rel_bias[_t5_bucket(rel)].astype(F32) * math.log2(math.e)).T
    period = qb + kb
    jj = np.arange(period)
    jj = np.where(jj < qb, jj, jj - period)
    qq = np.arange(qb)[None, :]
    tiles = []
    for off in range(-1, qb // kb):
        row = vec[:, np.clip(lo + off * kb - jj, 0, lo + qb - 1)]
        skew = jnp.tile(row, (1, kb))[:, :kb * (period - 1)].reshape(nh, kb, period - 1)[:, :, :qb]
        kk = off * kb + np.arange(kb)[:, None]
        visible = (kk // CHUNK) <= (qq // CHUNK)
        tiles.append(jnp.where(visible[None], skew, -jnp.inf))
    far = rel_bias[_t5_bucket(jnp.full((1,), -(seq - 1), jnp.int32))][0].astype(F32)
    return jnp.stack(tiles, axis=1), far


def _attn_kernel(far_ref, q1_ref, q2_ref, k1_ref, k2_ref, v_ref, bias_ref, lam_ref, g_ref, o_ref,
                 vt_ref, acc1_ref, acc2_ref, sa1_ref, sa2_ref, sb1_ref, sb2_ref, *, qb, kb, dv, lam_init):
    h = pl.program_id(1)
    i = pl.program_id(2)
    far = far_ref[h] * math.log2(math.e)
    nt_dims = (((1,), (1,)), ((), ()))
    per_tile = qb // kb
    pad_rows = vt_ref.shape[1] - dv
    s_a = (sa1_ref, sa2_ref)
    s_b = (sb1_ref, sb2_ref)

    @pl.when(i == 0)
    def _():
        rr = lax.broadcasted_iota(jnp.int32, (dv, dv), 0)
        cc = lax.broadcasted_iota(jnp.int32, (dv, dv), 1)
        eye = jnp.where(rr == cc, 1.0, 0.0).astype(BF16)
        ones_row = jnp.where(lax.broadcasted_iota(jnp.int32, (pad_rows, kb), 0) == 0, 1.0, 0.0).astype(BF16)

        def tr_body(j, carry):
            vj = v_ref[pl.ds(pl.multiple_of(j * kb, kb), kb), :]
            vt_ref[j, pl.ds(0, dv), :] = lax.dot_general(eye, vj, nt_dims, preferred_element_type=F32).astype(BF16)
            vt_ref[j, pl.ds(dv, pad_rows), :] = ones_row
            return carry

        lax.fori_loop(0, vt_ref.shape[0], tr_body, 0)

    acc1_ref[...] = jnp.zeros(acc1_ref.shape, F32)
    acc2_ref[...] = jnp.zeros(acc2_ref.shape, F32)

    def scores(j):
        ks = pl.ds(pl.multiple_of(j * kb, kb), kb)
        return (lax.dot_general(k1_ref[ks, :], q1_ref[...], nt_dims, preferred_element_type=F32),
                lax.dot_general(k2_ref[ks, :], q2_ref[...], nt_dims, preferred_element_type=F32))

    def softmax_step(s, m, bias, shift):
        if bias is not None:
            s = s + bias
        m_new = jnp.maximum(m, jnp.max(s, axis=0, keepdims=True) + shift)
        return m_new, jnp.exp2(m - m_new), jnp.exp2(s - (m_new - shift)).astype(BF16)

    def step(carry, j, src, dst, bias, shift):
        m1, m2 = carry
        if dst is not None:
            dst[0][...], dst[1][...] = scores(j + 1)
        m1, alpha1, p1 = softmax_step(src[0][...], m1, bias, shift)
        pv1 = jnp.dot(vt_ref[j], p1, preferred_element_type=F32)
        m2, alpha2, p2 = softmax_step(src[1][...], m2, bias, shift)
        pv2 = jnp.dot(vt_ref[j], p2, preferred_element_type=F32)
        acc1_ref[...] = acc1_ref[...] * alpha1 + pv1
        acc2_ref[...] = acc2_ref[...] * alpha2 + pv2
        return m1, m2

    def pair(carry, j, kinds, final=False):
        (bias_a, shift_a), (bias_b, shift_b) = kinds
        carry = step(carry, j, s_a, s_b, bias_a, shift_a)
        return step(carry, j + 1, s_b, None if final else s_a, bias_b, shift_b)

    assert per_tile == 2
    far_kind = (None, far)
    neg = jnp.full((1, qb), -jnp.inf, F32)
    s_a[0][...], s_a[1][...] = scores(0)
    n_ff = jnp.maximum(i - 1, 0)
    carry = lax.fori_loop(0, n_ff, lambda t, c: pair(c, 2 * t, (far_kind, far_kind)), (neg, neg))
    carry = lax.fori_loop(n_ff, i, lambda t, c: pair(c, 2 * t, (far_kind, (bias_ref[0], 0.0))), carry)
    pair(carry, 2 * i, ((bias_ref[1], 0.0), (bias_ref[2], 0.0)), final=True)

    lamv = lam_ref[...]
    lam = (jnp.exp(jnp.sum(lamv[0:1] * lamv[1:2], axis=-1, keepdims=True))
           - jnp.exp(jnp.sum(lamv[2:3] * lamv[3:4], axis=-1, keepdims=True)) + lam_init)
    a1 = acc1_ref[...]
    a2 = acc2_ref[...]
    o = (a1[:dv] / a1[dv:dv + 1] - lam * (a2[:dv] / a2[dv:dv + 1])).T
    y = o * lax.rsqrt(jnp.mean(o * o, axis=-1, keepdims=True) + SUBLN_EPS) * g_ref[...]
    o_ref[...] = (y * (1.0 - lam_init)).astype(o_ref.dtype)


def _attention(proj, bias_tiles, far, lamv, subln_g, batch, seq, lam_init):
    qb, kb = ATT_Q_BLOCK, ATT_KV_BLOCK
    nq = seq // qb
    nh = N_ATT_HEADS
    dk, dv = ATT_HEAD_DIM, ATT_V_DIM
    t = batch * seq
    k_off = 2 * nh
    v_off = 4 * nh * dk // dv
    kern = functools.partial(_attn_kernel, qb=qb, kb=kb, dv=dv, lam_init=lam_init)
    return pl.pallas_call(
        kern,
        grid=(batch, nh, nq),
        in_specs=[pl.BlockSpec(memory_space=pltpu.SMEM),
                  pl.BlockSpec((qb, dk), lambda b, h, i: (b * nq + i, h)),
                  pl.BlockSpec((qb, dk), lambda b, h, i: (b * nq + i, nh + h)),
                  pl.BlockSpec((seq, dk), lambda b, h, i: (b, k_off + h)),
                  pl.BlockSpec((seq, dk), lambda b, h, i: (b, k_off + nh + h)),
                  pl.BlockSpec((seq, dv), lambda b, h, i: (b, v_off + h)),
                  pl.BlockSpec((None, qb // kb + 1, kb, qb), lambda b, h, i: (h, 0, 0, 0)),
                  pl.BlockSpec((4, dk), lambda b, h, i: (0, 0)),
                  pl.BlockSpec((1, dv), lambda b, h, i: (0, 0))],
        out_specs=pl.BlockSpec((qb, dv), lambda b, h, i: (b * nq + i, h)),
        out_shape=jax.ShapeDtypeStruct((t, nh * dv), BF16),
        scratch_shapes=[pltpu.VMEM((seq // kb, dv + BF16_ROWS, kb), BF16),
                        pltpu.VMEM((dv + BF16_ROWS, qb), F32), pltpu.VMEM((dv + BF16_ROWS, qb), F32)]
        + [pltpu.VMEM((kb, qb), F32)] * 4,
        compiler_params=_params(("parallel", "parallel", "arbitrary")),
        name="diff_attention",
    )(far, proj, proj, proj, proj, proj, bias_tiles, lamv, subln_g.reshape(1, dv))


def _conv_silu(x_ref, hist_ref, w_ref, b_ref, blk):
    x = x_ref[...].astype(F32)
    hist_ref[pl.ds(SUBLANES, blk), :] = x
    w = w_ref[...]
    acc = x * w[SSD_CONV - 1:SSD_CONV] + b_ref[...]
    for j in range(1, SSD_CONV):
        acc = acc + hist_ref[pl.ds(SUBLANES - j, blk), :] * w[SSD_CONV - 1 - j:SSD_CONV - j]
    hist_ref[pl.ds(0, SUBLANES), :] = x[blk - SUBLANES:]
    return _silu(acc)


def _ssd_kernel(xs_ref, bm_ref, cm_ref, z_ref, dt_ref, e_ref, cwx_ref, cwb_ref, cwc_ref, cbx_ref, cbb_ref,
                cbc_ref, dtb_ref, alog_ref, dskip_ref, ng_ref, o_ref,
                state_ref, tx_ref, tb_ref, tc_ref, act_ref, *, blk):
    g = pl.program_id(1)

    @pl.when(pl.program_id(2) == 0)
    def _():
        state_ref[...] = jnp.zeros(state_ref.shape, F32)
        for hist_ref in (tx_ref, tb_ref, tc_ref):
            hist_ref[pl.ds(0, SUBLANES), :] = jnp.zeros((SUBLANES, hist_ref.shape[1]), F32)

    xs = _conv_silu(xs_ref, tx_ref, cwx_ref, cbx_ref, blk)
    bm = _conv_silu(bm_ref, tb_ref, cwb_ref, cbb_ref, blk)
    cm = _conv_silu(cm_ref, tc_ref, cwc_ref, cbc_ref, blk)

    dt = jax.nn.softplus(dt_ref[...] + dtb_ref[...])
    da = dt * (-jnp.exp(alog_ref[...]))
    row = lax.broadcasted_iota(jnp.int32, (blk, blk), 0)
    col = lax.broadcasted_iota(jnp.int32, (blk, blk), 1)
    causal = col <= row
    tri = jnp.where(causal, 1.0, 0.0).astype(BF16)
    acum = sum(jnp.dot(tri, part, preferred_element_type=F32) for part in _split3(da))
    both_e = jnp.dot(jnp.concatenate([jnp.concatenate(_split3(dt), axis=1),
                                      jnp.concatenate(_split3(acum), axis=1)], axis=0),
                     e_ref[...], preferred_element_type=F32)
    dt_e = both_e[:blk]
    acum_e = both_e[blk:]
    act_ref[...] = acum.T

    xdt = xs * dt_e
    xdt_b = xdt.astype(BF16)
    cm_b = cm.astype(BF16)
    cb = lax.dot_general(cm_b, bm.astype(BF16), (((1,), (1,)), ((), ())), preferred_element_type=F32)
    lane = lax.broadcasted_iota(jnp.int32, (blk, LANES), 1)
    halves = (lane < SSD_HEAD_DIM, lane >= SSD_HEAD_DIM)
    pieces = []
    for pair in range(GROUP_W // LANES):
        xp = xdt_b[:, pair * LANES:(pair + 1) * LANES]
        yp = jnp.zeros((blk, LANES), F32)
        for hh in range(2):
            r = 2 * pair + hh
            a_col = acum_e[:, r * SSD_HEAD_DIM:r * SSD_HEAD_DIM + 1]
            a_row = act_ref[pl.ds(g * SSD_HEADS_PER_GROUP + r, 1), :]
            decay = jnp.exp(jnp.where(causal, a_col - a_row, -jnp.inf))
            mat = (cb * decay).astype(BF16)
            yp = yp + jnp.dot(mat, jnp.where(halves[hh], xp, jnp.zeros_like(xp)), preferred_element_type=F32)
        pieces.append(yp)
    y = jnp.concatenate(pieces, axis=1)

    st = state_ref[...]
    y = y + jnp.exp(acum_e) * jnp.dot(cm_b, st.astype(BF16), preferred_element_type=F32)
    a_last = acum_e[blk - 1:blk, :]
    wgt = (xdt * jnp.exp(a_last - acum_e)).astype(BF16)
    state_ref[...] = st * jnp.exp(a_last) + jnp.dot(bm.T.astype(BF16), wgt, preferred_element_type=F32)

    y = y + dskip_ref[...] * xs
    y = y * _silu(z_ref[...].astype(F32))
    y = y * lax.rsqrt(jnp.mean(y * y, axis=-1, keepdims=True) + SUBLN_EPS)
    o_ref[...] = (y * ng_ref[...]).astype(o_ref.dtype)


def _ssd(proj, dt_raw, expand, conv_w, conv_b, dt_bias, a_log, d_skip, norm_g, batch, seq, z_col, xbc_col):
    blk = SSD_BLOCK
    nc = seq // blk
    ng = SSD_GROUPS
    t = batch * seq
    width = ng * GROUP_W
    heads = ng * SSD_HEADS_PER_GROUP
    z_blk = z_col // GROUP_W
    xs_blk = xbc_col // GROUP_W
    b_blk = (xbc_col + width) // SSD_STATE
    c_blk = b_blk + ng
    cw_b_blk = width // SSD_STATE
    pad = LANES - heads
    row = lambda b, g, c: b * nc + c
    dtb = jnp.pad(dt_bias, (0, pad)).reshape(1, LANES)
    alog = jnp.pad(a_log, (0, pad)).reshape(1, LANES)
    dskip = jnp.repeat(d_skip, SSD_HEAD_DIM).reshape(1, width)
    cb2 = conv_b.reshape(1, -1)
    kern = functools.partial(_ssd_kernel, blk=blk)
    return pl.pallas_call(
        kern,
        grid=(batch, ng, nc),
        in_specs=[pl.BlockSpec((blk, GROUP_W), lambda b, g, c: (row(b, g, c), xs_blk + g)),
                  pl.BlockSpec((blk, SSD_STATE), lambda b, g, c: (row(b, g, c), b_blk + g)),
                  pl.BlockSpec((blk, SSD_STATE), lambda b, g, c: (row(b, g, c), c_blk + g)),
                  pl.BlockSpec((blk, GROUP_W), lambda b, g, c: (row(b, g, c), z_blk + g)),
                  pl.BlockSpec((blk, LANES), lambda b, g, c: (row(b, g, c), 0)),
                  pl.BlockSpec((None, 3 * LANES, GROUP_W), lambda b, g, c: (g, 0, 0)),
                  pl.BlockSpec((SSD_CONV, GROUP_W), lambda b, g, c: (0, g)),
                  pl.BlockSpec((SSD_CONV, SSD_STATE), lambda b, g, c: (0, cw_b_blk + g)),
                  pl.BlockSpec((SSD_CONV, SSD_STATE), lambda b, g, c: (0, cw_b_blk + ng + g)),
                  pl.BlockSpec((1, GROUP_W), lambda b, g, c: (0, g)),
                  pl.BlockSpec((1, SSD_STATE), lambda b, g, c: (0, cw_b_blk + g)),
                  pl.BlockSpec((1, SSD_STATE), lambda b, g, c: (0, cw_b_blk + ng + g)),
                  pl.BlockSpec((1, LANES), lambda b, g, c: (0, 0)),
                  pl.BlockSpec((1, LANES), lambda b, g, c: (0, 0)),
                  pl.BlockSpec((1, GROUP_W), lambda b, g, c: (0, g)),
                  pl.BlockSpec((1, GROUP_W), lambda b, g, c: (0, g))],
        out_specs=pl.BlockSpec((blk, GROUP_W), lambda b, g, c: (row(b, g, c), g)),
        out_shape=jax.ShapeDtypeStruct((t, width), BF16),
        scratch_shapes=[pltpu.VMEM((SSD_STATE, GROUP_W), F32),
                        pltpu.VMEM((SUBLANES + blk, GROUP_W), F32),
                        pltpu.VMEM((SUBLANES + blk, SSD_STATE), F32),
                        pltpu.VMEM((SUBLANES + blk, SSD_STATE), F32),
                        pltpu.VMEM((LANES, blk), F32)],
        compiler_params=_params(("parallel", "parallel", "arbitrary")),
        name="ssd",
    )(proj, proj, proj, proj, dt_raw, expand, conv_w, conv_w, conv_w, cb2, cb2, cb2, dtb, alog, dskip,
      norm_g.reshape(1, width))


def _head_expand():
    e = np.zeros((SSD_GROUPS, LANES, GROUP_W), np.float32)
    for g in range(SSD_GROUPS):
        for r in range(SSD_HEADS_PER_GROUP):
            e[g, g * SSD_HEADS_PER_GROUP + r, r * SSD_HEAD_DIM:(r + 1) * SSD_HEAD_DIM] = 1.0
    return jnp.asarray(np.tile(e, (1, 3, 1)), dtype=BF16)


def _merge_kernel(oa_ref, os_ref, wa_ref, ws_ref, ga_ref, gs_ref, o_ref):
    ya = jnp.dot(oa_ref[...], wa_ref[...], preferred_element_type=F32)
    ys = jnp.dot(os_ref[...], ws_ref[...], preferred_element_type=F32)
    merged = jax.nn.sigmoid(ga_ref[...].astype(F32)) * ya + jax.nn.sigmoid(gs_ref[...].astype(F32)) * ys
    o_ref[...] = merged.astype(o_ref.dtype)


def _merge(o_att, o_ssd, proj, wa, ws, gate_col, tm, tn):
    t, ka = o_att.shape
    ks = o_ssd.shape[1]
    d = wa.shape[1]
    ga_blk = gate_col // tn
    gs_blk = (gate_col + d) // tn
    return pl.pallas_call(
        _merge_kernel,
        grid=(t // tm, d // tn),
        in_specs=[pl.BlockSpec((tm, ka), lambda i, j: (i, 0)),
                  pl.BlockSpec((tm, ks), lambda i, j: (i, 0)),
                  pl.BlockSpec((ka, tn), lambda i, j: (0, j)),
                  pl.BlockSpec((ks, tn), lambda i, j: (0, j)),
                  pl.BlockSpec((tm, tn), lambda i, j: (i, ga_blk + j)),
                  pl.BlockSpec((tm, tn), lambda i, j: (i, gs_blk + j))],
        out_specs=pl.BlockSpec((tm, tn), lambda i, j: (i, j)),
        out_shape=jax.ShapeDtypeStruct((t, d), BF16),
        compiler_params=_params(("parallel", "arbitrary")),
        name="branch_merge",
    )(o_att, o_ssd, wa, ws, proj, proj)


def _pack_bf16_pair(v):
    n = v.shape[1] // 2
    lo = lax.bitcast_convert_type(v[:, :n].astype(BF16).astype(F32), jnp.uint32)
    hi = lax.bitcast_convert_type(v[:, n:].astype(BF16).astype(F32), jnp.uint32)
    return (lo >> 16) | (hi & jnp.uint32(0xFFFF0000))


def _unpack_bf16_pair(p):
    lo = lax.bitcast_convert_type(p << 16, F32)
    hi = lax.bitcast_convert_type(p & jnp.uint32(0xFFFF0000), F32)
    return lo, hi


def _wo_kernel(m_ref, x_ref, w_ref, g1_ref, ng_ref, sc_ref, sh_ref, rw_ref, rb_ref, xo_ref, hp_ref, lg_ref):
    y = jnp.dot(m_ref[...], w_ref[...], preferred_element_type=F32)
    x = x_ref[...] + g1_ref[...] * y
    xo_ref[...] = x
    hn = x * lax.rsqrt(jnp.mean(x * x, axis=-1, keepdims=True) + NORM_EPS) * ng_ref[...]
    h2 = hn * (1.0 + sc_ref[...]) + sh_ref[...]
    hp_ref[...] = _pack_bf16_pair(h2)
    h_hi = h2.astype(BF16)
    h_lo = (h2 - h_hi.astype(F32)).astype(BF16)
    both = jnp.dot(h_hi, rw_ref[...], preferred_element_type=F32)
    cross = jnp.dot(h_lo, rw_ref[:, pl.ds(0, LANES)], preferred_element_type=F32)
    lg_ref[...] = both[:, :LANES] + (both[:, LANES:] + cross) + rb_ref[...]


def _wo_residual(merged, xf, w_o, mod_l, norm2_g, router_w, router_b, seq, tm):
    t, d = xf.shape
    ne = router_w.shape[1]
    tpb = seq // tm
    mod_spec = lambda idx: pl.BlockSpec((None, 1, d), lambda i: (i // tpb, 0, idx))
    rw = jnp.pad(router_w, ((0, 0), (0, LANES - ne)))
    rw_hi = rw.astype(BF16)
    rw = jnp.concatenate([rw_hi, (rw - rw_hi.astype(F32)).astype(BF16)], axis=1)
    rb = jnp.pad(router_b, (0, LANES - ne), constant_values=-jnp.inf).reshape(1, LANES)
    return pl.pallas_call(
        _wo_kernel,
        grid=(t // tm,),
        in_specs=[pl.BlockSpec((tm, d), lambda i: (i, 0)),
                  pl.BlockSpec((tm, d), lambda i: (i, 0)),
                  pl.BlockSpec((d, d), lambda i: (0, 0)),
                  mod_spec(2),
                  pl.BlockSpec((1, d), lambda i: (0, 0)),
                  mod_spec(4), mod_spec(3),
                  pl.BlockSpec((d, 2 * LANES), lambda i: (0, 0)),
                  pl.BlockSpec((1, LANES), lambda i: (0, 0))],
        out_specs=[pl.BlockSpec((tm, d), lambda i: (i, 0)), pl.BlockSpec((tm, d // 2), lambda i: (i, 0)),
                   pl.BlockSpec((tm, LANES), lambda i: (i, 0))],
        out_shape=[jax.ShapeDtypeStruct((t, d), F32), jax.ShapeDtypeStruct((t, d // 2), jnp.uint32),
                   jax.ShapeDtypeStruct((t, LANES), F32)],
        compiler_params=_params(("parallel",)),
        name="wo_residual",
    )(merged, xf, w_o, mod_l, norm2_g.reshape(1, d), mod_l, mod_l, rw, rb)


def _router_kernel(lg_ref, idx_ref, gate_ref, rank_ref, cnt_ref, run_ref, *, tm):
    @pl.when(pl.program_id(0) == 0)
    def _():
        run_ref[...] = jnp.zeros(run_ref.shape, F32)

    logits = lg_ref[...]
    lane = lax.broadcasted_iota(jnp.int32, (tm, LANES), 1)
    vals = logits
    picked = jnp.zeros((tm, LANES), F32)
    top_v, top_sel, top_i = [], [], []
    for _ in range(TOP_K):
        m = jnp.max(vals, axis=-1, keepdims=True)
        idx = jnp.min(jnp.where(vals == m, lane, LANES), axis=-1, keepdims=True)
        sel = lane == idx
        top_v.append(m)
        top_i.append(idx)
        top_sel.append(sel)
        vals = jnp.where(sel, -jnp.inf, vals)
        picked = picked + sel.astype(F32)

    row = lax.broadcasted_iota(jnp.int32, (tm, tm), 0)
    col = lax.broadcasted_iota(jnp.int32, (tm, tm), 1)
    before = jnp.dot((col < row).astype(BF16), picked.astype(BF16), preferred_element_type=F32) + run_ref[...]
    run_ref[...] = run_ref[...] + jnp.sum(picked, axis=0, keepdims=True)
    cnt_ref[...] = run_ref[...]

    exps = [jnp.exp(v - top_v[0]) for v in top_v]
    denom = exps[0] + exps[1] + exps[2] + exps[3]
    idx_out = jnp.zeros((tm, LANES), jnp.int32)
    rank_out = jnp.zeros((tm, LANES), jnp.int32)
    gate_out = jnp.zeros((tm, LANES), F32)
    for k in range(TOP_K):
        rank_k = jnp.sum(jnp.where(top_sel[k], before, 0.0), axis=-1, keepdims=True).astype(jnp.int32)
        idx_out = jnp.where(lane == k, top_i[k], idx_out)
        rank_out = jnp.where(lane == k, rank_k, rank_out)
        gate_out = jnp.where(lane == k, exps[k] / denom, gate_out)
    idx_ref[...] = idx_out
    rank_ref[...] = rank_out
    gate_ref[...] = gate_out


def _router(logits, tm):
    t = logits.shape[0]
    kern = functools.partial(_router_kernel, tm=tm)
    tok_spec = pl.BlockSpec((tm, LANES), lambda i: (i, 0))
    return pl.pallas_call(
        kern,
        grid=(t // tm,),
        in_specs=[tok_spec],
        out_specs=[tok_spec, tok_spec, tok_spec, pl.BlockSpec((1, LANES), lambda i: (0, 0))],
        out_shape=[jax.ShapeDtypeStruct((t, LANES), jnp.int32), jax.ShapeDtypeStruct((t, LANES), F32),
                   jax.ShapeDtypeStruct((t, LANES), jnp.int32), jax.ShapeDtypeStruct((1, LANES), F32)],
        scratch_shapes=[pltpu.VMEM((1, LANES), F32)],
        compiler_params=_params(("arbitrary",)),
        name="router",
    )(logits)


def _dispatch_kernel(dest_ref, h_ref, xs_in_hbm, xs_hbm, sem, *, tm):
    del xs_in_hbm

    def row_copy(t, k):
        return pltpu.make_async_copy(h_ref.at[pl.ds(t, 1)],
                                     xs_hbm.at[pl.ds(dest_ref[t * TOP_K + k], 1)], sem)

    def issue(t, carry):
        for k in range(TOP_K):
            row_copy(t, k).start()
        return carry

    lax.fori_loop(0, tm, issue, 0)
    for _ in range(TOP_K):
        pltpu.make_async_copy(h_ref, xs_hbm.at[pl.ds(0, tm)], sem).wait()


def _dispatch(h2, dest_flat, n_rows, tm):
    t, d = h2.shape
    kern = functools.partial(_dispatch_kernel, tm=tm)
    return pl.pallas_call(
        kern,
        grid=(t // tm,),
        in_specs=[pl.BlockSpec((tm * TOP_K,), lambda i: (i,), memory_space=pltpu.SMEM),
                  pl.BlockSpec((tm, d), lambda i: (i, 0)),
                  pl.BlockSpec(memory_space=pl.ANY)],
        out_specs=pl.BlockSpec(memory_space=pl.ANY),
        out_shape=jax.ShapeDtypeStruct((n_rows, d), h2.dtype),
        scratch_shapes=[pltpu.SemaphoreType.DMA(())],
        input_output_aliases={2: 0},
        compiler_params=_params(("arbitrary",)),
        name="moe_dispatch",
    )(dest_flat, h2, jnp.zeros((n_rows, d), h2.dtype))


def _expert_gu_kernel(be_ref, nv_ref, x_ref, w_ref, b_ref, o_ref, *, ff):
    del be_ref
    valid = pl.program_id(0) < nv_ref[0]

    @pl.when(valid)
    def _():
        x_lo, x_hi = _unpack_bf16_pair(x_ref[...])
        half = x_lo.shape[1]
        gu = (jnp.dot(x_lo.astype(BF16), w_ref[pl.ds(0, half), :], preferred_element_type=F32)
              + jnp.dot(x_hi.astype(BF16), w_ref[pl.ds(half, half), :], preferred_element_type=F32) + b_ref[...])
        g = jnp.minimum(gu[:, :ff], SWIGLU_LIMIT)
        u = jnp.clip(gu[:, ff:], -SWIGLU_LIMIT, SWIGLU_LIMIT)
        o_ref[...] = ((u + 1.0) * (g * jax.nn.sigmoid(SWIGLU_ALPHA * g))).astype(o_ref.dtype)

    @pl.when(jnp.logical_not(valid))
    def _():
        o_ref[...] = jnp.zeros(o_ref.shape, o_ref.dtype)


def _expert_down_kernel(be_ref, nv_ref, a_ref, w_ref, b_ref, o_ref):
    del be_ref
    valid = pl.program_id(0) < nv_ref[0]

    @pl.when(valid)
    def _():
        o_ref[...] = _pack_bf16_pair(jnp.dot(a_ref[...], w_ref[...], preferred_element_type=F32) + b_ref[...])

    @pl.when(jnp.logical_not(valid))
    def _():
        o_ref[...] = jnp.zeros(o_ref.shape, o_ref.dtype)


def _expert_ffn(xs, block_e, n_valid, w_gu, b_gu, w_dn, b_dn):
    n_rows, dp = xs.shape
    bm = EXPERT_BLOCK
    n_blocks = n_rows // bm
    ne, d, ff2 = w_gu.shape
    ff = ff2 // 2
    act = pl.pallas_call(
        functools.partial(_expert_gu_kernel, ff=ff),
        grid_spec=pltpu.PrefetchScalarGridSpec(
            num_scalar_prefetch=2, grid=(n_blocks,),
            in_specs=[pl.BlockSpec((bm, dp), lambda i, be, nv: (i, 0)),
                      pl.BlockSpec((None, d, ff2), lambda i, be, nv: (be[i], 0, 0)),
                      pl.BlockSpec((None, 1, ff2), lambda i, be, nv: (be[i], 0, 0))],
            out_specs=pl.BlockSpec((bm, ff), lambda i, be, nv: (i, 0))),
        out_shape=jax.ShapeDtypeStruct((n_rows, ff), BF16),
        compiler_params=_params(("arbitrary",)),
        name="expert_gate_up",
    )(block_e, n_valid, xs, w_gu, b_gu.reshape(ne, 1, ff2))
    return pl.pallas_call(
        _expert_down_kernel,
        grid_spec=pltpu.PrefetchScalarGridSpec(
            num_scalar_prefetch=2, grid=(n_blocks,),
            in_specs=[pl.BlockSpec((bm, ff), lambda i, be, nv: (i, 0)),
                      pl.BlockSpec((None, ff, d), lambda i, be, nv: (be[i], 0, 0)),
                      pl.BlockSpec((None, 1, d), lambda i, be, nv: (be[i], 0, 0))],
            out_specs=pl.BlockSpec((bm, dp), lambda i, be, nv: (i, 0))),
        out_shape=jax.ShapeDtypeStruct((n_rows, dp), jnp.uint32),
        compiler_params=_params(("arbitrary",)),
        name="expert_down",
    )(block_e, n_valid, act, w_dn, b_dn.reshape(ne, 1, d))


def _combine_kernel(dest_ref, ys_hbm, x_ref, gate_ref, g2_ref, fg_ref, o_ref, buf, sem, *, tm, final):
    def row_copy(t, k):
        return pltpu.make_async_copy(ys_hbm.at[pl.ds(dest_ref[t * TOP_K + k], 1)],
                                     buf.at[k, pl.ds(t, 1)], sem)

    def issue(t, carry):
        for k in range(TOP_K):
            row_copy(t, k).start()
        return carry

    lax.fori_loop(0, tm, issue, 0)
    for k in range(TOP_K):
        pltpu.make_async_copy(ys_hbm.at[pl.ds(0, tm)], buf.at[k], sem).wait()

    gates = gate_ref[...]
    moe_lo, moe_hi = (gates[:, 0:1] * part for part in _unpack_bf16_pair(buf[0]))
    for k in range(1, TOP_K):
        lo, hi = _unpack_bf16_pair(buf[k])
        moe_lo = moe_lo + gates[:, k:k + 1] * lo
        moe_hi = moe_hi + gates[:, k:k + 1] * hi
    x = x_ref[...] + g2_ref[...] * jnp.concatenate([moe_lo, moe_hi], axis=1)
    if final:
        x = x * lax.rsqrt(jnp.mean(x * x, axis=-1, keepdims=True) + NORM_EPS) * fg_ref[...]
    o_ref[...] = x


def _combine(ys, dest_flat, xf, gates, mod_l, final_g, seq, tm, final):
    t, d = xf.shape
    tpb = seq // tm
    kern = functools.partial(_combine_kernel, tm=tm, final=final)
    return pl.pallas_call(
        kern,
        grid=(t // tm,),
        in_specs=[pl.BlockSpec((tm * TOP_K,), lambda i: (i,), memory_space=pltpu.SMEM),
                  pl.BlockSpec(memory_space=pl.ANY),
                  pl.BlockSpec((tm, d), lambda i: (i, 0)),
                  pl.BlockSpec((tm, LANES), lambda i: (i, 0)),
                  pl.BlockSpec((None, 1, d), lambda i: (i // tpb, 0, 5)),
                  pl.BlockSpec((1, d), lambda i: (0, 0))],
        out_specs=pl.BlockSpec((tm, d), lambda i: (i, 0)),
        out_shape=jax.ShapeDtypeStruct((t, d), F32),
        scratch_shapes=[pltpu.VMEM((TOP_K, tm, d // 2), jnp.uint32), pltpu.SemaphoreType.DMA(())],
        compiler_params=_params(("arbitrary",)),
        name="moe_combine",
    )(dest_flat, ys, xf, gates, mod_l, final_g.reshape(1, d))


def _moe(h2p, logits, xf, mod_l, w_gu, b_gu, w_dn, b_dn, final_g, seq, final):
    t = h2p.shape[0]
    ne = w_gu.shape[0]
    bm = EXPERT_BLOCK
    top_i, gates, rank, counts = _router(logits, tm=256)
    cnt = counts[0, :ne].astype(jnp.int32)
    padded = (cnt + bm - 1) // bm * bm
    pad_end = jnp.cumsum(padded)
    pad_start = pad_end - padded
    experts = jnp.arange(ne, dtype=jnp.int32)
    start_of = jnp.sum(jnp.where(top_i[:, :TOP_K, None] == experts, pad_start, 0), axis=-1)
    dest = start_of + rank[:, :TOP_K]
    n_blocks = t * TOP_K // bm + ne
    blk_row = jnp.arange(n_blocks, dtype=jnp.int32)[:, None] * bm
    block_e = jnp.minimum(jnp.sum((pad_end[None, :] <= blk_row).astype(jnp.int32), axis=1), ne - 1)
    n_valid = (pad_end[-1:] // bm).astype(jnp.int32)
    dest_flat = dest.reshape(-1).astype(jnp.int32)

    xs = _dispatch(h2p, dest_flat, n_blocks * bm, tm=256)
    ys = _expert_ffn(xs, block_e, n_valid, w_gu, b_gu, w_dn, b_dn)
    return _combine(ys, dest_flat, xf, gates, mod_l, final_g, seq, tm=128, final=final)


def kernel(x, c, ada_w, ada_b, norm1_g, w_in, rel_bias, lam_q1, lam_k1, lam_q2, lam_k2, attn_subln_g, conv_w, conv_b, dt_bias, a_log, d_skip, ssd_norm_g, w_branch, w_o, norm2_g, router_w, router_b, w_gate_up, b_gate_up, w_down, b_down, final_g):
    batch, seq, d = x.shape
    t = batch * seq
    depth = ada_w.shape[0]
    att_w = N_ATT_HEADS * ATT_V_DIM
    ssd_w = SSD_GROUPS * GROUP_W
    n_heads_ssd = SSD_GROUPS * SSD_HEADS_PER_GROUP
    conv_ch = conv_w.shape[2]
    q_cols = 2 * N_ATT_HEADS * ATT_HEAD_DIM
    q_scale = ATT_HEAD_DIM ** -0.5 * math.log2(math.e)
    z_col = 2 * q_cols + att_w
    xbc_col = z_col + ssd_w
    dt_col = xbc_col + conv_ch
    gate_col = dt_col

    mod = _adaln(c, ada_w, ada_b)
    bias_tiles, far = _attn_bias_tiles(rel_bias, seq, ATT_Q_BLOCK, ATT_KV_BLOCK)
    expand = _head_expand()
    xf = x.reshape(t, d)
    tm_big = min(1024, seq)

    for layer in range(depth):
        mod_l = mod[layer].reshape(batch, 1, 6 * d)
        w_l = w_in[layer]
        w_main = jnp.concatenate([w_l[:, :q_cols] * q_scale, w_l[:, q_cols:dt_col],
                                  w_l[:, dt_col + n_heads_ssd:]], axis=1).astype(BF16)
        w_dt = jnp.pad(w_l[:, dt_col:dt_col + n_heads_ssd], ((0, 0), (0, LANES - n_heads_ssd))).astype(BF16)

        proj = _normmod_matmul(xf, norm1_g[layer], mod_l, 1, 0, w_main, BF16, seq, tm=tm_big, tn=512)
        dt_raw = _normmod_matmul(xf, norm1_g[layer], mod_l, 1, 0, w_dt, F32, seq, tm=tm_big, tn=LANES)

        lam_init = 0.8 - 0.6 * math.exp(-0.3 * layer)
        lamv = jnp.stack([lam_q1[layer], lam_k1[layer], lam_q2[layer], lam_k2[layer]], axis=0)
        o_att = _attention(proj, bias_tiles, far, lamv, attn_subln_g[layer], batch, seq, lam_init)
        o_ssd = _ssd(proj, dt_raw, expand, conv_w[layer], conv_b[layer], dt_bias[layer], a_log[layer],
                     d_skip[layer], ssd_norm_g[layer], batch, seq, z_col, xbc_col)

        wa = w_branch[layer, :att_w].astype(BF16)
        ws = w_branch[layer, att_w:].astype(BF16)
        merged = _merge(o_att, o_ssd, proj, wa, ws, gate_col, tm=tm_big, tn=256)
        xf, h2p, logits = _wo_residual(merged, xf, w_o[layer].astype(BF16), mod_l, norm2_g[layer],
                                       router_w[layer], router_b[layer], seq, tm=256)

        xf = _moe(h2p, logits, xf, mod_l, w_gate_up[layer].astype(BF16), b_gate_up[layer],
                  w_down[layer].astype(BF16), b_down[layer], final_g, seq, final=(layer == depth - 1))
    return xf.reshape(batch, seq, d)
```

```python
import functools
import math

import numpy as np
import jax
import jax.numpy as jnp
from jax import lax
from jax.experimental import pallas as pl
from jax.experimental.pallas import tpu as pltpu

F32 = jnp.float32
BF16 = jnp.bfloat16
HIGHEST = lax.Precision.HIGHEST

DEPTH = 2
CHUNK = 64
N_ATT_HEADS = 8
ATT_HEAD_DIM = 128
ATT_V_DIM = 2 * ATT_HEAD_DIM
REL_BUCKETS = 32
REL_MAX_DIST = 128
SSD_HEAD_DIM = 64
SSD_GROUPS = 8
SSD_HEADS_PER_GROUP = 8
SSD_STATE = 128
SSD_CONV = 4
N_EXPERTS = 32
TOP_K = 4
SWIGLU_LIMIT = 7.0
SWIGLU_ALPHA = 1.702
NORM_EPS = 1e-6
SUBLN_EPS = 1e-5

LANES = 128
SUBLANES = 8
BF16_ROWS = 16
VMEM_LIMIT = 56 * 1024 * 1024

ATT_Q_BLOCK = 512
ATT_KV_BLOCK = 256
SSD_BLOCK = 256
EXPERT_BLOCK = 256
GROUP_W = SSD_HEADS_PER_GROUP * SSD_HEAD_DIM


def _params(semantics):
    return pltpu.CompilerParams(dimension_semantics=semantics, vmem_limit_bytes=VMEM_LIMIT)


def _silu(v):
    half = 0.5 * v
    return half + half * jnp.tanh(half)


def _split3(v):
    hi = v.astype(BF16)
    rest = v - hi.astype(F32)
    mid = rest.astype(BF16)
    lo = (rest - mid.astype(F32)).astype(BF16)
    return hi, mid, lo


def _adaln_kernel(c_ref, w_ref, b_ref, o_ref):
    ca = _silu(c_ref[...])
    o_ref[...] = jnp.dot(ca, w_ref[...], precision=HIGHEST, preferred_element_type=F32) + b_ref[...]


def _adaln(c, ada_w, ada_b):
    nl, d, n = ada_w.shape
    b = c.shape[0]
    tn = 1024
    return pl.pallas_call(
        _adaln_kernel,
        grid=(nl, n // tn),
        in_specs=[pl.BlockSpec((b, d), lambda l, j: (0, 0)),
                  pl.BlockSpec((None, d, tn), lambda l, j: (l, 0, j)),
                  pl.BlockSpec((None, 1, tn), lambda l, j: (l, 0, j))],
        out_specs=pl.BlockSpec((None, b, tn), lambda l, j: (l, 0, j)),
        out_shape=jax.ShapeDtypeStruct((nl, b, n), F32),
        compiler_params=_params(("parallel", "parallel")),
        name="adaln",
    )(c, ada_w, ada_b.reshape(nl, 1, n))


def _normmod_matmul_kernel(x_ref, g_ref, sc_ref, sh_ref, w_ref, o_ref, h_ref):
    @pl.when(pl.program_id(1) == 0)
    def _():
        x = x_ref[...]
        y = x * lax.rsqrt(jnp.mean(x * x, axis=-1, keepdims=True) + NORM_EPS) * g_ref[...]
        h_ref[...] = (y * (1.0 + sc_ref[...]) + sh_ref[...]).astype(BF16)

    o_ref[...] = jnp.dot(h_ref[...], w_ref[...], preferred_element_type=F32).astype(o_ref.dtype)


def _normmod_matmul(xf, g, mod_l, sc_idx, sh_idx, w, out_dtype, seq, tm, tn):
    t, d = xf.shape
    n = w.shape[1]
    tpb = seq // tm
    return pl.pallas_call(
        _normmod_matmul_kernel,
        grid=(t // tm, n // tn),
        in_specs=[pl.BlockSpec((tm, d), lambda i, j: (i, 0)),
                  pl.BlockSpec((1, d), lambda i, j: (0, 0)),
                  pl.BlockSpec((None, 1, d), lambda i, j: (i // tpb, 0, sc_idx)),
                  pl.BlockSpec((None, 1, d), lambda i, j: (i // tpb, 0, sh_idx)),
                  pl.BlockSpec((d, tn), lambda i, j: (0, j))],
        out_specs=pl.BlockSpec((tm, tn), lambda i, j: (i, j)),
        out_shape=jax.ShapeDtypeStruct((t, n), out_dtype),
        scratch_shapes=[pltpu.VMEM((tm, d), BF16)],
        compiler_params=_params(("parallel", "arbitrary")),
        name="inproj",
    )(xf, g.reshape(1, d), mod_l, mod_l, w)


def _t5_bucket(rel):
    half = REL_BUCKETS // 2
    exact = half // 2
    ret = jnp.where(rel > 0, half, 0)
    n = jnp.abs(rel)
    nf = jnp.maximum(n, 1).astype(jnp.float32)
    large = exact + (jnp.log(nf / exact) / math.log(REL_MAX_DIST / exact) * (half - exact)).astype(jnp.int32)
    large = jnp.minimum(large, half - 1)
    return ret + jnp.where(n < exact, n, large)


def _attn_bias_tiles(rel_bias, seq, qb, kb):
    assert kb >= REL_MAX_DIST and kb % CHUNK == 0 and qb % kb == 0
    nh = rel_bias.shape[1]
    lo = kb + qb - 1
    rel = jnp.arange(-lo, qb, dtype=jnp.int32)
    vec = (rel_bias[_t5_bucket(rel)].astype(F32) * math.log2(math.e)).T
    period = qb + kb
    jj = np.arange(period)
    jj = np.where(jj < qb, jj, jj - period)
    qq = np.arange(qb)[None, :]
    tiles = []
    for off in range(-1, qb // kb):
        row = vec[:, np.clip(lo + off * kb - jj, 0, lo + qb - 1)]
        skew = jnp.tile(row, (1, kb))[:, :kb * (period - 1)].reshape(nh, kb, period - 1)[:, :, :qb]
        kk = off * kb + np.arange(kb)[:, None]
        visible = (kk // CHUNK) <= (qq // CHUNK)
        tiles.append(jnp.where(visible[None], skew, -jnp.inf))
    far = rel_bias[_t5_bucket(jnp.full((1,), -(seq - 1), jnp.int32))][0].astype(F32)
    return jnp.stack(tiles, axis=1), far


def _attn_kernel(far_ref, q1_ref, q2_ref, k1_ref, k2_ref, v_ref, bias_ref, lam_ref, g_ref, o_ref,
                 vt_ref, acc1_ref, acc2_ref, sa1_ref, sa2_ref, sb1_ref, sb2_ref, *, qb, kb, dv, lam_init):
    h = pl.program_id(1)
    i = pl.program_id(2)
    far = far_ref[h] * math.log2(math.e)
    nt_dims = (((1,), (1,)), ((), ()))
    per_tile = qb // kb
    pad_rows = vt_ref.shape[1] - dv
    s_a = (sa1_ref, sa2_ref)
    s_b = (sb1_ref, sb2_ref)

    @pl.when(i == 0)
    def _():
        rr = lax.broadcasted_iota(jnp.int32, (dv, dv), 0)
        cc = lax.broadcasted_iota(jnp.int32, (dv, dv), 1)
        eye = jnp.where(rr == cc, 1.0, 0.0).astype(BF16)
        ones_row = jnp.where(lax.broadcasted_iota(jnp.int32, (pad_rows, kb), 0) == 0, 1.0, 0.0).astype(BF16)

        def tr_body(j, carry):
            vj = v_ref[pl.ds(pl.multiple_of(j * kb, kb), kb), :]
            vt_ref[j, pl.ds(0, dv), :] = vj.T
            vt_ref[j, pl.ds(dv, pad_rows), :] = ones_row
            return carry

        lax.fori_loop(0, vt_ref.shape[0], tr_body, 0)

    acc1_ref[...] = jnp.zeros(acc1_ref.shape, F32)
    acc2_ref[...] = jnp.zeros(acc2_ref.shape, F32)

    def scores(j):
        ks = pl.ds(pl.multiple_of(j * kb, kb), kb)
        return (lax.dot_general(k1_ref[ks, :], q1_ref[...], nt_dims, preferred_element_type=F32),
                lax.dot_general(k2_ref[ks, :], q2_ref[...], nt_dims, preferred_element_type=F32))

    def softmax_step(s, m, bias, shift):
        if bias is not None:
            s = s + bias
        m_new = jnp.maximum(m, jnp.max(s, axis=0, keepdims=True) + shift)
        return m_new, jnp.exp2(m - m_new), jnp.exp2(s - (m_new - shift)).astype(BF16)

    def step(carry, j, src, dst, bias, shift):
        m1, m2 = carry
        if dst is not None:
            dst[0][...], dst[1][...] = scores(j + 1)
        m1, alpha1, p1 = softmax_step(src[0][...], m1, bias, shift)
        pv1 = jnp.dot(vt_ref[j], p1, preferred_element_type=F32)
        m2, alpha2, p2 = softmax_step(src[1][...], m2, bias, shift)
        pv2 = jnp.dot(vt_ref[j], p2, preferred_element_type=F32)
        acc1_ref[...] = acc1_ref[...] * alpha1 + pv1
        acc2_ref[...] = acc2_ref[...] * alpha2 + pv2
        return m1, m2

    def pair(carry, j, kinds, final=False):
        (bias_a, shift_a), (bias_b, shift_b) = kinds
        carry = step(carry, j, s_a, s_b, bias_a, shift_a)
        return step(carry, j + 1, s_b, None if final else s_a, bias_b, shift_b)

    assert per_tile == 2
    far_kind = (None, far)
    neg = jnp.full((1, qb), -jnp.inf, F32)
    s_a[0][...], s_a[1][...] = scores(0)
    n_ff = jnp.maximum(i - 1, 0)
    carry = lax.fori_loop(0, n_ff, lambda t, c: pair(c, 2 * t, (far_kind, far_kind)), (neg, neg))
    carry = lax.fori_loop(n_ff, i, lambda t, c: pair(c, 2 * t, (far_kind, (bias_ref[0], 0.0))), carry)
    pair(carry, 2 * i, ((bias_ref[1], 0.0), (bias_ref[2], 0.0)), final=True)

    lamv = lam_ref[...]
    lam = (jnp.exp(jnp.sum(lamv[0:1] * lamv[1:2], axis=-1, keepdims=True))
           - jnp.exp(jnp.sum(lamv[2:3] * lamv[3:4], axis=-1, keepdims=True)) + lam_init)
    a1 = acc1_ref[...]
    a2 = acc2_ref[...]
    o = (a1[:dv] / a1[dv:dv + 1] - lam * (a2[:dv] / a2[dv:dv + 1])).T
    y = o * lax.rsqrt(jnp.mean(o * o, axis=-1, keepdims=True) + SUBLN_EPS) * g_ref[...]
    o_ref[...] = (y * (1.0 - lam_init)).astype(o_ref.dtype)


def _attention(proj, bias_tiles, far, lamv, subln_g, batch, seq, lam_init):
    qb, kb = ATT_Q_BLOCK, ATT_KV_BLOCK
    nq = seq // qb
    nh = N_ATT_HEADS
    dk, dv = ATT_HEAD_DIM, ATT_V_DIM
    t = batch * seq
    k_off = 2 * nh
    v_off = 4 * nh * dk // dv
    kern = functools.partial(_attn_kernel, qb=qb, kb=kb, dv=dv, lam_init=lam_init)
    return pl.pallas_call(
        kern,
        grid=(batch, nh, nq),
        in_specs=[pl.BlockSpec(memory_space=pltpu.SMEM),
                  pl.BlockSpec((qb, dk), lambda b, h, i: (b * nq + i, h)),
                  pl.BlockSpec((qb, dk), lambda b, h, i: (b * nq + i, nh + h)),
                  pl.BlockSpec((seq, dk), lambda b, h, i: (b, k_off + h)),
                  pl.BlockSpec((seq, dk), lambda b, h, i: (b, k_off + nh + h)),
                  pl.BlockSpec((seq, dv), lambda b, h, i: (b, v_off + h)),
                  pl.BlockSpec((None, qb // kb + 1, kb, qb), lambda b, h, i: (h, 0, 0, 0)),
                  pl.BlockSpec((4, dk), lambda b, h, i: (0, 0)),
                  pl.BlockSpec((1, dv), lambda b, h, i: (0, 0))],
        out_specs=pl.BlockSpec((qb, dv), lambda b, h, i: (b * nq + i, h)),
        out_shape=jax.ShapeDtypeStruct((t, nh * dv), BF16),
        scratch_shapes=[pltpu.VMEM((seq // kb, dv + BF16_ROWS, kb), BF16),
                        pltpu.VMEM((dv + BF16_ROWS, qb), F32), pltpu.VMEM((dv + BF16_ROWS, qb), F32)]
        + [pltpu.VMEM((kb, qb), F32)] * 4,
        compiler_params=_params(("parallel", "parallel", "arbitrary")),
        name="diff_attention",
    )(far, proj, proj, proj, proj, proj, bias_tiles, lamv, subln_g.reshape(1, dv))


def _conv_silu(x_ref, hist_ref, w_ref, b_ref, blk):
    x = x_ref[...].astype(F32)
    hist_ref[pl.ds(SUBLANES, blk), :] = x
    w = w_ref[...]
    acc = x * w[SSD_CONV - 1:SSD_CONV] + b_ref[...]
    for j in range(1, SSD_CONV):
        acc = acc + hist_ref[pl.ds(SUBLANES - j, blk), :] * w[SSD_CONV - 1 - j:SSD_CONV - j]
    hist_ref[pl.ds(0, SUBLANES), :] = x[blk - SUBLANES:]
    return _silu(acc)


def _ssd_kernel(xs_ref, bm_ref, cm_ref, z_ref, dt_ref, e_ref, cwx_ref, cwb_ref, cwc_ref, cbx_ref, cbb_ref,
                cbc_ref, dtb_ref, alog_ref, dskip_ref, ng_ref, o_ref,
                state_ref, tx_ref, tb_ref, tc_ref, act_ref, *, blk):
    g = pl.program_id(1)

    @pl.when(pl.program_id(2) == 0)
    def _():
        state_ref[...] = jnp.zeros(state_ref.shape, F32)
        for hist_ref in (tx_ref, tb_ref, tc_ref):
            hist_ref[pl.ds(0, SUBLANES), :] = jnp.zeros((SUBLANES, hist_ref.shape[1]), F32)

    xs = _conv_silu(xs_ref, tx_ref, cwx_ref, cbx_ref, blk)
    bm = _conv_silu(bm_ref, tb_ref, cwb_ref, cbb_ref, blk)
    cm = _conv_silu(cm_ref, tc_ref, cwc_ref, cbc_ref, blk)

    dt = jax.nn.softplus(dt_ref[...] + dtb_ref[...])
    da = dt * (-jnp.exp(alog_ref[...]))
    row = lax.broadcasted_iota(jnp.int32, (blk, blk), 0)
    col = lax.broadcasted_iota(jnp.int32, (blk, blk), 1)
    causal = col <= row
    tri = jnp.where(causal, 1.0, 0.0).astype(BF16)
    acum = sum(jnp.dot(tri, part, preferred_element_type=F32) for part in _split3(da))
    both_e = jnp.dot(jnp.concatenate([jnp.concatenate(_split3(dt), axis=1),
                                      jnp.concatenate(_split3(acum), axis=1)], axis=0),
                     e_ref[...], preferred_element_type=F32)
    dt_e = both_e[:blk]
    acum_e = both_e[blk:]
    act_ref[...] = acum.T

    xdt = xs * dt_e
    xdt_b = xdt.astype(BF16)
    cm_b = cm.astype(BF16)
    cb = lax.dot_general(cm_b, bm.astype(BF16), (((1,), (1,)), ((), ())), preferred_element_type=F32)
    lane = lax.broadcasted_iota(jnp.int32, (blk, LANES), 1)
    halves = (lane < SSD_HEAD_DIM, lane >= SSD_HEAD_DIM)
    pieces = []
    for pair in range(GROUP_W // LANES):
        xp = xdt_b[:, pair * LANES:(pair + 1) * LANES]
        yp = jnp.zeros((blk, LANES), F32)
        for hh in range(2):
            r = 2 * pair + hh
            a_col = acum_e[:, r * SSD_HEAD_DIM:r * SSD_HEAD_DIM + 1]
            a_row = act_ref[pl.ds(g * SSD_HEADS_PER_GROUP + r, 1), :]
            decay = jnp.exp(jnp.where(causal, a_col - a_row, -jnp.inf))
            mat = (cb * decay).astype(BF16)
            yp = yp + jnp.dot(mat, jnp.where(halves[hh], xp, jnp.zeros_like(xp)), preferred_element_type=F32)
        pieces.append(yp)
    y = jnp.concatenate(pieces, axis=1)

    st = state_ref[...]
    y = y + jnp.exp(acum_e) * jnp.dot(cm_b, st.astype(BF16), preferred_element_type=F32)
    a_last = acum_e[blk - 1:blk, :]
    wgt = (xdt * jnp.exp(a_last - acum_e)).astype(BF16)
    state_ref[...] = st * jnp.exp(a_last) + jnp.dot(bm.T.astype(BF16), wgt, preferred_element_type=F32)

    y = y + dskip_ref[...] * xs
    y = y * _silu(z_ref[...].astype(F32))
    y = y * lax.rsqrt(jnp.mean(y * y, axis=-1, keepdims=True) + SUBLN_EPS)
    o_ref[...] = (y * ng_ref[...]).astype(o_ref.dtype)


def _ssd(proj, dt_raw, expand, conv_w, conv_b, dt_bias, a_log, d_skip, norm_g, batch, seq, z_col, xbc_col):
    blk = SSD_BLOCK
    nc = seq // blk
    ng = SSD_GROUPS
    t = batch * seq
    width = ng * GROUP_W
    heads = ng * SSD_HEADS_PER_GROUP
    z_blk = z_col // GROUP_W
    xs_blk = xbc_col // GROUP_W
    b_blk = (xbc_col + width) // SSD_STATE
    c_blk = b_blk + ng
    cw_b_blk = width // SSD_STATE
    pad = LANES - heads
    row = lambda b, g, c: b * nc + c
    dtb = jnp.pad(dt_bias, (0, pad)).reshape(1, LANES)
    alog = jnp.pad(a_log, (0, pad)).reshape(1, LANES)
    dskip = jnp.repeat(d_skip, SSD_HEAD_DIM).reshape(1, width)
    cb2 = conv_b.reshape(1, -1)
    kern = functools.partial(_ssd_kernel, blk=blk)
    return pl.pallas_call(
        kern,
        grid=(batch, ng, nc),
        in_specs=[pl.BlockSpec((blk, GROUP_W), lambda b, g, c: (row(b, g, c), xs_blk + g)),
                  pl.BlockSpec((blk, SSD_STATE), lambda b, g, c: (row(b, g, c), b_blk + g)),
                  pl.BlockSpec((blk, SSD_STATE), lambda b, g, c: (row(b, g, c), c_blk + g)),
                  pl.BlockSpec((blk, GROUP_W), lambda b, g, c: (row(b, g, c), z_blk + g)),
                  pl.BlockSpec((blk, LANES), lambda b, g, c: (row(b, g, c), 0)),
                  pl.BlockSpec((None, 3 * LANES, GROUP_W), lambda b, g, c: (g, 0, 0)),
                  pl.BlockSpec((SSD_CONV, GROUP_W), lambda b, g, c: (0, g)),
                  pl.BlockSpec((SSD_CONV, SSD_STATE), lambda b, g, c: (0, cw_b_blk + g)),
                  pl.BlockSpec((SSD_CONV, SSD_STATE), lambda b, g, c: (0, cw_b_blk + ng + g)),
                  pl.BlockSpec((1, GROUP_W), lambda b, g, c: (0, g)),
                  pl.BlockSpec((1, SSD_STATE), lambda b, g, c: (0, cw_b_blk + g)),
                  pl.BlockSpec((1, SSD_STATE), lambda b, g, c: (0, cw_b_blk + ng + g)),
                  pl.BlockSpec((1, LANES), lambda b, g, c: (0, 0)),
                  pl.BlockSpec((1, LANES), lambda b, g, c: (0, 0)),
                  pl.BlockSpec((1, GROUP_W), lambda b, g, c: (0, g)),
                  pl.BlockSpec((1, GROUP_W), lambda b, g, c: (0, g))],
        out_specs=pl.BlockSpec((blk, GROUP_W), lambda b, g, c: (row(b, g, c), g)),
        out_shape=jax.ShapeDtypeStruct((t, width), BF16),
        scratch_shapes=[pltpu.VMEM((SSD_STATE, GROUP_W), F32),
                        pltpu.VMEM((SUBLANES + blk, GROUP_W), F32),
                        pltpu.VMEM((SUBLANES + blk, SSD_STATE), F32),
                        pltpu.VMEM((SUBLANES + blk, SSD_STATE), F32),
                        pltpu.VMEM((LANES, blk), F32)],
        compiler_params=_params(("parallel", "parallel", "arbitrary")),
        name="ssd",
    )(proj, proj, proj, proj, dt_raw, expand, conv_w, conv_w, conv_w, cb2, cb2, cb2, dtb, alog, dskip,
      norm_g.reshape(1, width))


def _head_expand():
    e = np.zeros((SSD_GROUPS, LANES, GROUP_W), np.float32)
    for g in range(SSD_GROUPS):
        for r in range(SSD_HEADS_PER_GROUP):
            e[g, g * SSD_HEADS_PER_GROUP + r, r * SSD_HEAD_DIM:(r + 1) * SSD_HEAD_DIM] = 1.0
    return jnp.asarray(np.tile(e, (1, 3, 1)), dtype=BF16)


def _merge_kernel(oa_ref, os_ref, wa_ref, ws_ref, ga_ref, gs_ref, o_ref):
    ya = jnp.dot(oa_ref[...], wa_ref[...], preferred_element_type=F32)
    ys = jnp.dot(os_ref[...], ws_ref[...], preferred_element_type=F32)
    merged = jax.nn.sigmoid(ga_ref[...].astype(F32)) * ya + jax.nn.sigmoid(gs_ref[...].astype(F32)) * ys
    o_ref[...] = merged.astype(o_ref.dtype)


def _merge(o_att, o_ssd, gates, wa, ws, tm, tn):
    t, ka = o_att.shape
    ks = o_ssd.shape[1]
    d = wa.shape[1]
    gs_blk = d // tn
    return pl.pallas_call(
        _merge_kernel,
        grid=(t // tm, d // tn),
        in_specs=[pl.BlockSpec((tm, ka), lambda i, j: (i, 0)),
                  pl.BlockSpec((tm, ks), lambda i, j: (i, 0)),
                  pl.BlockSpec((ka, tn), lambda i, j: (0, j)),
                  pl.BlockSpec((ks, tn), lambda i, j: (0, j)),
                  pl.BlockSpec((tm, tn), lambda i, j: (i, j)),
                  pl.BlockSpec((tm, tn), lambda i, j: (i, gs_blk + j))],
        out_specs=pl.BlockSpec((tm, tn), lambda i, j: (i, j)),
        out_shape=jax.ShapeDtypeStruct((t, d), BF16),
        compiler_params=_params(("parallel", "arbitrary")),
        name="branch_merge",
    )(o_att, o_ssd, wa, ws, gates, gates)


def _pack_bf16_pair(v):
    n = v.shape[1] // 2
    lo = lax.bitcast_convert_type(v[:, :n].astype(BF16).astype(F32), jnp.uint32)
    hi = lax.bitcast_convert_type(v[:, n:].astype(BF16).astype(F32), jnp.uint32)
    return (lo >> 16) | (hi & jnp.uint32(0xFFFF0000))


def _unpack_bf16_pair(p):
    lo = lax.bitcast_convert_type(p << 16, F32)
    hi = lax.bitcast_convert_type(p & jnp.uint32(0xFFFF0000), F32)
    return lo, hi


def _wo_kernel(m_ref, x_ref, w_ref, g1_ref, ng_ref, sc_ref, sh_ref, rw_ref, rb_ref, xo_ref, hp_ref, lg_ref):
    y = jnp.dot(m_ref[...], w_ref[...], preferred_element_type=F32)
    x = x_ref[...] + g1_ref[...] * y
    xo_ref[...] = x
    hn = x * lax.rsqrt(jnp.mean(x * x, axis=-1, keepdims=True) + NORM_EPS) * ng_ref[...]
    h2 = hn * (1.0 + sc_ref[...]) + sh_ref[...]
    hp_ref[...] = _pack_bf16_pair(h2)
    h_hi = h2.astype(BF16)
    h_lo = (h2 - h_hi.astype(F32)).astype(BF16)
    both = jnp.dot(h_hi, rw_ref[...], preferred_element_type=F32)
    cross = jnp.dot(h_lo, rw_ref[:, pl.ds(0, LANES)], preferred_element_type=F32)
    lg_ref[...] = both[:, :LANES] + (both[:, LANES:] + cross) + rb_ref[...]


def _wo_residual(merged, xf, w_o, mod_l, norm2_g, router_w, router_b, seq, tm):
    t, d = xf.shape
    ne = router_w.shape[1]
    tpb = seq // tm
    mod_spec = lambda idx: pl.BlockSpec((None, 1, d), lambda i: (i // tpb, 0, idx))
    rw = jnp.pad(router_w, ((0, 0), (0, LANES - ne)))
    rw_hi = rw.astype(BF16)
    rw = jnp.concatenate([rw_hi, (rw - rw_hi.astype(F32)).astype(BF16)], axis=1)
    rb = jnp.pad(router_b, (0, LANES - ne), constant_values=-jnp.inf).reshape(1, LANES)
    return pl.pallas_call(
        _wo_kernel,
        grid=(t // tm,),
        in_specs=[pl.BlockSpec((tm, d), lambda i: (i, 0)),
                  pl.BlockSpec((tm, d), lambda i: (i, 0)),
                  pl.BlockSpec((d, d), lambda i: (0, 0)),
                  mod_spec(2),
                  pl.BlockSpec((1, d), lambda i: (0, 0)),
                  mod_spec(4), mod_spec(3),
                  pl.BlockSpec((d, 2 * LANES), lambda i: (0, 0)),
                  pl.BlockSpec((1, LANES), lambda i: (0, 0))],
        out_specs=[pl.BlockSpec((tm, d), lambda i: (i, 0)), pl.BlockSpec((tm, d // 2), lambda i: (i, 0)),
                   pl.BlockSpec((tm, LANES), lambda i: (i, 0))],
        out_shape=[jax.ShapeDtypeStruct((t, d), F32), jax.ShapeDtypeStruct((t, d // 2), jnp.uint32),
                   jax.ShapeDtypeStruct((t, LANES), F32)],
        compiler_params=_params(("parallel",)),
        name="wo_residual",
    )(merged, xf, w_o, mod_l, norm2_g.reshape(1, d), mod_l, mod_l, rw, rb)


def _router_kernel(lg_ref, idx_ref, gate_ref, rank_ref, cnt_ref, run_ref, *, tm):
    @pl.when(pl.program_id(0) == 0)
    def _():
        run_ref[...] = jnp.zeros(run_ref.shape, F32)

    logits = lg_ref[...]
    lane = lax.broadcasted_iota(jnp.int32, (tm, LANES), 1)
    vals = logits
    picked = jnp.zeros((tm, LANES), F32)
    top_v, top_sel, top_i = [], [], []
    for _ in range(TOP_K):
        m = jnp.max(vals, axis=-1, keepdims=True)
        idx = jnp.min(jnp.where(vals == m, lane, LANES), axis=-1, keepdims=True)
        sel = lane == idx
        top_v.append(m)
        top_i.append(idx)
        top_sel.append(sel)
        vals = jnp.where(sel, -jnp.inf, vals)
        picked = picked + sel.astype(F32)

    row = lax.broadcasted_iota(jnp.int32, (tm, tm), 0)
    col = lax.broadcasted_iota(jnp.int32, (tm, tm), 1)
    before = jnp.dot((col < row).astype(BF16), picked.astype(BF16), preferred_element_type=F32) + run_ref[...]
    run_ref[...] = run_ref[...] + jnp.sum(picked, axis=0, keepdims=True)
    cnt_ref[...] = run_ref[...]

    exps = [jnp.exp(v - top_v[0]) for v in top_v]
    denom = exps[0] + exps[1] + exps[2] + exps[3]
    idx_out = jnp.zeros((tm, LANES), jnp.int32)
    rank_out = jnp.zeros((tm, LANES), jnp.int32)
    gate_out = jnp.zeros((tm, LANES), F32)
    for k in range(TOP_K):
        rank_k = jnp.sum(jnp.where(top_sel[k], before, 0.0), axis=-1, keepdims=True).astype(jnp.int32)
        idx_out = jnp.where(lane == k, top_i[k], idx_out)
        rank_out = jnp.where(lane == k, rank_k, rank_out)
        gate_out = jnp.where(lane == k, exps[k] / denom, gate_out)
    idx_ref[...] = idx_out
    rank_ref[...] = rank_out
    gate_ref[...] = gate_out


def _router(logits, tm):
    t = logits.shape[0]
    kern = functools.partial(_router_kernel, tm=tm)
    tok_spec = pl.BlockSpec((tm, LANES), lambda i: (i, 0))
    return pl.pallas_call(
        kern,
        grid=(t // tm,),
        in_specs=[tok_spec],
        out_specs=[tok_spec, tok_spec, tok_spec, pl.BlockSpec((1, LANES), lambda i: (0, 0))],
        out_shape=[jax.ShapeDtypeStruct((t, LANES), jnp.int32), jax.ShapeDtypeStruct((t, LANES), F32),
                   jax.ShapeDtypeStruct((t, LANES), jnp.int32), jax.ShapeDtypeStruct((1, LANES), F32)],
        scratch_shapes=[pltpu.VMEM((1, LANES), F32)],
        compiler_params=_params(("arbitrary",)),
        name="router",
    )(logits)


def _dispatch_kernel(dest_ref, h_ref, xs_in_hbm, xs_hbm, sem, *, tm):
    del xs_in_hbm

    def row_copy(t, k):
        return pltpu.make_async_copy(h_ref.at[pl.ds(t, 1)],
                                     xs_hbm.at[pl.ds(dest_ref[t * TOP_K + k], 1)], sem)

    def issue(t, carry):
        for k in range(TOP_K):
            row_copy(t, k).start()
        return carry

    lax.fori_loop(0, tm, issue, 0)
    for _ in range(TOP_K):
        pltpu.make_async_copy(h_ref, xs_hbm.at[pl.ds(0, tm)], sem).wait()


def _dispatch(h2, dest_flat, n_rows, tm):
    t, d = h2.shape
    kern = functools.partial(_dispatch_kernel, tm=tm)
    return pl.pallas_call(
        kern,
        grid=(t // tm,),
        in_specs=[pl.BlockSpec((tm * TOP_K,), lambda i: (i,), memory_space=pltpu.SMEM),
                  pl.BlockSpec((tm, d), lambda i: (i, 0)),
                  pl.BlockSpec(memory_space=pl.ANY)],
        out_specs=pl.BlockSpec(memory_space=pl.ANY),
        out_shape=jax.ShapeDtypeStruct((n_rows, d), h2.dtype),
        scratch_shapes=[pltpu.SemaphoreType.DMA(())],
        input_output_aliases={2: 0},
        compiler_params=_params(("arbitrary",)),
        name="moe_dispatch",
    )(dest_flat, h2, jnp.zeros((n_rows, d), h2.dtype))


def _expert_gu_kernel(be_ref, nv_ref, x_ref, w_ref, b_ref, o_ref, *, ff):
    del be_ref
    valid = pl.program_id(0) < nv_ref[0]

    @pl.when(valid)
    def _():
        x_lo, x_hi = _unpack_bf16_pair(x_ref[...])
        half = x_lo.shape[1]
        gu = (jnp.dot(x_lo.astype(BF16), w_ref[pl.ds(0, half), :], preferred_element_type=F32)
              + jnp.dot(x_hi.astype(BF16), w_ref[pl.ds(half, half), :], preferred_element_type=F32) + b_ref[...])
        g = jnp.minimum(gu[:, :ff], SWIGLU_LIMIT)
        u = jnp.clip(gu[:, ff:], -SWIGLU_LIMIT, SWIGLU_LIMIT)
        o_ref[...] = ((u + 1.0) * (g * jax.nn.sigmoid(SWIGLU_ALPHA * g))).astype(o_ref.dtype)

    @pl.when(jnp.logical_not(valid))
    def _():
        o_ref[...] = jnp.zeros(o_ref.shape, o_ref.dtype)


def _expert_down_kernel(be_ref, nv_ref, a_ref, w_ref, b_ref, o_ref, wb_ref):
    i = pl.program_id(0)
    valid = i < nv_ref[0]
    new_expert = jnp.logical_or(i == 0, be_ref[i] != be_ref[jnp.maximum(i - 1, 0)])

    @pl.when(jnp.logical_and(valid, new_expert))
    def _():
        wb_ref[...] = w_ref[...].astype(BF16)

    @pl.when(valid)
    def _():
        o_ref[...] = _pack_bf16_pair(jnp.dot(a_ref[...], wb_ref[...], preferred_element_type=F32) + b_ref[...])

    @pl.when(jnp.logical_not(valid))
    def _():
        o_ref[...] = jnp.zeros(o_ref.shape, o_ref.dtype)


def _expert_ffn(xs, block_e, n_valid, w_gu, b_gu, w_dn, b_dn):
    n_rows, dp = xs.shape
    bm = EXPERT_BLOCK
    n_blocks = n_rows // bm
    ne, d, ff2 = w_gu.shape
    ff = ff2 // 2
    act = pl.pallas_call(
        functools.partial(_expert_gu_kernel, ff=ff),
        grid_spec=pltpu.PrefetchScalarGridSpec(
            num_scalar_prefetch=2, grid=(n_blocks,),
            in_specs=[pl.BlockSpec((bm, dp), lambda i, be, nv: (i, 0)),
                      pl.BlockSpec((None, d, ff2), lambda i, be, nv: (be[i], 0, 0)),
                      pl.BlockSpec((None, 1, ff2), lambda i, be, nv: (be[i], 0, 0))],
            out_specs=pl.BlockSpec((bm, ff), lambda i, be, nv: (i, 0))),
        out_shape=jax.ShapeDtypeStruct((n_rows, ff), BF16),
        compiler_params=_params(("arbitrary",)),
        name="expert_gate_up",
    )(block_e, n_valid, xs, w_gu, b_gu.reshape(ne, 1, ff2))
    return pl.pallas_call(
        _expert_down_kernel,
        grid_spec=pltpu.PrefetchScalarGridSpec(
            num_scalar_prefetch=2, grid=(n_blocks,),
            in_specs=[pl.BlockSpec((bm, ff), lambda i, be, nv: (i, 0)),
                      pl.BlockSpec((None, ff, d), lambda i, be, nv: (be[i], 0, 0)),
                      pl.BlockSpec((None, 1, d), lambda i, be, nv: (be[i], 0, 0))],
            out_specs=pl.BlockSpec((bm, dp), lambda i, be, nv: (i, 0)),
            scratch_shapes=[pltpu.VMEM((ff, d), BF16)]),
        out_shape=jax.ShapeDtypeStruct((n_rows, dp), jnp.uint32),
        compiler_params=_params(("arbitrary",)),
        name="expert_down",
    )(block_e, n_valid, act, w_dn, b_dn.reshape(ne, 1, d))


def _combine_kernel(dest_ref, dest_next_ref, ys_hbm, x_ref, gate_ref, g2_ref, fg_ref, o_ref,
                    buf_a, buf_b, sem_a, sem_b, *, half, final):
    i = pl.program_id(0)

    def start_rows(dref, base, buf, sem):
        for t in range(half):
            for k in range(TOP_K):
                pltpu.make_async_copy(ys_hbm.at[pl.ds(dref[base + t * TOP_K + k], 1)],
                                      buf.at[k, pl.ds(t, 1)], sem).start()

    def wait_rows(buf, sem):
        for k in range(TOP_K):
            pltpu.make_async_copy(ys_hbm.at[pl.ds(0, half)], buf.at[k], sem).wait()

    def finish(buf, rows):
        gates = gate_ref[rows, :]
        moe_lo, moe_hi = (gates[:, 0:1] * part for part in _unpack_bf16_pair(buf[0]))
        for k in range(1, TOP_K):
            lo, hi = _unpack_bf16_pair(buf[k])
            moe_lo = moe_lo + gates[:, k:k + 1] * lo
            moe_hi = moe_hi + gates[:, k:k + 1] * hi
        x = x_ref[rows, :] + g2_ref[...] * jnp.concatenate([moe_lo, moe_hi], axis=1)
        if final:
            x = x * lax.rsqrt(jnp.mean(x * x, axis=-1, keepdims=True) + NORM_EPS) * fg_ref[...]
        o_ref[rows, :] = x

    @pl.when(i == 0)
    def _():
        start_rows(dest_ref, 0, buf_a, sem_a)

    wait_rows(buf_a, sem_a)
    start_rows(dest_ref, half * TOP_K, buf_b, sem_b)
    finish(buf_a, pl.ds(0, half))
    wait_rows(buf_b, sem_b)
    start_rows(dest_next_ref, 0, buf_a, sem_a)
    finish(buf_b, pl.ds(half, half))

    @pl.when(i == pl.num_programs(0) - 1)
    def _():
        wait_rows(buf_a, sem_a)


def _combine(ys, dest_flat, xf, gates, mod_l, final_g, seq, half, final):
    t, d = xf.shape
    tm = 2 * half
    steps = t // tm
    tpb = seq // tm
    kern = functools.partial(_combine_kernel, half=half, final=final)
    return pl.pallas_call(
        kern,
        grid=(steps,),
        in_specs=[pl.BlockSpec((tm * TOP_K,), lambda i: (i,), memory_space=pltpu.SMEM),
                  pl.BlockSpec((tm * TOP_K,), lambda i: (jnp.minimum(i + 1, steps - 1),), memory_space=pltpu.SMEM),
                  pl.BlockSpec(memory_space=pl.ANY),
                  pl.BlockSpec((tm, d), lambda i: (i, 0)),
                  pl.BlockSpec((tm, LANES), lambda i: (i, 0)),
                  pl.BlockSpec((None, 1, d), lambda i: (i // tpb, 0, 5)),
                  pl.BlockSpec((1, d), lambda i: (0, 0))],
        out_specs=pl.BlockSpec((tm, d), lambda i: (i, 0)),
        out_shape=jax.ShapeDtypeStruct((t, d), F32),
        scratch_shapes=[pltpu.VMEM((TOP_K, half, d // 2), jnp.uint32), pltpu.VMEM((TOP_K, half, d // 2), jnp.uint32),
                        pltpu.SemaphoreType.DMA(()), pltpu.SemaphoreType.DMA(())],
        compiler_params=_params(("arbitrary",)),
        name="moe_combine",
    )(dest_flat, dest_flat, ys, xf, gates, mod_l, final_g.reshape(1, d))


def _moe(h2p, logits, xf, mod_l, w_gu, b_gu, w_dn, b_dn, final_g, seq, final):
    t = h2p.shape[0]
    ne = w_gu.shape[0]
    bm = EXPERT_BLOCK
    top_i, gates, rank, counts = _router(logits, tm=256)
    cnt = counts[0, :ne].astype(jnp.int32)
    padded = (cnt + bm - 1) // bm * bm
    pad_end = jnp.cumsum(padded)
    pad_start = pad_end - padded
    experts = jnp.arange(ne, dtype=jnp.int32)
    start_of = jnp.sum(jnp.where(top_i[:, :TOP_K, None] == experts, pad_start, 0), axis=-1)
    dest = start_of + rank[:, :TOP_K]
    n_blocks = t * TOP_K // bm + ne
    blk_row = jnp.arange(n_blocks, dtype=jnp.int32)[:, None] * bm
    block_e = jnp.minimum(jnp.sum((pad_end[None, :] <= blk_row).astype(jnp.int32), axis=1), ne - 1)
    n_valid = (pad_end[-1:] // bm).astype(jnp.int32)
    dest_flat = dest.reshape(-1).astype(jnp.int32)

    xs = _dispatch(h2p, dest_flat, n_blocks * bm, tm=256)
    ys = _expert_ffn(xs, block_e, n_valid, w_gu, b_gu, w_dn, b_dn)
    return _combine(ys, dest_flat, xf, gates, mod_l, final_g, seq, half=128, final=final)


def kernel(x, c, ada_w, ada_b, norm1_g, w_in, rel_bias, lam_q1, lam_k1, lam_q2, lam_k2, attn_subln_g, conv_w, conv_b, dt_bias, a_log, d_skip, ssd_norm_g, w_branch, w_o, norm2_g, router_w, router_b, w_gate_up, b_gate_up, w_down, b_down, final_g):
    batch, seq, d = x.shape
    t = batch * seq
    depth = ada_w.shape[0]
    att_w = N_ATT_HEADS * ATT_V_DIM
    ssd_w = SSD_GROUPS * GROUP_W
    n_heads_ssd = SSD_GROUPS * SSD_HEADS_PER_GROUP
    conv_ch = conv_w.shape[2]
    q_cols = 2 * N_ATT_HEADS * ATT_HEAD_DIM
    q_scale = ATT_HEAD_DIM ** -0.5 * math.log2(math.e)
    z_col = 2 * q_cols + att_w
    xbc_col = z_col + ssd_w
    dt_col = xbc_col + conv_ch
    col_scale = jnp.concatenate([jnp.full((q_cols,), q_scale, F32), jnp.ones((dt_col - q_cols,), F32)])

    mod = _adaln(c, ada_w, ada_b)
    bias_tiles, far = _attn_bias_tiles(rel_bias, seq, ATT_Q_BLOCK, ATT_KV_BLOCK)
    expand = _head_expand()
    xf = x.reshape(t, d)
    tm_big = min(1024, seq)

    for layer in range(depth):
        mod_l = mod[layer].reshape(batch, 1, 6 * d)
        w_l = w_in[layer]
        w_main = (w_l[:, :dt_col] * col_scale).astype(BF16)
        w_gate = w_l[:, dt_col + n_heads_ssd:].astype(BF16)
        w_dt = jnp.pad(w_l[:, dt_col:dt_col + n_heads_ssd], ((0, 0), (0, LANES - n_heads_ssd))).astype(BF16)

        proj = _normmod_matmul(xf, norm1_g[layer], mod_l, 1, 0, w_main, BF16, seq, tm=tm_big, tn=1024)
        gates = _normmod_matmul(xf, norm1_g[layer], mod_l, 1, 0, w_gate, BF16, seq, tm=tm_big, tn=1024)
        dt_raw = _normmod_matmul(xf, norm1_g[layer], mod_l, 1, 0, w_dt, F32, seq, tm=tm_big, tn=LANES)

        lam_init = 0.8 - 0.6 * math.exp(-0.3 * layer)
        lamv = jnp.stack([lam_q1[layer], lam_k1[layer], lam_q2[layer], lam_k2[layer]], axis=0)
        o_att = _attention(proj, bias_tiles, far, lamv, attn_subln_g[layer], batch, seq, lam_init)
        o_ssd = _ssd(proj, dt_raw, expand, conv_w[layer], conv_b[layer], dt_bias[layer], a_log[layer],
                     d_skip[layer], ssd_norm_g[layer], batch, seq, z_col, xbc_col)

        wa = w_branch[layer, :att_w].astype(BF16)
        ws = w_branch[layer, att_w:].astype(BF16)
        merged = _merge(o_att, o_ssd, gates, wa, ws, tm=tm_big, tn=256)
        xf, h2p, logits = _wo_residual(merged, xf, w_o[layer].astype(BF16), mod_l, norm2_g[layer],
                                       router_w[layer], router_b[layer], seq, tm=256)

        xf = _moe(h2p, logits, xf, mod_l, w_gate_up[layer].astype(BF16), b_gate_up[layer],
                  w_down[layer], b_down[layer], final_g, seq, final=(layer == depth - 1))
    return xf.reshape(batch, seq, d)
```

```python
import functools
import math

import numpy as np
import jax
import jax.numpy as jnp
from jax import lax
from jax.experimental import pallas as pl
from jax.experimental.pallas import tpu as pltpu

F32 = jnp.float32
BF16 = jnp.bfloat16
HIGHEST = lax.Precision.HIGHEST

DEPTH = 2
CHUNK = 64
N_ATT_HEADS = 8
ATT_HEAD_DIM = 128
ATT_V_DIM = 2 * ATT_HEAD_DIM
REL_BUCKETS = 32
REL_MAX_DIST = 128
SSD_HEAD_DIM = 64
SSD_GROUPS = 8
SSD_HEADS_PER_GROUP = 8
SSD_STATE = 128
SSD_CONV = 4
N_EXPERTS = 32
TOP_K = 4
SWIGLU_LIMIT = 7.0
SWIGLU_ALPHA = 1.702
NORM_EPS = 1e-6
SUBLN_EPS = 1e-5

LANES = 128
SUBLANES = 8
BF16_ROWS = 16
VMEM_LIMIT = 56 * 1024 * 1024

ATT_Q_BLOCK = 512
ATT_KV_BLOCK = 256
SSD_BLOCK = 256
EXPERT_BLOCK = 256
GROUP_W = SSD_HEADS_PER_GROUP * SSD_HEAD_DIM


def _params(semantics):
    return pltpu.CompilerParams(dimension_semantics=semantics, vmem_limit_bytes=VMEM_LIMIT)


def _silu(v):
    half = 0.5 * v
    return half + half * jnp.tanh(half)


def _split3(v):
    hi = v.astype(BF16)
    rest = v - hi.astype(F32)
    mid = rest.astype(BF16)
    lo = (rest - mid.astype(F32)).astype(BF16)
    return hi, mid, lo


def _adaln_kernel(c_ref, w_ref, b_ref, o_ref):
    ca = _silu(c_ref[...])
    o_ref[...] = jnp.dot(ca, w_ref[...], precision=HIGHEST, preferred_element_type=F32) + b_ref[...]


def _adaln(c, ada_w, ada_b):
    nl, d, n = ada_w.shape
    b = c.shape[0]
    tn = 1024
    return pl.pallas_call(
        _adaln_kernel,
        grid=(nl, n // tn),
        in_specs=[pl.BlockSpec((b, d), lambda l, j: (0, 0)),
                  pl.BlockSpec((None, d, tn), lambda l, j: (l, 0, j)),
                  pl.BlockSpec((None, 1, tn), lambda l, j: (l, 0, j))],
        out_specs=pl.BlockSpec((None, b, tn), lambda l, j: (l, 0, j)),
        out_shape=jax.ShapeDtypeStruct((nl, b, n), F32),
        compiler_params=_params(("parallel", "parallel")),
        name="adaln",
    )(c, ada_w, ada_b.reshape(nl, 1, n))


def _normmod_matmul_kernel(x_ref, g_ref, sc_ref, sh_ref, w_ref, o_ref, h_ref):
    @pl.when(pl.program_id(1) == 0)
    def _():
        x = x_ref[...]
        y = x * lax.rsqrt(jnp.mean(x * x, axis=-1, keepdims=True) + NORM_EPS) * g_ref[...]
        h_ref[...] = (y * (1.0 + sc_ref[...]) + sh_ref[...]).astype(BF16)

    o_ref[...] = jnp.dot(h_ref[...], w_ref[...], preferred_element_type=F32).astype(o_ref.dtype)


def _normmod_matmul(xf, g, mod_l, sc_idx, sh_idx, w_all, layer, out_dtype, seq, tm, tn):
    t, d = xf.shape
    n = w_all.shape[2]
    tpb = seq // tm
    return pl.pallas_call(
        _normmod_matmul_kernel,
        grid=(t // tm, n // tn),
        in_specs=[pl.BlockSpec((tm, d), lambda i, j: (i, 0)),
                  pl.BlockSpec((1, d), lambda i, j: (0, 0)),
                  pl.BlockSpec((None, 1, d), lambda i, j: (i // tpb, 0, sc_idx)),
                  pl.BlockSpec((None, 1, d), lambda i, j: (i // tpb, 0, sh_idx)),
                  pl.BlockSpec((None, d, tn), lambda i, j: (layer, 0, j))],
        out_specs=pl.BlockSpec((tm, tn), lambda i, j: (i, j)),
        out_shape=jax.ShapeDtypeStruct((t, n), out_dtype),
        scratch_shapes=[pltpu.VMEM((tm, d), BF16)],
        compiler_params=_params(("parallel", "arbitrary")),
        name="inproj",
    )(xf, g.reshape(1, d), mod_l, mod_l, w_all)


def _t5_bucket(rel):
    half = REL_BUCKETS // 2
    exact = half // 2
    ret = jnp.where(rel > 0, half, 0)
    n = jnp.abs(rel)
    nf = jnp.maximum(n, 1).astype(jnp.float32)
    large = exact + (jnp.log(nf / exact) / math.log(REL_MAX_DIST / exact) * (half - exact)).astype(jnp.int32)
    large = jnp.minimum(large, half - 1)
    return ret + jnp.where(n < exact, n, large)


def _attn_bias_tiles(rel_bias, seq, qb, kb):
    assert kb >= REL_MAX_DIST and kb % CHUNK == 0 and qb % kb == 0
    nh = rel_bias.shape[1]
    lo = kb + qb - 1
    rel = jnp.arange(-lo, qb, dtype=jnp.int32)
    vec = (rel_bias[_t5_bucket(rel)].astype(F32) * math.log2(math.e)).T
    period = qb + kb
    jj = np.arange(period)
    jj = np.where(jj < qb, jj, jj - period)
    qq = np.arange(qb)[None, :]
    tiles = []
    for off in range(-1, qb // kb):
        row = vec[:, np.clip(lo + off * kb - jj, 0, lo + qb - 1)]
        skew = jnp.tile(row, (1, kb))[:, :kb * (period - 1)].reshape(nh, kb, period - 1)[:, :, :qb]
        kk = off * kb + np.arange(kb)[:, None]
        visible = (kk // CHUNK) <= (qq // CHUNK)
        tiles.append(jnp.where(visible[None], skew, -jnp.inf))
    far = rel_bias[_t5_bucket(jnp.full((1,), -(seq - 1), jnp.int32))][0].astype(F32)
    return jnp.stack(tiles, axis=1), far


def _attn_kernel(far_ref, q1_ref, q2_ref, k1_ref, k2_ref, v_ref, bias_ref, lam_ref, g_ref, o_ref,
                 vt_ref, acc1_ref, acc2_ref, sa1_ref, sa2_ref, sb1_ref, sb2_ref, *, qb, kb, dv, lam_init):
    h = pl.program_id(1)
    i = pl.program_id(2)
    far = far_ref[h] * math.log2(math.e)
    nt_dims = (((1,), (1,)), ((), ()))
    per_tile = qb // kb
    pad_rows = vt_ref.shape[1] - dv
    s_a = (sa1_ref, sa2_ref)
    s_b = (sb1_ref, sb2_ref)

    @pl.when(i == 0)
    def _():
        rr = lax.broadcasted_iota(jnp.int32, (dv, dv), 0)
        cc = lax.broadcasted_iota(jnp.int32, (dv, dv), 1)
        eye = jnp.where(rr == cc, 1.0, 0.0).astype(BF16)
        ones_row = jnp.where(lax.broadcasted_iota(jnp.int32, (pad_rows, kb), 0) == 0, 1.0, 0.0).astype(BF16)

        def tr_body(j, carry):
            vj = v_ref[pl.ds(pl.multiple_of(j * kb, kb), kb), :]
            vt_ref[j, pl.ds(0, dv), :] = vj.T
            vt_ref[j, pl.ds(dv, pad_rows), :] = ones_row
            return carry

        lax.fori_loop(0, vt_ref.shape[0], tr_body, 0)

    acc1_ref[...] = jnp.zeros(acc1_ref.shape, F32)
    acc2_ref[...] = jnp.zeros(acc2_ref.shape, F32)

    def scores(j):
        ks = pl.ds(pl.multiple_of(j * kb, kb), kb)
        return (lax.dot_general(k1_ref[ks, :], q1_ref[...], nt_dims, preferred_element_type=F32),
                lax.dot_general(k2_ref[ks, :], q2_ref[...], nt_dims, preferred_element_type=F32))

    def softmax_step(s, m, bias, shift):
        if bias is not None:
            s = s + bias
        m_new = jnp.maximum(m, jnp.max(s, axis=0, keepdims=True) + shift)
        return m_new, jnp.exp2(m - m_new), jnp.exp2(s - (m_new - shift)).astype(BF16)

    def step(carry, j, src, dst, bias, shift):
        m1, m2 = carry
        if dst is not None:
            dst[0][...], dst[1][...] = scores(j + 1)
        m1, alpha1, p1 = softmax_step(src[0][...], m1, bias, shift)
        pv1 = jnp.dot(vt_ref[j], p1, preferred_element_type=F32)
        m2, alpha2, p2 = softmax_step(src[1][...], m2, bias, shift)
        pv2 = jnp.dot(vt_ref[j], p2, preferred_element_type=F32)
        acc1_ref[...] = acc1_ref[...] * alpha1 + pv1
        acc2_ref[...] = acc2_ref[...] * alpha2 + pv2
        return m1, m2

    def pair(carry, j, kinds, final=False):
        (bias_a, shift_a), (bias_b, shift_b) = kinds
        carry = step(carry, j, s_a, s_b, bias_a, shift_a)
        return step(carry, j + 1, s_b, None if final else s_a, bias_b, shift_b)

    assert per_tile == 2
    far_kind = (None, far)
    neg = jnp.full((1, qb), -jnp.inf, F32)
    s_a[0][...], s_a[1][...] = scores(0)
    n_ff = jnp.maximum(i - 1, 0)
    carry = lax.fori_loop(0, n_ff, lambda t, c: pair(c, 2 * t, (far_kind, far_kind)), (neg, neg))
    carry = lax.fori_loop(n_ff, i, lambda t, c: pair(c, 2 * t, (far_kind, (bias_ref[0], 0.0))), carry)
    pair(carry, 2 * i, ((bias_ref[1], 0.0), (bias_ref[2], 0.0)), final=True)

    lamv = lam_ref[...]
    lam = (jnp.exp(jnp.sum(lamv[0:1] * lamv[1:2], axis=-1, keepdims=True))
           - jnp.exp(jnp.sum(lamv[2:3] * lamv[3:4], axis=-1, keepdims=True)) + lam_init)
    a1 = acc1_ref[...]
    a2 = acc2_ref[...]
    o = (a1[:dv] / a1[dv:dv + 1] - lam * (a2[:dv] / a2[dv:dv + 1])).T
    y = o * lax.rsqrt(jnp.mean(o * o, axis=-1, keepdims=True) + SUBLN_EPS) * g_ref[...]
    o_ref[...] = (y * (1.0 - lam_init)).astype(o_ref.dtype)


def _attention(proj, bias_tiles, far, lamv, subln_g, batch, seq, lam_init):
    qb, kb = ATT_Q_BLOCK, ATT_KV_BLOCK
    nq = seq // qb
    nh = N_ATT_HEADS
    dk, dv = ATT_HEAD_DIM, ATT_V_DIM
    t = batch * seq
    k_off = 2 * nh
    v_off = 4 * nh * dk // dv
    kern = functools.partial(_attn_kernel, qb=qb, kb=kb, dv=dv, lam_init=lam_init)
    return pl.pallas_call(
        kern,
        grid=(batch, nh, nq),
        in_specs=[pl.BlockSpec(memory_space=pltpu.SMEM),
                  pl.BlockSpec((qb, dk), lambda b, h, i: (b * nq + i, h)),
                  pl.BlockSpec((qb, dk), lambda b, h, i: (b * nq + i, nh + h)),
                  pl.BlockSpec((seq, dk), lambda b, h, i: (b, k_off + h)),
                  pl.BlockSpec((seq, dk), lambda b, h, i: (b, k_off + nh + h)),
                  pl.BlockSpec((seq, dv), lambda b, h, i: (b, v_off + h)),
                  pl.BlockSpec((None, qb // kb + 1, kb, qb), lambda b, h, i: (h, 0, 0, 0)),
                  pl.BlockSpec((4, dk), lambda b, h, i: (0, 0)),
                  pl.BlockSpec((1, dv), lambda b, h, i: (0, 0))],
        out_specs=pl.BlockSpec((qb, dv), lambda b, h, i: (b * nq + i, h)),
        out_shape=jax.ShapeDtypeStruct((t, nh * dv), BF16),
        scratch_shapes=[pltpu.VMEM((seq // kb, dv + BF16_ROWS, kb), BF16),
                        pltpu.VMEM((dv + BF16_ROWS, qb), F32), pltpu.VMEM((dv + BF16_ROWS, qb), F32)]
        + [pltpu.VMEM((kb, qb), F32)] * 4,
        compiler_params=_params(("parallel", "parallel", "arbitrary")),
        name="diff_attention",
    )(far, proj, proj, proj, proj, proj, bias_tiles, lamv, subln_g.reshape(1, dv))


def _conv_silu(x_ref, hist_ref, w_ref, b_ref, blk):
    x = x_ref[...].astype(F32)
    hist_ref[pl.ds(SUBLANES, blk), :] = x
    w = w_ref[...]
    acc = x * w[SSD_CONV - 1:SSD_CONV] + b_ref[...]
    for j in range(1, SSD_CONV):
        acc = acc + hist_ref[pl.ds(SUBLANES - j, blk), :] * w[SSD_CONV - 1 - j:SSD_CONV - j]
    hist_ref[pl.ds(0, SUBLANES), :] = x[blk - SUBLANES:]
    return _silu(acc)


def _ssd_kernel(xs_ref, bm_ref, cm_ref, z_ref, dt_ref, e_ref, cwx_ref, cwb_ref, cwc_ref, cbx_ref, cbb_ref,
                cbc_ref, dtb_ref, alog_ref, dskip_ref, ng_ref, o_ref,
                state_ref, tx_ref, tb_ref, tc_ref, act_ref, *, blk):
    g = pl.program_id(1)

    @pl.when(pl.program_id(2) == 0)
    def _():
        state_ref[...] = jnp.zeros(state_ref.shape, F32)
        for hist_ref in (tx_ref, tb_ref, tc_ref):
            hist_ref[pl.ds(0, SUBLANES), :] = jnp.zeros((SUBLANES, hist_ref.shape[1]), F32)

    xs = _conv_silu(xs_ref, tx_ref, cwx_ref, cbx_ref, blk)
    bm = _conv_silu(bm_ref, tb_ref, cwb_ref, cbb_ref, blk)
    cm = _conv_silu(cm_ref, tc_ref, cwc_ref, cbc_ref, blk)

    dt = jax.nn.softplus(dt_ref[...] + dtb_ref[...])
    da = dt * (-jnp.exp(alog_ref[...]))
    row = lax.broadcasted_iota(jnp.int32, (blk, blk), 0)
    col = lax.broadcasted_iota(jnp.int32, (blk, blk), 1)
    causal = col <= row
    tri = jnp.where(causal, 1.0, 0.0).astype(BF16)
    acum = sum(jnp.dot(tri, part, preferred_element_type=F32) for part in _split3(da))
    both_e = jnp.dot(jnp.concatenate([jnp.concatenate(_split3(dt), axis=1),
                                      jnp.concatenate(_split3(acum), axis=1)], axis=0),
                     e_ref[...], preferred_element_type=F32)
    dt_e = both_e[:blk]
    acum_e = both_e[blk:]
    act_ref[...] = acum.T

    xdt = xs * dt_e
    xdt_b = xdt.astype(BF16)
    cm_b = cm.astype(BF16)
    cb = lax.dot_general(cm_b, bm.astype(BF16), (((1,), (1,)), ((), ())), preferred_element_type=F32)
    lane = lax.broadcasted_iota(jnp.int32, (blk, LANES), 1)
    halves = (lane < SSD_HEAD_DIM, lane >= SSD_HEAD_DIM)
    pieces = []
    for pair in range(GROUP_W // LANES):
        xp = xdt_b[:, pair * LANES:(pair + 1) * LANES]
        yp = jnp.zeros((blk, LANES), F32)
        for hh in range(2):
            r = 2 * pair + hh
            a_col = acum_e[:, r * SSD_HEAD_DIM:r * SSD_HEAD_DIM + 1]
            a_row = act_ref[pl.ds(g * SSD_HEADS_PER_GROUP + r, 1), :]
            decay = jnp.exp(jnp.where(causal, a_col - a_row, -jnp.inf))
            mat = (cb * decay).astype(BF16)
            yp = yp + jnp.dot(mat, jnp.where(halves[hh], xp, jnp.zeros_like(xp)), preferred_element_type=F32)
        pieces.append(yp)
    y = jnp.concatenate(pieces, axis=1)

    st = state_ref[...]
    y = y + jnp.exp(acum_e) * jnp.dot(cm_b, st.astype(BF16), preferred_element_type=F32)
    a_last = acum_e[blk - 1:blk, :]
    wgt = (xdt * jnp.exp(a_last - acum_e)).astype(BF16)
    state_ref[...] = st * jnp.exp(a_last) + jnp.dot(bm.T.astype(BF16), wgt, preferred_element_type=F32)

    y = y + dskip_ref[...] * xs
    y = y * _silu(z_ref[...].astype(F32))
    y = y * lax.rsqrt(jnp.mean(y * y, axis=-1, keepdims=True) + SUBLN_EPS)
    o_ref[...] = (y * ng_ref[...]).astype(o_ref.dtype)


def _ssd(proj, dt_raw, expand, conv_w, conv_b, dt_bias, a_log, d_skip, norm_g, batch, seq, z_col, xbc_col):
    blk = SSD_BLOCK
    nc = seq // blk
    ng = SSD_GROUPS
    t = batch * seq
    width = ng * GROUP_W
    heads = ng * SSD_HEADS_PER_GROUP
    z_blk = z_col // GROUP_W
    xs_blk = xbc_col // GROUP_W
    b_blk = (xbc_col + width) // SSD_STATE
    c_blk = b_blk + ng
    cw_b_blk = width // SSD_STATE
    pad = LANES - heads
    row = lambda b, g, c: b * nc + c
    dtb = jnp.pad(dt_bias, (0, pad)).reshape(1, LANES)
    alog = jnp.pad(a_log, (0, pad)).reshape(1, LANES)
    dskip = jnp.repeat(d_skip, SSD_HEAD_DIM).reshape(1, width)
    cb2 = conv_b.reshape(1, -1)
    kern = functools.partial(_ssd_kernel, blk=blk)
    return pl.pallas_call(
        kern,
        grid=(batch, ng, nc),
        in_specs=[pl.BlockSpec((blk, GROUP_W), lambda b, g, c: (row(b, g, c), xs_blk + g)),
                  pl.BlockSpec((blk, SSD_STATE), lambda b, g, c: (row(b, g, c), b_blk + g)),
                  pl.BlockSpec((blk, SSD_STATE), lambda b, g, c: (row(b, g, c), c_blk + g)),
                  pl.BlockSpec((blk, GROUP_W), lambda b, g, c: (row(b, g, c), z_blk + g)),
                  pl.BlockSpec((blk, LANES), lambda b, g, c: (row(b, g, c), 0)),
                  pl.BlockSpec((None, 3 * LANES, GROUP_W), lambda b, g, c: (g, 0, 0)),
                  pl.BlockSpec((SSD_CONV, GROUP_W), lambda b, g, c: (0, g)),
                  pl.BlockSpec((SSD_CONV, SSD_STATE), lambda b, g, c: (0, cw_b_blk + g)),
                  pl.BlockSpec((SSD_CONV, SSD_STATE), lambda b, g, c: (0, cw_b_blk + ng + g)),
                  pl.BlockSpec((1, GROUP_W), lambda b, g, c: (0, g)),
                  pl.BlockSpec((1, SSD_STATE), lambda b, g, c: (0, cw_b_blk + g)),
                  pl.BlockSpec((1, SSD_STATE), lambda b, g, c: (0, cw_b_blk + ng + g)),
                  pl.BlockSpec((1, LANES), lambda b, g, c: (0, 0)),
                  pl.BlockSpec((1, LANES), lambda b, g, c: (0, 0)),
                  pl.BlockSpec((1, GROUP_W), lambda b, g, c: (0, g)),
                  pl.BlockSpec((1, GROUP_W), lambda b, g, c: (0, g))],
        out_specs=pl.BlockSpec((blk, GROUP_W), lambda b, g, c: (row(b, g, c), g)),
        out_shape=jax.ShapeDtypeStruct((t, width), BF16),
        scratch_shapes=[pltpu.VMEM((SSD_STATE, GROUP_W), F32),
                        pltpu.VMEM((SUBLANES + blk, GROUP_W), F32),
                        pltpu.VMEM((SUBLANES + blk, SSD_STATE), F32),
                        pltpu.VMEM((SUBLANES + blk, SSD_STATE), F32),
                        pltpu.VMEM((LANES, blk), F32)],
        compiler_params=_params(("parallel", "parallel", "arbitrary")),
        name="ssd",
    )(proj, proj, proj, proj, dt_raw, expand, conv_w, conv_w, conv_w, cb2, cb2, cb2, dtb, alog, dskip,
      norm_g.reshape(1, width))


def _head_expand():
    e = np.zeros((SSD_GROUPS, LANES, GROUP_W), np.float32)
    for g in range(SSD_GROUPS):
        for r in range(SSD_HEADS_PER_GROUP):
            e[g, g * SSD_HEADS_PER_GROUP + r, r * SSD_HEAD_DIM:(r + 1) * SSD_HEAD_DIM] = 1.0
    return jnp.asarray(np.tile(e, (1, 3, 1)), dtype=BF16)


def _merge_kernel(oa_ref, os_ref, wa_ref, ws1_ref, ws2_ref, ga_ref, gs_ref, o_ref):
    ka = wa_ref.shape[0]
    ya = jnp.dot(oa_ref[...], wa_ref[...], preferred_element_type=F32)
    ys = (jnp.dot(os_ref[:, pl.ds(0, ka)], ws1_ref[...], preferred_element_type=F32)
          + jnp.dot(os_ref[:, pl.ds(ka, ka)], ws2_ref[...], preferred_element_type=F32))
    merged = jax.nn.sigmoid(ga_ref[...].astype(F32)) * ya + jax.nn.sigmoid(gs_ref[...].astype(F32)) * ys
    o_ref[...] = merged.astype(o_ref.dtype)


def _merge(o_att, o_ssd, gates, w_branch_all, layer, tm, tn):
    t, ka = o_att.shape
    ks = o_ssd.shape[1]
    d = w_branch_all.shape[2]
    assert ks == 2 * ka
    w_spec = lambda blk: pl.BlockSpec((None, ka, tn), lambda i, j: (layer, blk, j))
    gs_blk = d // tn
    return pl.pallas_call(
        _merge_kernel,
        grid=(t // tm, d // tn),
        in_specs=[pl.BlockSpec((tm, ka), lambda i, j: (i, 0)),
                  pl.BlockSpec((tm, ks), lambda i, j: (i, 0)),
                  w_spec(0), w_spec(1), w_spec(2),
                  pl.BlockSpec((tm, tn), lambda i, j: (i, j)),
                  pl.BlockSpec((tm, tn), lambda i, j: (i, gs_blk + j))],
        out_specs=pl.BlockSpec((tm, tn), lambda i, j: (i, j)),
        out_shape=jax.ShapeDtypeStruct((t, d), BF16),
        compiler_params=_params(("parallel", "arbitrary")),
        name="branch_merge",
    )(o_att, o_ssd, w_branch_all, w_branch_all, w_branch_all, gates, gates)


def _pack_bf16_pair(v):
    n = v.shape[1] // 2
    lo = lax.bitcast_convert_type(v[:, :n].astype(BF16).astype(F32), jnp.uint32)
    hi = lax.bitcast_convert_type(v[:, n:].astype(BF16).astype(F32), jnp.uint32)
    return (lo >> 16) | (hi & jnp.uint32(0xFFFF0000))


def _unpack_bf16_pair(p):
    lo = lax.bitcast_convert_type(p << 16, F32)
    hi = lax.bitcast_convert_type(p & jnp.uint32(0xFFFF0000), F32)
    return lo, hi


def _wo_kernel(m_ref, x_ref, w_ref, g1_ref, ng_ref, sc_ref, sh_ref, rw_ref, rb_ref, xo_ref, hp_ref, lg_ref):
    y = jnp.dot(m_ref[...], w_ref[...], preferred_element_type=F32)
    x = x_ref[...] + g1_ref[...] * y
    xo_ref[...] = x
    hn = x * lax.rsqrt(jnp.mean(x * x, axis=-1, keepdims=True) + NORM_EPS) * ng_ref[...]
    h2 = hn * (1.0 + sc_ref[...]) + sh_ref[...]
    hp_ref[...] = _pack_bf16_pair(h2)
    h_hi = h2.astype(BF16)
    h_lo = (h2 - h_hi.astype(F32)).astype(BF16)
    both = jnp.dot(h_hi, rw_ref[...], preferred_element_type=F32)
    cross = jnp.dot(h_lo, rw_ref[:, pl.ds(0, LANES)], preferred_element_type=F32)
    lg_ref[...] = both[:, :LANES] + (both[:, LANES:] + cross) + rb_ref[...]


def _wo_residual(merged, xf, w_o_all, layer, mod_l, norm2_g, router_w, router_b, seq, tm):
    t, d = xf.shape
    ne = router_w.shape[1]
    tpb = seq // tm
    mod_spec = lambda idx: pl.BlockSpec((None, 1, d), lambda i: (i // tpb, 0, idx))
    rw = jnp.pad(router_w, ((0, 0), (0, LANES - ne)))
    rw_hi = rw.astype(BF16)
    rw = jnp.concatenate([rw_hi, (rw - rw_hi.astype(F32)).astype(BF16)], axis=1)
    rb = jnp.pad(router_b, (0, LANES - ne), constant_values=-jnp.inf).reshape(1, LANES)
    return pl.pallas_call(
        _wo_kernel,
        grid=(t // tm,),
        in_specs=[pl.BlockSpec((tm, d), lambda i: (i, 0)),
                  pl.BlockSpec((tm, d), lambda i: (i, 0)),
                  pl.BlockSpec((None, d, d), lambda i: (layer, 0, 0)),
                  mod_spec(2),
                  pl.BlockSpec((1, d), lambda i: (0, 0)),
                  mod_spec(4), mod_spec(3),
                  pl.BlockSpec((d, 2 * LANES), lambda i: (0, 0)),
                  pl.BlockSpec((1, LANES), lambda i: (0, 0))],
        out_specs=[pl.BlockSpec((tm, d), lambda i: (i, 0)), pl.BlockSpec((tm, d // 2), lambda i: (i, 0)),
                   pl.BlockSpec((tm, LANES), lambda i: (i, 0))],
        out_shape=[jax.ShapeDtypeStruct((t, d), F32), jax.ShapeDtypeStruct((t, d // 2), jnp.uint32),
                   jax.ShapeDtypeStruct((t, LANES), F32)],
        compiler_params=_params(("parallel",)),
        name="wo_residual",
    )(merged, xf, w_o_all, mod_l, norm2_g.reshape(1, d), mod_l, mod_l, rw, rb)


def _router_kernel(lg_ref, idx_ref, gate_ref, rank_ref, cnt_ref, run_ref, *, tm):
    @pl.when(pl.program_id(0) == 0)
    def _():
        run_ref[...] = jnp.zeros(run_ref.shape, F32)

    logits = lg_ref[...]
    lane = lax.broadcasted_iota(jnp.int32, (tm, LANES), 1)
    vals = logits
    picked = jnp.zeros((tm, LANES), F32)
    top_v, top_sel, top_i = [], [], []
    for _ in range(TOP_K):
        m = jnp.max(vals, axis=-1, keepdims=True)
        idx = jnp.min(jnp.where(vals == m, lane, LANES), axis=-1, keepdims=True)
        sel = lane == idx
        top_v.append(m)
        top_i.append(idx)
        top_sel.append(sel)
        vals = jnp.where(sel, -jnp.inf, vals)
        picked = picked + sel.astype(F32)

    row = lax.broadcasted_iota(jnp.int32, (tm, tm), 0)
    col = lax.broadcasted_iota(jnp.int32, (tm, tm), 1)
    before = jnp.dot((col < row).astype(BF16), picked.astype(BF16), preferred_element_type=F32) + run_ref[...]
    run_ref[...] = run_ref[...] + jnp.sum(picked, axis=0, keepdims=True)
    cnt_ref[...] = run_ref[...]

    exps = [jnp.exp(v - top_v[0]) for v in top_v]
    denom = exps[0] + exps[1] + exps[2] + exps[3]
    idx_out = jnp.zeros((tm, LANES), jnp.int32)
    rank_out = jnp.zeros((tm, LANES), jnp.int32)
    gate_out = jnp.zeros((tm, LANES), F32)
    for k in range(TOP_K):
        rank_k = jnp.sum(jnp.where(top_sel[k], before, 0.0), axis=-1, keepdims=True).astype(jnp.int32)
        idx_out = jnp.where(lane == k, top_i[k], idx_out)
        rank_out = jnp.where(lane == k, rank_k, rank_out)
        gate_out = jnp.where(lane == k, exps[k] / denom, gate_out)
    idx_ref[...] = idx_out
    rank_ref[...] = rank_out
    gate_ref[...] = gate_out


def _router(logits, tm):
    t = logits.shape[0]
    kern = functools.partial(_router_kernel, tm=tm)
    tok_spec = pl.BlockSpec((tm, LANES), lambda i: (i, 0))
    return pl.pallas_call(
        kern,
        grid=(t // tm,),
        in_specs=[tok_spec],
        out_specs=[tok_spec, tok_spec, tok_spec, pl.BlockSpec((1, LANES), lambda i: (0, 0))],
        out_shape=[jax.ShapeDtypeStruct((t, LANES), jnp.int32), jax.ShapeDtypeStruct((t, LANES), F32),
                   jax.ShapeDtypeStruct((t, LANES), jnp.int32), jax.ShapeDtypeStruct((1, LANES), F32)],
        scratch_shapes=[pltpu.VMEM((1, LANES), F32)],
        compiler_params=_params(("arbitrary",)),
        name="router",
    )(logits)


def _dispatch_kernel(dest_ref, h_ref, xs_in_hbm, xs_hbm, sem, *, tm):
    del xs_in_hbm

    def row_copy(t, k):
        return pltpu.make_async_copy(h_ref.at[pl.ds(t, 1)],
                                     xs_hbm.at[pl.ds(dest_ref[t * TOP_K + k], 1)], sem)

    def issue(t, carry):
        for k in range(TOP_K):
            row_copy(t, k).start()
        return carry

    lax.fori_loop(0, tm, issue, 0)
    for _ in range(TOP_K):
        pltpu.make_async_copy(h_ref, xs_hbm.at[pl.ds(0, tm)], sem).wait()


def _dispatch(h2, dest_flat, n_rows, tm):
    t, d = h2.shape
    kern = functools.partial(_dispatch_kernel, tm=tm)
    return pl.pallas_call(
        kern,
        grid=(t // tm,),
        in_specs=[pl.BlockSpec((tm * TOP_K,), lambda i: (i,), memory_space=pltpu.SMEM),
                  pl.BlockSpec((tm, d), lambda i: (i, 0)),
                  pl.BlockSpec(memory_space=pl.ANY)],
        out_specs=pl.BlockSpec(memory_space=pl.ANY),
        out_shape=jax.ShapeDtypeStruct((n_rows, d), h2.dtype),
        scratch_shapes=[pltpu.SemaphoreType.DMA(())],
        input_output_aliases={2: 0},
        compiler_params=_params(("arbitrary",)),
        name="moe_dispatch",
    )(dest_flat, h2, jnp.zeros((n_rows, d), h2.dtype))


def _expert_gu_kernel(be_ref, nv_ref, x_ref, w_ref, b_ref, o_ref, *, ff):
    del be_ref
    valid = pl.program_id(0) < nv_ref[0]

    @pl.when(valid)
    def _():
        x_lo, x_hi = _unpack_bf16_pair(x_ref[...])
        half = x_lo.shape[1]
        gu = (jnp.dot(x_lo.astype(BF16), w_ref[pl.ds(0, half), :], preferred_element_type=F32)
              + jnp.dot(x_hi.astype(BF16), w_ref[pl.ds(half, half), :], preferred_element_type=F32) + b_ref[...])
        g = jnp.minimum(gu[:, :ff], SWIGLU_LIMIT)
        u = jnp.clip(gu[:, ff:], -SWIGLU_LIMIT, SWIGLU_LIMIT)
        o_ref[...] = ((u + 1.0) * (g * jax.nn.sigmoid(SWIGLU_ALPHA * g))).astype(o_ref.dtype)

    @pl.when(jnp.logical_not(valid))
    def _():
        o_ref[...] = jnp.zeros(o_ref.shape, o_ref.dtype)


def _expert_down_kernel(be_ref, nv_ref, a_ref, w_ref, b_ref, o_ref, wb_ref):
    i = pl.program_id(0)
    valid = i < nv_ref[0]
    new_expert = jnp.logical_or(i == 0, be_ref[i] != be_ref[jnp.maximum(i - 1, 0)])

    @pl.when(jnp.logical_and(valid, new_expert))
    def _():
        wb_ref[...] = w_ref[...].astype(BF16)

    @pl.when(valid)
    def _():
        o_ref[...] = _pack_bf16_pair(jnp.dot(a_ref[...], wb_ref[...], preferred_element_type=F32) + b_ref[...])

    @pl.when(jnp.logical_not(valid))
    def _():
        o_ref[...] = jnp.zeros(o_ref.shape, o_ref.dtype)


def _expert_ffn(xs, block_e, n_valid, w_gu, b_gu, w_dn, b_dn, layer):
    n_rows, dp = xs.shape
    bm = EXPERT_BLOCK
    n_blocks = n_rows // bm
    nl, ne, d, ff2 = w_gu.shape
    ff = ff2 // 2
    act = pl.pallas_call(
        functools.partial(_expert_gu_kernel, ff=ff),
        grid_spec=pltpu.PrefetchScalarGridSpec(
            num_scalar_prefetch=2, grid=(n_blocks,),
            in_specs=[pl.BlockSpec((bm, dp), lambda i, be, nv: (i, 0)),
                      pl.BlockSpec((None, None, d, ff2), lambda i, be, nv: (layer, be[i], 0, 0)),
                      pl.BlockSpec((None, None, 1, ff2), lambda i, be, nv: (layer, be[i], 0, 0))],
            out_specs=pl.BlockSpec((bm, ff), lambda i, be, nv: (i, 0))),
        out_shape=jax.ShapeDtypeStruct((n_rows, ff), BF16),
        compiler_params=_params(("arbitrary",)),
        name="expert_gate_up",
    )(block_e, n_valid, xs, w_gu, b_gu.reshape(nl, ne, 1, ff2))
    return pl.pallas_call(
        _expert_down_kernel,
        grid_spec=pltpu.PrefetchScalarGridSpec(
            num_scalar_prefetch=2, grid=(n_blocks,),
            in_specs=[pl.BlockSpec((bm, ff), lambda i, be, nv: (i, 0)),
                      pl.BlockSpec((None, None, ff, d), lambda i, be, nv: (layer, be[i], 0, 0)),
                      pl.BlockSpec((None, None, 1, d), lambda i, be, nv: (layer, be[i], 0, 0))],
            out_specs=pl.BlockSpec((bm, dp), lambda i, be, nv: (i, 0)),
            scratch_shapes=[pltpu.VMEM((ff, d), BF16)]),
        out_shape=jax.ShapeDtypeStruct((n_rows, dp), jnp.uint32),
        compiler_params=_params(("arbitrary",)),
        name="expert_down",
    )(block_e, n_valid, act, w_dn, b_dn.reshape(nl, ne, 1, d))


def _combine_kernel(dest_ref, dest_next_ref, ys_hbm, x_ref, gate_ref, g2_ref, fg_ref, o_ref,
                    buf_a, buf_b, sem_a, sem_b, *, half, final):
    i = pl.program_id(0)

    def start_rows(dref, base, buf, sem):
        for t in range(half):
            for k in range(TOP_K):
                pltpu.make_async_copy(ys_hbm.at[pl.ds(dref[base + t * TOP_K + k], 1)],
                                      buf.at[k, pl.ds(t, 1)], sem).start()

    def wait_rows(buf, sem):
        for k in range(TOP_K):
            pltpu.make_async_copy(ys_hbm.at[pl.ds(0, half)], buf.at[k], sem).wait()

    def finish(buf, rows):
        gates = gate_ref[rows, :]
        moe_lo, moe_hi = (gates[:, 0:1] * part for part in _unpack_bf16_pair(buf[0]))
        for k in range(1, TOP_K):
            lo, hi = _unpack_bf16_pair(buf[k])
            moe_lo = moe_lo + gates[:, k:k + 1] * lo
            moe_hi = moe_hi + gates[:, k:k + 1] * hi
        x = x_ref[rows, :] + g2_ref[...] * jnp.concatenate([moe_lo, moe_hi], axis=1)
        if final:
            x = x * lax.rsqrt(jnp.mean(x * x, axis=-1, keepdims=True) + NORM_EPS) * fg_ref[...]
        o_ref[rows, :] = x

    @pl.when(i == 0)
    def _():
        start_rows(dest_ref, 0, buf_a, sem_a)

    wait_rows(buf_a, sem_a)
    start_rows(dest_ref, half * TOP_K, buf_b, sem_b)
    finish(buf_a, pl.ds(0, half))
    wait_rows(buf_b, sem_b)
    start_rows(dest_next_ref, 0, buf_a, sem_a)
    finish(buf_b, pl.ds(half, half))

    @pl.when(i == pl.num_programs(0) - 1)
    def _():
        wait_rows(buf_a, sem_a)


def _combine(ys, dest_flat, xf, gates, mod_l, final_g, seq, half, final):
    t, d = xf.shape
    tm = 2 * half
    steps = t // tm
    tpb = seq // tm
    kern = functools.partial(_combine_kernel, half=half, final=final)
    return pl.pallas_call(
        kern,
        grid=(steps,),
        in_specs=[pl.BlockSpec((tm * TOP_K,), lambda i: (i,), memory_space=pltpu.SMEM),
                  pl.BlockSpec((tm * TOP_K,), lambda i: (jnp.minimum(i + 1, steps - 1),), memory_space=pltpu.SMEM),
                  pl.BlockSpec(memory_space=pl.ANY),
                  pl.BlockSpec((tm, d), lambda i: (i, 0)),
                  pl.BlockSpec((tm, LANES), lambda i: (i, 0)),
                  pl.BlockSpec((None, 1, d), lambda i: (i // tpb, 0, 5)),
                  pl.BlockSpec((1, d), lambda i: (0, 0))],
        out_specs=pl.BlockSpec((tm, d), lambda i: (i, 0)),
        out_shape=jax.ShapeDtypeStruct((t, d), F32),
        scratch_shapes=[pltpu.VMEM((TOP_K, half, d // 2), jnp.uint32), pltpu.VMEM((TOP_K, half, d // 2), jnp.uint32),
                        pltpu.SemaphoreType.DMA(()), pltpu.SemaphoreType.DMA(())],
        compiler_params=_params(("arbitrary",)),
        name="moe_combine",
    )(dest_flat, dest_flat, ys, xf, gates, mod_l, final_g.reshape(1, d))


def _moe(h2p, logits, xf, mod_l, w_gu, b_gu, w_dn, b_dn, layer, final_g, seq, final):
    t = h2p.shape[0]
    ne = w_gu.shape[1]
    bm = EXPERT_BLOCK
    top_i, gates, rank, counts = _router(logits, tm=256)
    cnt = counts[0, :ne].astype(jnp.int32)
    padded = (cnt + bm - 1) // bm * bm
    pad_end = jnp.cumsum(padded)
    pad_start = pad_end - padded
    experts = jnp.arange(ne, dtype=jnp.int32)
    start_of = jnp.sum(jnp.where(top_i[:, :TOP_K, None] == experts, pad_start, 0), axis=-1)
    dest = start_of + rank[:, :TOP_K]
    n_blocks = t * TOP_K // bm + ne
    blk_row = jnp.arange(n_blocks, dtype=jnp.int32)[:, None] * bm
    block_e = jnp.minimum(jnp.sum((pad_end[None, :] <= blk_row).astype(jnp.int32), axis=1), ne - 1)
    n_valid = (pad_end[-1:] // bm).astype(jnp.int32)
    dest_flat = dest.reshape(-1).astype(jnp.int32)

    xs = _dispatch(h2p, dest_flat, n_blocks * bm, tm=256)
    ys = _expert_ffn(xs, block_e, n_valid, w_gu, b_gu, w_dn, b_dn, layer)
    return _combine(ys, dest_flat, xf, gates, mod_l, final_g, seq, half=128, final=final)


def kernel(x, c, ada_w, ada_b, norm1_g, w_in, rel_bias, lam_q1, lam_k1, lam_q2, lam_k2, attn_subln_g, conv_w, conv_b, dt_bias, a_log, d_skip, ssd_norm_g, w_branch, w_o, norm2_g, router_w, router_b, w_gate_up, b_gate_up, w_down, b_down, final_g):
    batch, seq, d = x.shape
    t = batch * seq
    depth = ada_w.shape[0]
    att_w = N_ATT_HEADS * ATT_V_DIM
    ssd_w = SSD_GROUPS * GROUP_W
    n_heads_ssd = SSD_GROUPS * SSD_HEADS_PER_GROUP
    conv_ch = conv_w.shape[2]
    q_cols = 2 * N_ATT_HEADS * ATT_HEAD_DIM
    q_scale = ATT_HEAD_DIM ** -0.5 * math.log2(math.e)
    z_col = 2 * q_cols + att_w
    xbc_col = z_col + ssd_w
    dt_col = xbc_col + conv_ch
    col_scale = jnp.concatenate([jnp.full((q_cols,), q_scale, F32), jnp.ones((dt_col - q_cols,), F32)])

    mod = _adaln(c, ada_w, ada_b)
    bias_tiles, far = _attn_bias_tiles(rel_bias, seq, ATT_Q_BLOCK, ATT_KV_BLOCK)
    expand = _head_expand()
    xf = x.reshape(t, d)
    tm_big = min(1024, seq)

    w_main_all = (w_in[:, :, :dt_col] * col_scale).astype(BF16)
    w_gate_all = w_in[:, :, dt_col + n_heads_ssd:].astype(BF16)
    w_dt_all = jnp.pad(w_in[:, :, dt_col:dt_col + n_heads_ssd],
                       ((0, 0), (0, 0), (0, LANES - n_heads_ssd))).astype(BF16)
    w_branch_all = w_branch.astype(BF16)
    w_o_all = w_o.astype(BF16)
    w_gate_up_all = w_gate_up.astype(BF16)

    for layer in range(depth):
        mod_l = mod[layer].reshape(batch, 1, 6 * d)
        inproj = functools.partial(_normmod_matmul, xf, norm1_g[layer], mod_l, 1, 0, layer=layer, seq=seq, tm=tm_big)
        proj = inproj(w_all=w_main_all, out_dtype=BF16, tn=1024)
        gates = inproj(w_all=w_gate_all, out_dtype=BF16, tn=1024)
        dt_raw = inproj(w_all=w_dt_all, out_dtype=F32, tn=LANES)

        lam_init = 0.8 - 0.6 * math.exp(-0.3 * layer)
        lamv = jnp.stack([lam_q1[layer], lam_k1[layer], lam_q2[layer], lam_k2[layer]], axis=0)
        o_att = _attention(proj, bias_tiles, far, lamv, attn_subln_g[layer], batch, seq, lam_init)
        o_ssd = _ssd(proj, dt_raw, expand, conv_w[layer], conv_b[layer], dt_bias[layer], a_log[layer],
                     d_skip[layer], ssd_norm_g[layer], batch, seq, z_col, xbc_col)

        merged = _merge(o_att, o_ssd, gates, w_branch_all, layer, tm=tm_big, tn=256)
        xf, h2p, logits = _wo_residual(merged, xf, w_o_all, layer, mod_l, norm2_g[layer],
                                       router_w[layer], router_b[layer], seq, tm=256)

        xf = _moe(h2p, logits, xf, mod_l, w_gate_up_all, b_gate_up, w_down, b_down, layer, final_g, seq,
                  final=(layer == depth - 1))
    return xf.reshape(batch, seq, d)
```

```python
import functools
import math

import numpy as np
import jax
import jax.numpy as jnp
from jax import lax
from jax.experimental import pallas as pl
from jax.experimental.pallas import tpu as pltpu

F32 = jnp.float32
BF16 = jnp.bfloat16
HIGHEST = lax.Precision.HIGHEST

DEPTH = 2
CHUNK = 64
N_ATT_HEADS = 8
ATT_HEAD_DIM = 128
ATT_V_DIM = 2 * ATT_HEAD_DIM
REL_BUCKETS = 32
REL_MAX_DIST = 128
SSD_HEAD_DIM = 64
SSD_GROUPS = 8
SSD_HEADS_PER_GROUP = 8
SSD_STATE = 128
SSD_CONV = 4
N_EXPERTS = 32
TOP_K = 4
SWIGLU_LIMIT = 7.0
SWIGLU_ALPHA = 1.702
NORM_EPS = 1e-6
SUBLN_EPS = 1e-5

LANES = 128
SUBLANES = 8
BF16_ROWS = 16
VMEM_LIMIT = 56 * 1024 * 1024

ATT_Q_BLOCK = 512
ATT_KV_BLOCK = 256
SSD_BLOCK = 256
EXPERT_BLOCK = 256
GROUP_W = SSD_HEADS_PER_GROUP * SSD_HEAD_DIM


def _params(semantics):
    return pltpu.CompilerParams(dimension_semantics=semantics, vmem_limit_bytes=VMEM_LIMIT)


def _silu(v):
    half = 0.5 * v
    return half + half * jnp.tanh(half)


def _split3(v):
    hi = v.astype(BF16)
    rest = v - hi.astype(F32)
    mid = rest.astype(BF16)
    lo = (rest - mid.astype(F32)).astype(BF16)
    return hi, mid, lo


def _adaln_kernel(c_ref, w_ref, b_ref, o_ref):
    ca = _silu(c_ref[...])
    o_ref[...] = jnp.dot(ca, w_ref[...], precision=HIGHEST, preferred_element_type=F32) + b_ref[...]


def _adaln(c, ada_w, ada_b):
    nl, d, n = ada_w.shape
    b = c.shape[0]
    tn = 1024
    return pl.pallas_call(
        _adaln_kernel,
        grid=(nl, n // tn),
        in_specs=[pl.BlockSpec((b, d), lambda l, j: (0, 0)),
                  pl.BlockSpec((None, d, tn), lambda l, j: (l, 0, j)),
                  pl.BlockSpec((None, 1, tn), lambda l, j: (l, 0, j))],
        out_specs=pl.BlockSpec((None, b, tn), lambda l, j: (l, 0, j)),
        out_shape=jax.ShapeDtypeStruct((nl, b, n), F32),
        compiler_params=_params(("parallel", "parallel")),
        name="adaln",
    )(c, ada_w, ada_b.reshape(nl, 1, n))


def _modulated_norm(x, g, sc, sh):
    y = x * lax.rsqrt(jnp.mean(x * x, axis=-1, keepdims=True) + NORM_EPS) * g
    return y * (1.0 + sc) + sh


def _prenorm_kernel(x_ref, g_ref, sc_ref, sh_ref, h_ref):
    h_ref[...] = _modulated_norm(x_ref[...], g_ref[...], sc_ref[...], sh_ref[...]).astype(h_ref.dtype)


def _prenorm(xf, g, mod_l, sc_idx, sh_idx, seq, tm):
    t, d = xf.shape
    tpb = seq // tm
    return pl.pallas_call(
        _prenorm_kernel,
        grid=(t // tm,),
        in_specs=[pl.BlockSpec((tm, d), lambda i: (i, 0)),
                  pl.BlockSpec((1, d), lambda i: (0, 0)),
                  pl.BlockSpec((None, 1, d), lambda i: (i // tpb, 0, sc_idx)),
                  pl.BlockSpec((None, 1, d), lambda i: (i // tpb, 0, sh_idx))],
        out_specs=pl.BlockSpec((tm, d), lambda i: (i, 0)),
        out_shape=jax.ShapeDtypeStruct((t, d), BF16),
        compiler_params=_params(("parallel",)),
        name="prenorm",
    )(xf, g.reshape(1, d), mod_l, mod_l)


def _inproj_kernel(h_ref, w_ref, o_ref):
    o_ref[...] = jnp.dot(h_ref[...], w_ref[...], preferred_element_type=F32).astype(o_ref.dtype)


def _inproj(h, w_all, layer, out_dtype, tm, tn):
    t, d = h.shape
    n = w_all.shape[2]
    return pl.pallas_call(
        _inproj_kernel,
        grid=(t // tm, n // tn),
        in_specs=[pl.BlockSpec((tm, d), lambda i, j: (i, 0)),
                  pl.BlockSpec((None, d, tn), lambda i, j: (layer, 0, j))],
        out_specs=pl.BlockSpec((tm, tn), lambda i, j: (i, j)),
        out_shape=jax.ShapeDtypeStruct((t, n), out_dtype),
        compiler_params=_params(("parallel", "arbitrary")),
        name="inproj",
    )(h, w_all)


def _t5_bucket(rel):
    half = REL_BUCKETS // 2
    exact = half // 2
    ret = jnp.where(rel > 0, half, 0)
    n = jnp.abs(rel)
    nf = jnp.maximum(n, 1).astype(jnp.float32)
    large = exact + (jnp.log(nf / exact) / math.log(REL_MAX_DIST / exact) * (half - exact)).astype(jnp.int32)
    large = jnp.minimum(large, half - 1)
    return ret + jnp.where(n < exact, n, large)


def _attn_bias_tiles(rel_bias, seq, qb, kb):
    assert kb >= REL_MAX_DIST and kb % CHUNK == 0 and qb % kb == 0
    nh = rel_bias.shape[1]
    lo = kb + qb - 1
    rel = jnp.arange(-lo, qb, dtype=jnp.int32)
    vec = (rel_bias[_t5_bucket(rel)].astype(F32) * math.log2(math.e)).T
    period = qb + kb
    jj = np.arange(period)
    jj = np.where(jj < qb, jj, jj - period)
    qq = np.arange(qb)[None, :]
    tiles = []
    for off in range(-1, qb // kb):
        row = vec[:, np.clip(lo + off * kb - jj, 0, lo + qb - 1)]
        skew = jnp.tile(row, (1, kb))[:, :kb * (period - 1)].reshape(nh, kb, period - 1)[:, :, :qb]
        kk = off * kb + np.arange(kb)[:, None]
        visible = (kk // CHUNK) <= (qq // CHUNK)
        tiles.append(jnp.where(visible[None], skew, -jnp.inf))
    far = rel_bias[_t5_bucket(jnp.full((1,), -(seq - 1), jnp.int32))][0].astype(F32)
    return jnp.stack(tiles, axis=1), far


def _attn_kernel(far_ref, q1_ref, q2_ref, k1_ref, k2_ref, v_ref, bias_ref, lam_ref, g_ref, o_ref,
                 vt_ref, acc1_ref, acc2_ref, sa1_ref, sa2_ref, sb1_ref, sb2_ref, *, qb, kb, dv, lam_init):
    h = pl.program_id(1)
    i = pl.program_id(2)
    far = far_ref[h] * math.log2(math.e)
    nt_dims = (((1,), (1,)), ((), ()))
    per_tile = qb // kb
    pad_rows = vt_ref.shape[1] - dv
    s_a = (sa1_ref, sa2_ref)
    s_b = (sb1_ref, sb2_ref)

    @pl.when(i == 0)
    def _():
        rr = lax.broadcasted_iota(jnp.int32, (dv, dv), 0)
        cc = lax.broadcasted_iota(jnp.int32, (dv, dv), 1)
        eye = jnp.where(rr == cc, 1.0, 0.0).astype(BF16)
        ones_row = jnp.where(lax.broadcasted_iota(jnp.int32, (pad_rows, kb), 0) == 0, 1.0, 0.0).astype(BF16)

        def tr_body(j, carry):
            vj = v_ref[pl.ds(pl.multiple_of(j * kb, kb), kb), :]
            vt_ref[j, pl.ds(0, dv), :] = vj.T
            vt_ref[j, pl.ds(dv, pad_rows), :] = ones_row
            return carry

        lax.fori_loop(0, vt_ref.shape[0], tr_body, 0)

    acc1_ref[...] = jnp.zeros(acc1_ref.shape, F32)
    acc2_ref[...] = jnp.zeros(acc2_ref.shape, F32)

    def scores(j):
        ks = pl.ds(pl.multiple_of(j * kb, kb), kb)
        return (lax.dot_general(k1_ref[ks, :], q1_ref[...], nt_dims, preferred_element_type=F32),
                lax.dot_general(k2_ref[ks, :], q2_ref[...], nt_dims, preferred_element_type=F32))

    def softmax_step(s, m, bias, shift):
        if bias is not None:
            s = s + bias
        m_new = jnp.maximum(m, jnp.max(s, axis=0, keepdims=True) + shift)
        return m_new, jnp.exp2(m - m_new), jnp.exp2(s - (m_new - shift)).astype(BF16)

    def step(carry, j, src, dst, bias, shift):
        m1, m2 = carry
        if dst is not None:
            dst[0][...], dst[1][...] = scores(j + 1)
        m1, alpha1, p1 = softmax_step(src[0][...], m1, bias, shift)
        pv1 = jnp.dot(vt_ref[j], p1, preferred_element_type=F32)
        m2, alpha2, p2 = softmax_step(src[1][...], m2, bias, shift)
        pv2 = jnp.dot(vt_ref[j], p2, preferred_element_type=F32)
        acc1_ref[...] = acc1_ref[...] * alpha1 + pv1
        acc2_ref[...] = acc2_ref[...] * alpha2 + pv2
        return m1, m2

    def pair(carry, j, kinds, final=False):
        (bias_a, shift_a), (bias_b, shift_b) = kinds
        carry = step(carry, j, s_a, s_b, bias_a, shift_a)
        return step(carry, j + 1, s_b, None if final else s_a, bias_b, shift_b)

    assert per_tile == 2
    far_kind = (None, far)
    neg = jnp.full((1, qb), -jnp.inf, F32)
    s_a[0][...], s_a[1][...] = scores(0)
    n_ff = jnp.maximum(i - 1, 0)
    carry = lax.fori_loop(0, n_ff, lambda t, c: pair(c, 2 * t, (far_kind, far_kind)), (neg, neg))
    carry = lax.fori_loop(n_ff, i, lambda t, c: pair(c, 2 * t, (far_kind, (bias_ref[0], 0.0))), carry)
    pair(carry, 2 * i, ((bias_ref[1], 0.0), (bias_ref[2], 0.0)), final=True)

    lamv = lam_ref[...]
    lam = (jnp.exp(jnp.sum(lamv[0:1] * lamv[1:2], axis=-1, keepdims=True))
           - jnp.exp(jnp.sum(lamv[2:3] * lamv[3:4], axis=-1, keepdims=True)) + lam_init)
    a1 = acc1_ref[...]
    a2 = acc2_ref[...]
    o = (a1[:dv] / a1[dv:dv + 1] - lam * (a2[:dv] / a2[dv:dv + 1])).T
    y = o * lax.rsqrt(jnp.mean(o * o, axis=-1, keepdims=True) + SUBLN_EPS) * g_ref[...]
    o_ref[...] = (y * (1.0 - lam_init)).astype(o_ref.dtype)


def _attention(proj, bias_tiles, far, lamv, subln_g, batch, seq, lam_init):
    qb, kb = ATT_Q_BLOCK, ATT_KV_BLOCK
    nq = seq // qb
    nh = N_ATT_HEADS
    dk, dv = ATT_HEAD_DIM, ATT_V_DIM
    t = batch * seq
    k_off = 2 * nh
    v_off = 4 * nh * dk // dv
    kern = functools.partial(_attn_kernel, qb=qb, kb=kb, dv=dv, lam_init=lam_init)
    return pl.pallas_call(
        kern,
        grid=(batch, nh, nq),
        in_specs=[pl.BlockSpec(memory_space=pltpu.SMEM),
                  pl.BlockSpec((qb, dk), lambda b, h, i: (b * nq + i, h)),
                  pl.BlockSpec((qb, dk), lambda b, h, i: (b * nq + i, nh + h)),
                  pl.BlockSpec((seq, dk), lambda b, h, i: (b, k_off + h)),
                  pl.BlockSpec((seq, dk), lambda b, h, i: (b, k_off + nh + h)),
                  pl.BlockSpec((seq, dv), lambda b, h, i: (b, v_off + h)),
                  pl.BlockSpec((None, qb // kb + 1, kb, qb), lambda b, h, i: (h, 0, 0, 0)),
                  pl.BlockSpec((4, dk), lambda b, h, i: (0, 0)),
                  pl.BlockSpec((1, dv), lambda b, h, i: (0, 0))],
        out_specs=pl.BlockSpec((qb, dv), lambda b, h, i: (b * nq + i, h)),
        out_shape=jax.ShapeDtypeStruct((t, nh * dv), BF16),
        scratch_shapes=[pltpu.VMEM((seq // kb, dv + BF16_ROWS, kb), BF16),
                        pltpu.VMEM((dv + BF16_ROWS, qb), F32), pltpu.VMEM((dv + BF16_ROWS, qb), F32)]
        + [pltpu.VMEM((kb, qb), F32)] * 4,
        compiler_params=_params(("parallel", "parallel", "arbitrary")),
        name="diff_attention",
    )(far, proj, proj, proj, proj, proj, bias_tiles, lamv, subln_g.reshape(1, dv))


def _conv_silu(x_ref, hist_ref, w_ref, b_ref, blk):
    x = x_ref[...].astype(F32)
    hist_ref[pl.ds(SUBLANES, blk), :] = x
    w = w_ref[...]
    acc = x * w[SSD_CONV - 1:SSD_CONV] + b_ref[...]
    for j in range(1, SSD_CONV):
        acc = acc + hist_ref[pl.ds(SUBLANES - j, blk), :] * w[SSD_CONV - 1 - j:SSD_CONV - j]
    hist_ref[pl.ds(0, SUBLANES), :] = x[blk - SUBLANES:]
    return _silu(acc)


def _ssd_kernel(xs_ref, bm_ref, cm_ref, z_ref, dt_ref, e_ref, cwx_ref, cwb_ref, cwc_ref, cbx_ref, cbb_ref,
                cbc_ref, dtb_ref, alog_ref, dskip_ref, ng_ref, o_ref,
                state_ref, tx_ref, tb_ref, tc_ref, act_ref, *, blk):
    g = pl.program_id(1)

    @pl.when(pl.program_id(2) == 0)
    def _():
        state_ref[...] = jnp.zeros(state_ref.shape, F32)
        for hist_ref in (tx_ref, tb_ref, tc_ref):
            hist_ref[pl.ds(0, SUBLANES), :] = jnp.zeros((SUBLANES, hist_ref.shape[1]), F32)

    xs = _conv_silu(xs_ref, tx_ref, cwx_ref, cbx_ref, blk)
    bm = _conv_silu(bm_ref, tb_ref, cwb_ref, cbb_ref, blk)
    cm = _conv_silu(cm_ref, tc_ref, cwc_ref, cbc_ref, blk)

    dt = jax.nn.softplus(dt_ref[...] + dtb_ref[...])
    da = dt * (-jnp.exp(alog_ref[...]))
    row = lax.broadcasted_iota(jnp.int32, (blk, blk), 0)
    col = lax.broadcasted_iota(jnp.int32, (blk, blk), 1)
    causal = col <= row
    tri = jnp.where(causal, 1.0, 0.0).astype(BF16)
    acum = sum(jnp.dot(tri, part, preferred_element_type=F32) for part in _split3(da))
    both_e = jnp.dot(jnp.concatenate([jnp.concatenate(_split3(dt), axis=1),
                                      jnp.concatenate(_split3(acum), axis=1)], axis=0),
                     e_ref[...], preferred_element_type=F32)
    dt_e = both_e[:blk]
    acum_e = both_e[blk:]
    act_ref[...] = acum.T

    xdt = xs * dt_e
    xdt_b = xdt.astype(BF16)
    cm_b = cm.astype(BF16)
    cb = lax.dot_general(cm_b, bm.astype(BF16), (((1,), (1,)), ((), ())), preferred_element_type=F32)
    lane = lax.broadcasted_iota(jnp.int32, (blk, LANES), 1)
    halves = (lane < SSD_HEAD_DIM, lane >= SSD_HEAD_DIM)
    pieces = []
    for pair in range(GROUP_W // LANES):
        xp = xdt_b[:, pair * LANES:(pair + 1) * LANES]
        yp = jnp.zeros((blk, LANES), F32)
        for hh in range(2):
            r = 2 * pair + hh
            a_col = acum_e[:, r * SSD_HEAD_DIM:r * SSD_HEAD_DIM + 1]
            a_row = act_ref[pl.ds(g * SSD_HEADS_PER_GROUP + r, 1), :]
            decay = jnp.exp(jnp.where(causal, a_col - a_row, -jnp.inf))
            mat = (cb * decay).astype(BF16)
            yp = yp + jnp.dot(mat, jnp.where(halves[hh], xp, jnp.zeros_like(xp)), preferred_element_type=F32)
        pieces.append(yp)
    y = jnp.concatenate(pieces, axis=1)

    st = state_ref[...]
    y = y + jnp.exp(acum_e) * jnp.dot(cm_b, st.astype(BF16), preferred_element_type=F32)
    a_last = acum_e[blk - 1:blk, :]
    wgt = (xdt * jnp.exp(a_last - acum_e)).astype(BF16)
    state_ref[...] = st * jnp.exp(a_last) + jnp.dot(bm.T.astype(BF16), wgt, preferred_element_type=F32)

    y = y + dskip_ref[...] * xs
    y = y * _silu(z_ref[...].astype(F32))
    y = y * lax.rsqrt(jnp.mean(y * y, axis=-1, keepdims=True) + SUBLN_EPS)
    o_ref[...] = (y * ng_ref[...]).astype(o_ref.dtype)


def _ssd(proj, dt_raw, expand, conv_w, conv_b, dt_bias, a_log, d_skip, norm_g, batch, seq, z_col, xbc_col):
    blk = SSD_BLOCK
    nc = seq // blk
    ng = SSD_GROUPS
    t = batch * seq
    width = ng * GROUP_W
    heads = ng * SSD_HEADS_PER_GROUP
    z_blk = z_col // GROUP_W
    xs_blk = xbc_col // GROUP_W
    b_blk = (xbc_col + width) // SSD_STATE
    c_blk = b_blk + ng
    cw_b_blk = width // SSD_STATE
    pad = LANES - heads
    row = lambda b, g, c: b * nc + c
    dtb = jnp.pad(dt_bias, (0, pad)).reshape(1, LANES)
    alog = jnp.pad(a_log, (0, pad)).reshape(1, LANES)
    dskip = jnp.repeat(d_skip, SSD_HEAD_DIM).reshape(1, width)
    cb2 = conv_b.reshape(1, -1)
    kern = functools.partial(_ssd_kernel, blk=blk)
    return pl.pallas_call(
        kern,
        grid=(batch, ng, nc),
        in_specs=[pl.BlockSpec((blk, GROUP_W), lambda b, g, c: (row(b, g, c), xs_blk + g)),
                  pl.BlockSpec((blk, SSD_STATE), lambda b, g, c: (row(b, g, c), b_blk + g)),
                  pl.BlockSpec((blk, SSD_STATE), lambda b, g, c: (row(b, g, c), c_blk + g)),
                  pl.BlockSpec((blk, GROUP_W), lambda b, g, c: (row(b, g, c), z_blk + g)),
                  pl.BlockSpec((blk, LANES), lambda b, g, c: (row(b, g, c), 0)),
                  pl.BlockSpec((None, 3 * LANES, GROUP_W), lambda b, g, c: (g, 0, 0)),
                  pl.BlockSpec((SSD_CONV, GROUP_W), lambda b, g, c: (0, g)),
                  pl.BlockSpec((SSD_CONV, SSD_STATE), lambda b, g, c: (0, cw_b_blk + g)),
                  pl.BlockSpec((SSD_CONV, SSD_STATE), lambda b, g, c: (0, cw_b_blk + ng + g)),
                  pl.BlockSpec((1, GROUP_W), lambda b, g, c: (0, g)),
                  pl.BlockSpec((1, SSD_STATE), lambda b, g, c: (0, cw_b_blk + g)),
                  pl.BlockSpec((1, SSD_STATE), lambda b, g, c: (0, cw_b_blk + ng + g)),
                  pl.BlockSpec((1, LANES), lambda b, g, c: (0, 0)),
                  pl.BlockSpec((1, LANES), lambda b, g, c: (0, 0)),
                  pl.BlockSpec((1, GROUP_W), lambda b, g, c: (0, g)),
                  pl.BlockSpec((1, GROUP_W), lambda b, g, c: (0, g))],
        out_specs=pl.BlockSpec((blk, GROUP_W), lambda b, g, c: (row(b, g, c), g)),
        out_shape=jax.ShapeDtypeStruct((t, width), BF16),
        scratch_shapes=[pltpu.VMEM((SSD_STATE, GROUP_W), F32),
                        pltpu.VMEM((SUBLANES + blk, GROUP_W), F32),
                        pltpu.VMEM((SUBLANES + blk, SSD_STATE), F32),
                        pltpu.VMEM((SUBLANES + blk, SSD_STATE), F32),
                        pltpu.VMEM((LANES, blk), F32)],
        compiler_params=_params(("parallel", "parallel", "arbitrary")),
        name="ssd",
    )(proj, proj, proj, proj, dt_raw, expand, conv_w, conv_w, conv_w, cb2, cb2, cb2, dtb, alog, dskip,
      norm_g.reshape(1, width))


def _head_expand():
    e = np.zeros((SSD_GROUPS, LANES, GROUP_W), np.float32)
    for g in range(SSD_GROUPS):
        for r in range(SSD_HEADS_PER_GROUP):
            e[g, g * SSD_HEADS_PER_GROUP + r, r * SSD_HEAD_DIM:(r + 1) * SSD_HEAD_DIM] = 1.0
    return jnp.asarray(np.tile(e, (1, 3, 1)), dtype=BF16)


def _merge_kernel(oa_ref, os_ref, wa_ref, ws1_ref, ws2_ref, ga_ref, gs_ref, o_ref):
    ka = wa_ref.shape[0]
    ya = jnp.dot(oa_ref[...], wa_ref[...], preferred_element_type=F32)
    ys = (jnp.dot(os_ref[:, pl.ds(0, ka)], ws1_ref[...], preferred_element_type=F32)
          + jnp.dot(os_ref[:, pl.ds(ka, ka)], ws2_ref[...], preferred_element_type=F32))
    merged = jax.nn.sigmoid(ga_ref[...].astype(F32)) * ya + jax.nn.sigmoid(gs_ref[...].astype(F32)) * ys
    o_ref[...] = merged.astype(o_ref.dtype)


def _merge(o_att, o_ssd, gates, w_branch_all, layer, tm, tn):
    t, ka = o_att.shape
    ks = o_ssd.shape[1]
    d = w_branch_all.shape[2]
    assert ks == 2 * ka
    w_spec = lambda blk: pl.BlockSpec((None, ka, tn), lambda i, j: (layer, blk, j))
    gs_blk = d // tn
    return pl.pallas_call(
        _merge_kernel,
        grid=(t // tm, d // tn),
        in_specs=[pl.BlockSpec((tm, ka), lambda i, j: (i, 0)),
                  pl.BlockSpec((tm, ks), lambda i, j: (i, 0)),
                  w_spec(0), w_spec(1), w_spec(2),
                  pl.BlockSpec((tm, tn), lambda i, j: (i, j)),
                  pl.BlockSpec((tm, tn), lambda i, j: (i, gs_blk + j))],
        out_specs=pl.BlockSpec((tm, tn), lambda i, j: (i, j)),
        out_shape=jax.ShapeDtypeStruct((t, d), BF16),
        compiler_params=_params(("parallel", "arbitrary")),
        name="branch_merge",
    )(o_att, o_ssd, w_branch_all, w_branch_all, w_branch_all, gates, gates)


def _pack_bf16_pair(v):
    n = v.shape[1] // 2
    lo = lax.bitcast_convert_type(v[:, :n].astype(BF16).astype(F32), jnp.uint32)
    hi = lax.bitcast_convert_type(v[:, n:].astype(BF16).astype(F32), jnp.uint32)
    return (lo >> 16) | (hi & jnp.uint32(0xFFFF0000))


def _unpack_bf16_pair(p):
    lo = lax.bitcast_convert_type(p << 16, F32)
    hi = lax.bitcast_convert_type(p & jnp.uint32(0xFFFF0000), F32)
    return lo, hi


def _wo_kernel(m_ref, x_ref, w_ref, g1_ref, ng_ref, sc_ref, sh_ref, rw_ref, rb_ref, xo_ref, hp_ref, lg_ref):
    y = jnp.dot(m_ref[...], w_ref[...], preferred_element_type=F32)
    x = x_ref[...] + g1_ref[...] * y
    xo_ref[...] = x
    h2 = _modulated_norm(x, ng_ref[...], sc_ref[...], sh_ref[...])
    hp_ref[...] = _pack_bf16_pair(h2)
    h_hi = h2.astype(BF16)
    h_lo = (h2 - h_hi.astype(F32)).astype(BF16)
    both = jnp.dot(h_hi, rw_ref[...], preferred_element_type=F32)
    cross = jnp.dot(h_lo, rw_ref[:, pl.ds(0, LANES)], preferred_element_type=F32)
    lg_ref[...] = both[:, :LANES] + (both[:, LANES:] + cross) + rb_ref[...]


def _wo_residual(merged, xf, w_o_all, layer, mod_l, norm2_g, router_w, router_b, seq, tm):
    t, d = xf.shape
    ne = router_w.shape[1]
    tpb = seq // tm
    mod_spec = lambda idx: pl.BlockSpec((None, 1, d), lambda i: (i // tpb, 0, idx))
    rw = jnp.pad(router_w, ((0, 0), (0, LANES - ne)))
    rw_hi = rw.astype(BF16)
    rw = jnp.concatenate([rw_hi, (rw - rw_hi.astype(F32)).astype(BF16)], axis=1)
    rb = jnp.pad(router_b, (0, LANES - ne), constant_values=-jnp.inf).reshape(1, LANES)
    return pl.pallas_call(
        _wo_kernel,
        grid=(t // tm,),
        in_specs=[pl.BlockSpec((tm, d), lambda i: (i, 0)),
                  pl.BlockSpec((tm, d), lambda i: (i, 0)),
                  pl.BlockSpec((None, d, d), lambda i: (layer, 0, 0)),
                  mod_spec(2),
                  pl.BlockSpec((1, d), lambda i: (0, 0)),
                  mod_spec(4), mod_spec(3),
                  pl.BlockSpec((d, 2 * LANES), lambda i: (0, 0)),
                  pl.BlockSpec((1, LANES), lambda i: (0, 0))],
        out_specs=[pl.BlockSpec((tm, d), lambda i: (i, 0)), pl.BlockSpec((tm, d // 2), lambda i: (i, 0)),
                   pl.BlockSpec((tm, LANES), lambda i: (i, 0))],
        out_shape=[jax.ShapeDtypeStruct((t, d), F32), jax.ShapeDtypeStruct((t, d // 2), jnp.uint32),
                   jax.ShapeDtypeStruct((t, LANES), F32)],
        compiler_params=_params(("parallel",)),
        name="wo_residual",
    )(merged, xf, w_o_all, mod_l, norm2_g.reshape(1, d), mod_l, mod_l, rw, rb)


def _router_kernel(lg_ref, idx_ref, gate_ref, rank_ref, cnt_ref, run_ref, *, tm):
    @pl.when(pl.program_id(0) == 0)
    def _():
        run_ref[...] = jnp.zeros(run_ref.shape, F32)

    logits = lg_ref[...]
    lane = lax.broadcasted_iota(jnp.int32, (tm, LANES), 1)
    vals = logits
    picked = jnp.zeros((tm, LANES), F32)
    top_v, top_sel, top_i = [], [], []
    for _ in range(TOP_K):
        m = jnp.max(vals, axis=-1, keepdims=True)
        idx = jnp.min(jnp.where(vals == m, lane, LANES), axis=-1, keepdims=True)
        sel = lane == idx
        top_v.append(m)
        top_i.append(idx)
        top_sel.append(sel)
        vals = jnp.where(sel, -jnp.inf, vals)
        picked = picked + sel.astype(F32)

    row = lax.broadcasted_iota(jnp.int32, (tm, tm), 0)
    col = lax.broadcasted_iota(jnp.int32, (tm, tm), 1)
    before = jnp.dot((col < row).astype(BF16), picked.astype(BF16), preferred_element_type=F32) + run_ref[...]
    run_ref[...] = run_ref[...] + jnp.sum(picked, axis=0, keepdims=True)
    cnt_ref[...] = run_ref[...]

    exps = [jnp.exp(v - top_v[0]) for v in top_v]
    denom = exps[0] + exps[1] + exps[2] + exps[3]
    idx_out = jnp.zeros((tm, LANES), jnp.int32)
    rank_out = jnp.zeros((tm, LANES), jnp.int32)
    gate_out = jnp.zeros((tm, LANES), F32)
    for k in range(TOP_K):
        rank_k = jnp.sum(jnp.where(top_sel[k], before, 0.0), axis=-1, keepdims=True).astype(jnp.int32)
        idx_out = jnp.where(lane == k, top_i[k], idx_out)
        rank_out = jnp.where(lane == k, rank_k, rank_out)
        gate_out = jnp.where(lane == k, exps[k] / denom, gate_out)
    idx_ref[...] = idx_out
    rank_ref[...] = rank_out
    gate_ref[...] = gate_out


def _router(logits, tm):
    t = logits.shape[0]
    kern = functools.partial(_router_kernel, tm=tm)
    tok_spec = pl.BlockSpec((tm, LANES), lambda i: (i, 0))
    return pl.pallas_call(
        kern,
        grid=(t // tm,),
        in_specs=[tok_spec],
        out_specs=[tok_spec, tok_spec, tok_spec, pl.BlockSpec((1, LANES), lambda i: (0, 0))],
        out_shape=[jax.ShapeDtypeStruct((t, LANES), jnp.int32), jax.ShapeDtypeStruct((t, LANES), F32),
                   jax.ShapeDtypeStruct((t, LANES), jnp.int32), jax.ShapeDtypeStruct((1, LANES), F32)],
        scratch_shapes=[pltpu.VMEM((1, LANES), F32)],
        compiler_params=_params(("arbitrary",)),
        name="router",
    )(logits)


def _dispatch_kernel(dest_ref, h_ref, xs_in_hbm, xs_hbm, sem, *, tm):
    del xs_in_hbm

    def row_copy(t, k):
        return pltpu.make_async_copy(h_ref.at[pl.ds(t, 1)],
                                     xs_hbm.at[pl.ds(dest_ref[t * TOP_K + k], 1)], sem)

    for t in range(tm):
        for k in range(TOP_K):
            row_copy(t, k).start()
    for _ in range(TOP_K):
        pltpu.make_async_copy(h_ref, xs_hbm.at[pl.ds(0, tm)], sem).wait()


def _dispatch(h2, dest_flat, n_rows, tm):
    t, d = h2.shape
    kern = functools.partial(_dispatch_kernel, tm=tm)
    return pl.pallas_call(
        kern,
        grid=(t // tm,),
        in_specs=[pl.BlockSpec((tm * TOP_K,), lambda i: (i,), memory_space=pltpu.SMEM),
                  pl.BlockSpec((tm, d), lambda i: (i, 0)),
                  pl.BlockSpec(memory_space=pl.ANY)],
        out_specs=pl.BlockSpec(memory_space=pl.ANY),
        out_shape=jax.ShapeDtypeStruct((n_rows, d), h2.dtype),
        scratch_shapes=[pltpu.SemaphoreType.DMA(())],
        input_output_aliases={2: 0},
        compiler_params=_params(("arbitrary",)),
        name="moe_dispatch",
    )(dest_flat, h2, jnp.zeros((n_rows, d), h2.dtype))


def _expert_gu_kernel(be_ref, nv_ref, x_ref, w_ref, b_ref, o_ref, *, ff):
    del be_ref
    valid = pl.program_id(0) < nv_ref[0]

    @pl.when(valid)
    def _():
        x_lo, x_hi = _unpack_bf16_pair(x_ref[...])
        half = x_lo.shape[1]
        gu = (jnp.dot(x_lo.astype(BF16), w_ref[pl.ds(0, half), :], preferred_element_type=F32)
              + jnp.dot(x_hi.astype(BF16), w_ref[pl.ds(half, half), :], preferred_element_type=F32) + b_ref[...])
        g = jnp.minimum(gu[:, :ff], SWIGLU_LIMIT)
        u = jnp.clip(gu[:, ff:], -SWIGLU_LIMIT, SWIGLU_LIMIT)
        o_ref[...] = ((u + 1.0) * (g * jax.nn.sigmoid(SWIGLU_ALPHA * g))).astype(o_ref.dtype)

    @pl.when(jnp.logical_not(valid))
    def _():
        o_ref[...] = jnp.zeros(o_ref.shape, o_ref.dtype)


def _expert_down_kernel(be_ref, nv_ref, a_ref, w_ref, b_ref, o_ref, wb_ref):
    i = pl.program_id(0)
    valid = i < nv_ref[0]
    new_expert = jnp.logical_or(i == 0, be_ref[i] != be_ref[jnp.maximum(i - 1, 0)])

    @pl.when(jnp.logical_and(valid, new_expert))
    def _():
        wb_ref[...] = w_ref[...].astype(BF16)

    @pl.when(valid)
    def _():
        o_ref[...] = _pack_bf16_pair(jnp.dot(a_ref[...], wb_ref[...], preferred_element_type=F32) + b_ref[...])

    @pl.when(jnp.logical_not(valid))
    def _():
        o_ref[...] = jnp.zeros(o_ref.shape, o_ref.dtype)


def _expert_ffn(xs, block_e, n_valid, w_gu, b_gu, w_dn, b_dn, layer):
    n_rows, dp = xs.shape
    bm = EXPERT_BLOCK
    n_blocks = n_rows // bm
    nl, ne, d, ff2 = w_gu.shape
    ff = ff2 // 2
    act = pl.pallas_call(
        functools.partial(_expert_gu_kernel, ff=ff),
        grid_spec=pltpu.PrefetchScalarGridSpec(
            num_scalar_prefetch=2, grid=(n_blocks,),
            in_specs=[pl.BlockSpec((bm, dp), lambda i, be, nv: (i, 0)),
                      pl.BlockSpec((None, None, d, ff2), lambda i, be, nv: (layer, be[i], 0, 0)),
                      pl.BlockSpec((None, None, 1, ff2), lambda i, be, nv: (layer, be[i], 0, 0))],
            out_specs=pl.BlockSpec((bm, ff), lambda i, be, nv: (i, 0))),
        out_shape=jax.ShapeDtypeStruct((n_rows, ff), BF16),
        compiler_params=_params(("arbitrary",)),
        name="expert_gate_up",
    )(block_e, n_valid, xs, w_gu, b_gu.reshape(nl, ne, 1, ff2))
    return pl.pallas_call(
        _expert_down_kernel,
        grid_spec=pltpu.PrefetchScalarGridSpec(
            num_scalar_prefetch=2, grid=(n_blocks,),
            in_specs=[pl.BlockSpec((bm, ff), lambda i, be, nv: (i, 0)),
                      pl.BlockSpec((None, None, ff, d), lambda i, be, nv: (layer, be[i], 0, 0)),
                      pl.BlockSpec((None, None, 1, d), lambda i, be, nv: (layer, be[i], 0, 0))],
            out_specs=pl.BlockSpec((bm, dp), lambda i, be, nv: (i, 0)),
            scratch_shapes=[pltpu.VMEM((ff, d), BF16)]),
        out_shape=jax.ShapeDtypeStruct((n_rows, dp), jnp.uint32),
        compiler_params=_params(("arbitrary",)),
        name="expert_down",
    )(block_e, n_valid, act, w_dn, b_dn.reshape(nl, ne, 1, d))


def _combine_kernel(dest_ref, dest_next_ref, ys_hbm, x_ref, gate_ref, g2_ref, ng_ref, *rest, half, final):
    if final:
        o_ref, buf_a, buf_b, sem_a, sem_b = rest
    else:
        sc_ref, sh_ref, o_ref, h_ref, buf_a, buf_b, sem_a, sem_b = rest
    i = pl.program_id(0)

    def start_rows(dref, base, buf, sem):
        for t in range(half):
            for k in range(TOP_K):
                pltpu.make_async_copy(ys_hbm.at[pl.ds(dref[base + t * TOP_K + k], 1)],
                                      buf.at[k, pl.ds(t, 1)], sem).start()

    def wait_rows(buf, sem):
        for k in range(TOP_K):
            pltpu.make_async_copy(ys_hbm.at[pl.ds(0, half)], buf.at[k], sem).wait()

    def finish(buf, rows):
        gates = gate_ref[rows, :]
        moe_lo, moe_hi = (gates[:, 0:1] * part for part in _unpack_bf16_pair(buf[0]))
        for k in range(1, TOP_K):
            lo, hi = _unpack_bf16_pair(buf[k])
            moe_lo = moe_lo + gates[:, k:k + 1] * lo
            moe_hi = moe_hi + gates[:, k:k + 1] * hi
        x = x_ref[rows, :] + g2_ref[...] * jnp.concatenate([moe_lo, moe_hi], axis=1)
        if final:
            o_ref[rows, :] = x * lax.rsqrt(jnp.mean(x * x, axis=-1, keepdims=True) + NORM_EPS) * ng_ref[...]
        else:
            o_ref[rows, :] = x
            h_ref[rows, :] = _modulated_norm(x, ng_ref[...], sc_ref[...], sh_ref[...]).astype(h_ref.dtype)

    @pl.when(i == 0)
    def _():
        start_rows(dest_ref, 0, buf_a, sem_a)

    wait_rows(buf_a, sem_a)
    start_rows(dest_ref, half * TOP_K, buf_b, sem_b)
    finish(buf_a, pl.ds(0, half))
    wait_rows(buf_b, sem_b)
    start_rows(dest_next_ref, 0, buf_a, sem_a)
    finish(buf_b, pl.ds(half, half))

    @pl.when(i == pl.num_programs(0) - 1)
    def _():
        wait_rows(buf_a, sem_a)


def _combine(ys, dest_flat, xf, gates, mod_l, norm_g, mod_next, seq, half):
    t, d = xf.shape
    final = mod_next is None
    tm = 2 * half
    steps = t // tm
    tpb = seq // tm
    kern = functools.partial(_combine_kernel, half=half, final=final)
    tile = pl.BlockSpec((tm, d), lambda i: (i, 0))
    mod_spec = lambda idx: pl.BlockSpec((None, 1, d), lambda i: (i // tpb, 0, idx))
    in_specs = [pl.BlockSpec((tm * TOP_K,), lambda i: (i,), memory_space=pltpu.SMEM),
                pl.BlockSpec((tm * TOP_K,), lambda i: (jnp.minimum(i + 1, steps - 1),), memory_space=pltpu.SMEM),
                pl.BlockSpec(memory_space=pl.ANY),
                tile,
                pl.BlockSpec((tm, LANES), lambda i: (i, 0)),
                mod_spec(5),
                pl.BlockSpec((1, d), lambda i: (0, 0))]
    args = [dest_flat, dest_flat, ys, xf, gates, mod_l, norm_g.reshape(1, d)]
    if final:
        out_specs, out_shape = tile, jax.ShapeDtypeStruct((t, d), F32)
    else:
        in_specs += [mod_spec(1), mod_spec(0)]
        args += [mod_next, mod_next]
        out_specs = [tile, tile]
        out_shape = [jax.ShapeDtypeStruct((t, d), F32), jax.ShapeDtypeStruct((t, d), BF16)]
    return pl.pallas_call(
        kern,
        grid=(steps,),
        in_specs=in_specs,
        out_specs=out_specs,
        out_shape=out_shape,
        scratch_shapes=[pltpu.VMEM((TOP_K, half, d // 2), jnp.uint32), pltpu.VMEM((TOP_K, half, d // 2), jnp.uint32),
                        pltpu.SemaphoreType.DMA(()), pltpu.SemaphoreType.DMA(())],
        compiler_params=_params(("arbitrary",)),
        name="moe_combine",
    )(*args)


def _moe(h2p, logits, xf, mod_l, w_gu, b_gu, w_dn, b_dn, layer, norm_g, mod_next, seq):
    t = h2p.shape[0]
    ne = w_gu.shape[1]
    bm = EXPERT_BLOCK
    top_i, gates, rank, counts = _router(logits, tm=256)
    cnt = counts[0, :ne].astype(jnp.int32)
    padded = (cnt + bm - 1) // bm * bm
    pad_end = jnp.cumsum(padded)
    pad_start = pad_end - padded
    experts = jnp.arange(ne, dtype=jnp.int32)
    start_of = jnp.sum(jnp.where(top_i[:, :TOP_K, None] == experts, pad_start, 0), axis=-1)
    dest = start_of + rank[:, :TOP_K]
    n_blocks = t * TOP_K // bm + ne
    blk_row = jnp.arange(n_blocks, dtype=jnp.int32)[:, None] * bm
    block_e = jnp.minimum(jnp.sum((pad_end[None, :] <= blk_row).astype(jnp.int32), axis=1), ne - 1)
    n_valid = (pad_end[-1:] // bm).astype(jnp.int32)
    dest_flat = dest.reshape(-1).astype(jnp.int32)

    xs = _dispatch(h2p, dest_flat, n_blocks * bm, tm=256)
    ys = _expert_ffn(xs, block_e, n_valid, w_gu, b_gu, w_dn, b_dn, layer)
    return _combine(ys, dest_flat, xf, gates, mod_l, norm_g, mod_next, seq, half=128)


def kernel(x, c, ada_w, ada_b, norm1_g, w_in, rel_bias, lam_q1, lam_k1, lam_q2, lam_k2, attn_subln_g, conv_w, conv_b, dt_bias, a_log, d_skip, ssd_norm_g, w_branch, w_o, norm2_g, router_w, router_b, w_gate_up, b_gate_up, w_down, b_down, final_g):
    batch, seq, d = x.shape
    t = batch * seq
    depth = ada_w.shape[0]
    att_w = N_ATT_HEADS * ATT_V_DIM
    ssd_w = SSD_GROUPS * GROUP_W
    n_heads_ssd = SSD_GROUPS * SSD_HEADS_PER_GROUP
    conv_ch = conv_w.shape[2]
    q_cols = 2 * N_ATT_HEADS * ATT_HEAD_DIM
    q_scale = ATT_HEAD_DIM ** -0.5 * math.log2(math.e)
    z_col = 2 * q_cols + att_w
    xbc_col = z_col + ssd_w
    dt_col = xbc_col + conv_ch
    col_scale = jnp.concatenate([jnp.full((q_cols,), q_scale, F32), jnp.ones((dt_col - q_cols,), F32)])

    mod = _adaln(c, ada_w, ada_b)
    bias_tiles, far = _attn_bias_tiles(rel_bias, seq, ATT_Q_BLOCK, ATT_KV_BLOCK)
    expand = _head_expand()
    xf = x.reshape(t, d)
    tm_big = min(1024, seq)

    w_main_all = (w_in[:, :, :dt_col] * col_scale).astype(BF16)
    w_gate_all = w_in[:, :, dt_col + n_heads_ssd:].astype(BF16)
    w_dt_all = jnp.pad(w_in[:, :, dt_col:dt_col + n_heads_ssd],
                       ((0, 0), (0, 0), (0, LANES - n_heads_ssd))).astype(BF16)
    w_branch_all = w_branch.astype(BF16)
    w_o_all = w_o.astype(BF16)
    w_gate_up_all = w_gate_up.astype(BF16)

    mods = [mod[layer].reshape(batch, 1, 6 * d) for layer in range(depth)]
    h = _prenorm(xf, norm1_g[0], mods[0], 1, 0, seq, tm=tm_big)
    for layer in range(depth):
        mod_l = mods[layer]
        proj = _inproj(h, w_main_all, layer, BF16, tm=tm_big, tn=1024)
        gates = _inproj(h, w_gate_all, layer, BF16, tm=tm_big, tn=1024)
        dt_raw = _inproj(h, w_dt_all, layer, F32, tm=tm_big, tn=LANES)

        lam_init = 0.8 - 0.6 * math.exp(-0.3 * layer)
        lamv = jnp.stack([lam_q1[layer], lam_k1[layer], lam_q2[layer], lam_k2[layer]], axis=0)
        o_att = _attention(proj, bias_tiles, far, lamv, attn_subln_g[layer], batch, seq, lam_init)
        o_ssd = _ssd(proj, dt_raw, expand, conv_w[layer], conv_b[layer], dt_bias[layer], a_log[layer],
                     d_skip[layer], ssd_norm_g[layer], batch, seq, z_col, xbc_col)

        merged = _merge(o_att, o_ssd, gates, w_branch_all, layer, tm=tm_big, tn=256)
        xf, h2p, logits = _wo_residual(merged, xf, w_o_all, layer, mod_l, norm2_g[layer],
                                       router_w[layer], router_b[layer], seq, tm=256)

        moe = functools.partial(_moe, h2p, logits, xf, mod_l, w_gate_up_all, b_gate_up, w_down, b_down, layer)
        if layer == depth - 1:
            xf = moe(final_g, None, seq)
        else:
            xf, h = moe(norm1_g[layer + 1], mods[layer + 1], seq)
    return xf.reshape(batch, seq, d)
```

```python
import functools
import math

import numpy as np
import jax
import jax.numpy as jnp
from jax import lax
from jax.experimental import pallas as pl
from jax.experimental.pallas import tpu as pltpu

F32 = jnp.float32
BF16 = jnp.bfloat16
HIGHEST = lax.Precision.HIGHEST

DEPTH = 2
CHUNK = 64
N_ATT_HEADS = 8
ATT_HEAD_DIM = 128
ATT_V_DIM = 2 * ATT_HEAD_DIM
REL_BUCKETS = 32
REL_MAX_DIST = 128
SSD_HEAD_DIM = 64
SSD_GROUPS = 8
SSD_HEADS_PER_GROUP = 8
SSD_STATE = 128
SSD_CONV = 4
N_EXPERTS = 32
TOP_K = 4
SWIGLU_LIMIT = 7.0
SWIGLU_ALPHA = 1.702
NORM_EPS = 1e-6
SUBLN_EPS = 1e-5

LANES = 128
SUBLANES = 8
BF16_ROWS = 16
VMEM_LIMIT = 56 * 1024 * 1024

ATT_Q_BLOCK = 512
ATT_KV_BLOCK = 256
SSD_BLOCK = 256
EXPERT_BLOCK = 256
GROUP_W = SSD_HEADS_PER_GROUP * SSD_HEAD_DIM


def _params(semantics):
    return pltpu.CompilerParams(dimension_semantics=semantics, vmem_limit_bytes=VMEM_LIMIT)


def _silu(v):
    half = 0.5 * v
    return half + half * jnp.tanh(half)


def _split3(v):
    hi = v.astype(BF16)
    rest = v - hi.astype(F32)
    mid = rest.astype(BF16)
    lo = (rest - mid.astype(F32)).astype(BF16)
    return hi, mid, lo


def _adaln_kernel(c_ref, w_ref, b_ref, o_ref):
    ca = _silu(c_ref[...])
    o_ref[...] = jnp.dot(ca, w_ref[...], precision=HIGHEST, preferred_element_type=F32) + b_ref[...]


def _adaln(c, ada_w, ada_b):
    nl, d, n = ada_w.shape
    b = c.shape[0]
    tn = 1024
    return pl.pallas_call(
        _adaln_kernel,
        grid=(nl, n // tn),
        in_specs=[pl.BlockSpec((b, d), lambda l, j: (0, 0)),
                  pl.BlockSpec((None, d, tn), lambda l, j: (l, 0, j)),
                  pl.BlockSpec((None, 1, tn), lambda l, j: (l, 0, j))],
        out_specs=pl.BlockSpec((None, b, tn), lambda l, j: (l, 0, j)),
        out_shape=jax.ShapeDtypeStruct((nl, b, n), F32),
        compiler_params=_params(("parallel", "parallel")),
        name="adaln",
    )(c, ada_w, ada_b.reshape(nl, 1, n))


def _modulated_norm(x, g, sc, sh):
    y = x * lax.rsqrt(jnp.mean(x * x, axis=-1, keepdims=True) + NORM_EPS) * g
    return y * (1.0 + sc) + sh


def _prenorm_kernel(x_ref, g_ref, sc_ref, sh_ref, h_ref):
    h_ref[...] = _modulated_norm(x_ref[...], g_ref[...], sc_ref[...], sh_ref[...]).astype(h_ref.dtype)


def _prenorm(xf, g, mod_l, sc_idx, sh_idx, seq, tm):
    t, d = xf.shape
    tpb = seq // tm
    return pl.pallas_call(
        _prenorm_kernel,
        grid=(t // tm,),
        in_specs=[pl.BlockSpec((tm, d), lambda i: (i, 0)),
                  pl.BlockSpec((1, d), lambda i: (0, 0)),
                  pl.BlockSpec((None, 1, d), lambda i: (i // tpb, 0, sc_idx)),
                  pl.BlockSpec((None, 1, d), lambda i: (i // tpb, 0, sh_idx))],
        out_specs=pl.BlockSpec((tm, d), lambda i: (i, 0)),
        out_shape=jax.ShapeDtypeStruct((t, d), BF16),
        compiler_params=_params(("parallel",)),
        name="prenorm",
    )(xf, g.reshape(1, d), mod_l, mod_l)


def _inproj_kernel(h_ref, w_ref, o_ref):
    o_ref[...] = jnp.dot(h_ref[...], w_ref[...], preferred_element_type=F32).astype(o_ref.dtype)


def _inproj(h, w_all, layer, out_dtype, tm, tn):
    t, d = h.shape
    n = w_all.shape[2]
    return pl.pallas_call(
        _inproj_kernel,
        grid=(t // tm, n // tn),
        in_specs=[pl.BlockSpec((tm, d), lambda i, j: (i, 0)),
                  pl.BlockSpec((None, d, tn), lambda i, j: (layer, 0, j))],
        out_specs=pl.BlockSpec((tm, tn), lambda i, j: (i, j)),
        out_shape=jax.ShapeDtypeStruct((t, n), out_dtype),
        compiler_params=_params(("parallel", "arbitrary")),
        name="inproj",
    )(h, w_all)


def _t5_bucket(rel):
    half = REL_BUCKETS // 2
    exact = half // 2
    ret = jnp.where(rel > 0, half, 0)
    n = jnp.abs(rel)
    nf = jnp.maximum(n, 1).astype(jnp.float32)
    large = exact + (jnp.log(nf / exact) / math.log(REL_MAX_DIST / exact) * (half - exact)).astype(jnp.int32)
    large = jnp.minimum(large, half - 1)
    return ret + jnp.where(n < exact, n, large)


def _attn_bias_tiles(rel_bias, seq, qb, kb):
    assert kb >= REL_MAX_DIST and kb % CHUNK == 0 and qb % kb == 0
    nh = rel_bias.shape[1]
    lo = kb + qb - 1
    rel = jnp.arange(-lo, qb, dtype=jnp.int32)
    vec = (rel_bias[_t5_bucket(rel)].astype(F32) * math.log2(math.e)).T
    period = qb + kb
    jj = np.arange(period)
    jj = np.where(jj < qb, jj, jj - period)
    qq = np.arange(qb)[None, :]
    tiles = []
    for off in range(-1, qb // kb):
        row = vec[:, np.clip(lo + off * kb - jj, 0, lo + qb - 1)]
        skew = jnp.tile(row, (1, kb))[:, :kb * (period - 1)].reshape(nh, kb, period - 1)[:, :, :qb]
        kk = off * kb + np.arange(kb)[:, None]
        visible = (kk // CHUNK) <= (qq // CHUNK)
        tiles.append(jnp.where(visible[None], skew, -jnp.inf))
    far = rel_bias[_t5_bucket(jnp.full((1,), -(seq - 1), jnp.int32))][0].astype(F32)
    return jnp.stack(tiles, axis=1), far


def _attn_kernel(far_ref, q1_ref, q2_ref, k1_ref, k2_ref, v_ref, bias_ref, lam_ref, g_ref, o_ref,
                 vt_ref, acc1_ref, acc2_ref, sa1_ref, sa2_ref, sb1_ref, sb2_ref, *, qb, kb, dv, lam_init):
    h = pl.program_id(1)
    i = pl.program_id(2)
    far = far_ref[h] * math.log2(math.e)
    nt_dims = (((1,), (1,)), ((), ()))
    per_tile = qb // kb
    pad_rows = vt_ref.shape[1] - dv
    s_a = (sa1_ref, sa2_ref)
    s_b = (sb1_ref, sb2_ref)

    @pl.when(i == 0)
    def _():
        rr = lax.broadcasted_iota(jnp.int32, (dv, dv), 0)
        cc = lax.broadcasted_iota(jnp.int32, (dv, dv), 1)
        eye = jnp.where(rr == cc, 1.0, 0.0).astype(BF16)
        ones_row = jnp.where(lax.broadcasted_iota(jnp.int32, (pad_rows, kb), 0) == 0, 1.0, 0.0).astype(BF16)

        def tr_body(j, carry):
            vj = v_ref[pl.ds(pl.multiple_of(j * kb, kb), kb), :]
            vt_ref[j, pl.ds(0, dv), :] = vj.T
            vt_ref[j, pl.ds(dv, pad_rows), :] = ones_row
            return carry

        lax.fori_loop(0, vt_ref.shape[0], tr_body, 0)

    acc1_ref[...] = jnp.zeros(acc1_ref.shape, F32)
    acc2_ref[...] = jnp.zeros(acc2_ref.shape, F32)

    def scores(j):
        ks = pl.ds(pl.multiple_of(j * kb, kb), kb)
        return (lax.dot_general(k1_ref[ks, :], q1_ref[...], nt_dims, preferred_element_type=F32),
                lax.dot_general(k2_ref[ks, :], q2_ref[...], nt_dims, preferred_element_type=F32))

    def softmax_step(s, m, bias, shift):
        if bias is not None:
            s = s + bias
        m_new = jnp.maximum(m, jnp.max(s, axis=0, keepdims=True) + shift)
        return m_new, jnp.exp2(m - m_new), jnp.exp2(s - (m_new - shift)).astype(BF16)

    def step(carry, j, src, dst, bias, shift):
        m1, m2 = carry
        if dst is not None:
            dst[0][...], dst[1][...] = scores(j + 1)
        m1, alpha1, p1 = softmax_step(src[0][...], m1, bias, shift)
        pv1 = jnp.dot(vt_ref[j], p1, preferred_element_type=F32)
        m2, alpha2, p2 = softmax_step(src[1][...], m2, bias, shift)
        pv2 = jnp.dot(vt_ref[j], p2, preferred_element_type=F32)
        acc1_ref[...] = acc1_ref[...] * alpha1 + pv1
        acc2_ref[...] = acc2_ref[...] * alpha2 + pv2
        return m1, m2

    def pair(carry, j, kinds, final=False):
        (bias_a, shift_a), (bias_b, shift_b) = kinds
        carry = step(carry, j, s_a, s_b, bias_a, shift_a)
        return step(carry, j + 1, s_b, None if final else s_a, bias_b, shift_b)

    assert per_tile == 2
    far_kind = (None, far)
    neg = jnp.full((1, qb), -jnp.inf, F32)
    s_a[0][...], s_a[1][...] = scores(0)
    n_ff = jnp.maximum(i - 1, 0)
    carry = lax.fori_loop(0, n_ff, lambda t, c: pair(c, 2 * t, (far_kind, far_kind)), (neg, neg))
    carry = lax.fori_loop(n_ff, i, lambda t, c: pair(c, 2 * t, (far_kind, (bias_ref[0], 0.0))), carry)
    pair(carry, 2 * i, ((bias_ref[1], 0.0), (bias_ref[2], 0.0)), final=True)

    lamv = lam_ref[...]
    lam = (jnp.exp(jnp.sum(lamv[0:1] * lamv[1:2], axis=-1, keepdims=True))
           - jnp.exp(jnp.sum(lamv[2:3] * lamv[3:4], axis=-1, keepdims=True)) + lam_init)
    a1 = acc1_ref[...]
    a2 = acc2_ref[...]
    o = (a1[:dv] / a1[dv:dv + 1] - lam * (a2[:dv] / a2[dv:dv + 1])).T
    y = o * lax.rsqrt(jnp.mean(o * o, axis=-1, keepdims=True) + SUBLN_EPS) * g_ref[...]
    o_ref[...] = (y * (1.0 - lam_init)).astype(o_ref.dtype)


def _attention(proj, bias_tiles, far, lamv, subln_g, batch, seq, lam_init):
    qb, kb = ATT_Q_BLOCK, ATT_KV_BLOCK
    nq = seq // qb
    nh = N_ATT_HEADS
    dk, dv = ATT_HEAD_DIM, ATT_V_DIM
    t = batch * seq
    k_off = 2 * nh
    v_off = 4 * nh * dk // dv
    kern = functools.partial(_attn_kernel, qb=qb, kb=kb, dv=dv, lam_init=lam_init)
    return pl.pallas_call(
        kern,
        grid=(batch, nh, nq),
        in_specs=[pl.BlockSpec(memory_space=pltpu.SMEM),
                  pl.BlockSpec((qb, dk), lambda b, h, i: (b * nq + i, h)),
                  pl.BlockSpec((qb, dk), lambda b, h, i: (b * nq + i, nh + h)),
                  pl.BlockSpec((seq, dk), lambda b, h, i: (b, k_off + h)),
                  pl.BlockSpec((seq, dk), lambda b, h, i: (b, k_off + nh + h)),
                  pl.BlockSpec((seq, dv), lambda b, h, i: (b, v_off + h)),
                  pl.BlockSpec((None, qb // kb + 1, kb, qb), lambda b, h, i: (h, 0, 0, 0)),
                  pl.BlockSpec((4, dk), lambda b, h, i: (0, 0)),
                  pl.BlockSpec((1, dv), lambda b, h, i: (0, 0))],
        out_specs=pl.BlockSpec((qb, dv), lambda b, h, i: (b * nq + i, h)),
        out_shape=jax.ShapeDtypeStruct((t, nh * dv), BF16),
        scratch_shapes=[pltpu.VMEM((seq // kb, dv + BF16_ROWS, kb), BF16),
                        pltpu.VMEM((dv + BF16_ROWS, qb), F32), pltpu.VMEM((dv + BF16_ROWS, qb), F32)]
        + [pltpu.VMEM((kb, qb), F32)] * 4,
        compiler_params=_params(("parallel", "parallel", "arbitrary")),
        name="diff_attention",
    )(far, proj, proj, proj, proj, proj, bias_tiles, lamv, subln_g.reshape(1, dv))


def _conv_silu(x_ref, hist_ref, w_ref, b_ref, blk):
    x = x_ref[...].astype(F32)
    hist_ref[pl.ds(SUBLANES, blk), :] = x
    w = w_ref[...]
    acc = x * w[SSD_CONV - 1:SSD_CONV] + b_ref[...]
    for j in range(1, SSD_CONV):
        acc = acc + hist_ref[pl.ds(SUBLANES - j, blk), :] * w[SSD_CONV - 1 - j:SSD_CONV - j]
    hist_ref[pl.ds(0, SUBLANES), :] = x[blk - SUBLANES:]
    return _silu(acc)


def _ssd_kernel(xs_ref, bm_ref, cm_ref, z_ref, dt_ref, e_ref, cwx_ref, cwb_ref, cwc_ref, cbx_ref, cbb_ref,
                cbc_ref, dtb_ref, alog_ref, dskip_ref, ng_ref, o_ref,
                state_ref, tx_ref, tb_ref, tc_ref, act_ref, *, blk):
    g = pl.program_id(1)

    @pl.when(pl.program_id(2) == 0)
    def _():
        state_ref[...] = jnp.zeros(state_ref.shape, F32)
        for hist_ref in (tx_ref, tb_ref, tc_ref):
            hist_ref[pl.ds(0, SUBLANES), :] = jnp.zeros((SUBLANES, hist_ref.shape[1]), F32)

    xs = _conv_silu(xs_ref, tx_ref, cwx_ref, cbx_ref, blk)
    bm = _conv_silu(bm_ref, tb_ref, cwb_ref, cbb_ref, blk)
    cm = _conv_silu(cm_ref, tc_ref, cwc_ref, cbc_ref, blk)

    dt = jax.nn.softplus(dt_ref[...] + dtb_ref[...])
    da = dt * (-jnp.exp(alog_ref[...]))
    row = lax.broadcasted_iota(jnp.int32, (blk, blk), 0)
    col = lax.broadcasted_iota(jnp.int32, (blk, blk), 1)
    causal = col <= row
    tri = jnp.where(causal, 1.0, 0.0).astype(BF16)
    acum = sum(jnp.dot(tri, part, preferred_element_type=F32) for part in _split3(da))
    both_e = jnp.dot(jnp.concatenate([jnp.concatenate(_split3(dt), axis=1),
                                      jnp.concatenate(_split3(acum), axis=1)], axis=0),
                     e_ref[...], preferred_element_type=F32)
    dt_e = both_e[:blk]
    acum_e = both_e[blk:]
    act_ref[...] = acum.T

    xdt = xs * dt_e
    xdt_b = xdt.astype(BF16)
    cm_b = cm.astype(BF16)
    cb = lax.dot_general(cm_b, bm.astype(BF16), (((1,), (1,)), ((), ())), preferred_element_type=F32)
    lane = lax.broadcasted_iota(jnp.int32, (blk, LANES), 1)
    halves = (lane < SSD_HEAD_DIM, lane >= SSD_HEAD_DIM)
    pieces = []
    for pair in range(GROUP_W // LANES):
        xp = xdt_b[:, pair * LANES:(pair + 1) * LANES]
        yp = jnp.zeros((blk, LANES), F32)
        for hh in range(2):
            r = 2 * pair + hh
            a_col = acum_e[:, r * SSD_HEAD_DIM:r * SSD_HEAD_DIM + 1]
            a_row = act_ref[pl.ds(g * SSD_HEADS_PER_GROUP + r, 1), :]
            decay = jnp.exp(jnp.where(causal, a_col - a_row, -jnp.inf))
            mat = (cb * decay).astype(BF16)
            yp = yp + jnp.dot(mat, jnp.where(halves[hh], xp, jnp.zeros_like(xp)), preferred_element_type=F32)
        pieces.append(yp)
    y = jnp.concatenate(pieces, axis=1)

    st = state_ref[...]
    y = y + jnp.exp(acum_e) * jnp.dot(cm_b, st.astype(BF16), preferred_element_type=F32)
    a_last = acum_e[blk - 1:blk, :]
    wgt = (xdt * jnp.exp(a_last - acum_e)).astype(BF16)
    state_ref[...] = st * jnp.exp(a_last) + jnp.dot(bm.T.astype(BF16), wgt, preferred_element_type=F32)

    y = y + dskip_ref[...] * xs
    y = y * _silu(z_ref[...].astype(F32))
    y = y * lax.rsqrt(jnp.mean(y * y, axis=-1, keepdims=True) + SUBLN_EPS)
    o_ref[...] = (y * ng_ref[...]).astype(o_ref.dtype)


def _ssd(proj, dt_raw, expand, conv_w, conv_b, dt_bias, a_log, d_skip, norm_g, batch, seq, z_col, xbc_col):
    blk = SSD_BLOCK
    nc = seq // blk
    ng = SSD_GROUPS
    t = batch * seq
    width = ng * GROUP_W
    heads = ng * SSD_HEADS_PER_GROUP
    z_blk = z_col // GROUP_W
    xs_blk = xbc_col // GROUP_W
    b_blk = (xbc_col + width) // SSD_STATE
    c_blk = b_blk + ng
    cw_b_blk = width // SSD_STATE
    pad = LANES - heads
    row = lambda b, g, c: b * nc + c
    dtb = jnp.pad(dt_bias, (0, pad)).reshape(1, LANES)
    alog = jnp.pad(a_log, (0, pad)).reshape(1, LANES)
    dskip = jnp.repeat(d_skip, SSD_HEAD_DIM).reshape(1, width)
    cb2 = conv_b.reshape(1, -1)
    kern = functools.partial(_ssd_kernel, blk=blk)
    return pl.pallas_call(
        kern,
        grid=(batch, ng, nc),
        in_specs=[pl.BlockSpec((blk, GROUP_W), lambda b, g, c: (row(b, g, c), xs_blk + g)),
                  pl.BlockSpec((blk, SSD_STATE), lambda b, g, c: (row(b, g, c), b_blk + g)),
                  pl.BlockSpec((blk, SSD_STATE), lambda b, g, c: (row(b, g, c), c_blk + g)),
                  pl.BlockSpec((blk, GROUP_W), lambda b, g, c: (row(b, g, c), z_blk + g)),
                  pl.BlockSpec((blk, LANES), lambda b, g, c: (row(b, g, c), 0)),
                  pl.BlockSpec((None, 3 * LANES, GROUP_W), lambda b, g, c: (g, 0, 0)),
                  pl.BlockSpec((SSD_CONV, GROUP_W), lambda b, g, c: (0, g)),
                  pl.BlockSpec((SSD_CONV, SSD_STATE), lambda b, g, c: (0, cw_b_blk + g)),
                  pl.BlockSpec((SSD_CONV, SSD_STATE), lambda b, g, c: (0, cw_b_blk + ng + g)),
                  pl.BlockSpec((1, GROUP_W), lambda b, g, c: (0, g)),
                  pl.BlockSpec((1, SSD_STATE), lambda b, g, c: (0, cw_b_blk + g)),
                  pl.BlockSpec((1, SSD_STATE), lambda b, g, c: (0, cw_b_blk + ng + g)),
                  pl.BlockSpec((1, LANES), lambda b, g, c: (0, 0)),
                  pl.BlockSpec((1, LANES), lambda b, g, c: (0, 0)),
                  pl.BlockSpec((1, GROUP_W), lambda b, g, c: (0, g)),
                  pl.BlockSpec((1, GROUP_W), lambda b, g, c: (0, g))],
        out_specs=pl.BlockSpec((blk, GROUP_W), lambda b, g, c: (row(b, g, c), g)),
        out_shape=jax.ShapeDtypeStruct((t, width), BF16),
        scratch_shapes=[pltpu.VMEM((SSD_STATE, GROUP_W), F32),
                        pltpu.VMEM((SUBLANES + blk, GROUP_W), F32),
                        pltpu.VMEM((SUBLANES + blk, SSD_STATE), F32),
                        pltpu.VMEM((SUBLANES + blk, SSD_STATE), F32),
                        pltpu.VMEM((LANES, blk), F32)],
        compiler_params=_params(("parallel", "parallel", "arbitrary")),
        name="ssd",
    )(proj, proj, proj, proj, dt_raw, expand, conv_w, conv_w, conv_w, cb2, cb2, cb2, dtb, alog, dskip,
      norm_g.reshape(1, width))


def _head_expand():
    e = np.zeros((SSD_GROUPS, LANES, GROUP_W), np.float32)
    for g in range(SSD_GROUPS):
        for r in range(SSD_HEADS_PER_GROUP):
            e[g, g * SSD_HEADS_PER_GROUP + r, r * SSD_HEAD_DIM:(r + 1) * SSD_HEAD_DIM] = 1.0
    return jnp.asarray(np.tile(e, (1, 3, 1)), dtype=BF16)


def _merge_kernel(oa_ref, os_ref, wa_ref, ws1_ref, ws2_ref, ga_ref, gs_ref, o_ref):
    ka = wa_ref.shape[0]
    ya = jnp.dot(oa_ref[...], wa_ref[...], preferred_element_type=F32)
    ys = (jnp.dot(os_ref[:, pl.ds(0, ka)], ws1_ref[...], preferred_element_type=F32)
          + jnp.dot(os_ref[:, pl.ds(ka, ka)], ws2_ref[...], preferred_element_type=F32))
    merged = jax.nn.sigmoid(ga_ref[...].astype(F32)) * ya + jax.nn.sigmoid(gs_ref[...].astype(F32)) * ys
    o_ref[...] = merged.astype(o_ref.dtype)


def _merge(o_att, o_ssd, gates, w_branch_all, layer, tm, tn):
    t, ka = o_att.shape
    ks = o_ssd.shape[1]
    d = w_branch_all.shape[2]
    assert ks == 2 * ka
    w_spec = lambda blk: pl.BlockSpec((None, ka, tn), lambda i, j: (layer, blk, j))
    gs_blk = d // tn
    return pl.pallas_call(
        _merge_kernel,
        grid=(t // tm, d // tn),
        in_specs=[pl.BlockSpec((tm, ka), lambda i, j: (i, 0)),
                  pl.BlockSpec((tm, ks), lambda i, j: (i, 0)),
                  w_spec(0), w_spec(1), w_spec(2),
                  pl.BlockSpec((tm, tn), lambda i, j: (i, j)),
                  pl.BlockSpec((tm, tn), lambda i, j: (i, gs_blk + j))],
        out_specs=pl.BlockSpec((tm, tn), lambda i, j: (i, j)),
        out_shape=jax.ShapeDtypeStruct((t, d), BF16),
        compiler_params=_params(("parallel", "arbitrary")),
        name="branch_merge",
    )(o_att, o_ssd, w_branch_all, w_branch_all, w_branch_all, gates, gates)


def _pack_bf16_pair(v):
    n = v.shape[1] // 2
    lo = lax.bitcast_convert_type(v[:, :n].astype(BF16).astype(F32), jnp.uint32)
    hi = lax.bitcast_convert_type(v[:, n:].astype(BF16).astype(F32), jnp.uint32)
    return (lo >> 16) | (hi & jnp.uint32(0xFFFF0000))


def _unpack_bf16_pair(p):
    lo = lax.bitcast_convert_type(p << 16, F32)
    hi = lax.bitcast_convert_type(p & jnp.uint32(0xFFFF0000), F32)
    return lo, hi


def _wo_kernel(m_ref, x_ref, w_ref, g1_ref, ng_ref, sc_ref, sh_ref, rw_ref, rb_ref, xo_ref, hp_ref, lg_ref):
    y = jnp.dot(m_ref[...], w_ref[...], preferred_element_type=F32)
    x = x_ref[...] + g1_ref[...] * y
    xo_ref[...] = x
    h2 = _modulated_norm(x, ng_ref[...], sc_ref[...], sh_ref[...])
    hp_ref[...] = _pack_bf16_pair(h2)
    h_hi = h2.astype(BF16)
    h_lo = (h2 - h_hi.astype(F32)).astype(BF16)
    both = jnp.dot(h_hi, rw_ref[...], preferred_element_type=F32)
    cross = jnp.dot(h_lo, rw_ref[:, pl.ds(0, LANES)], preferred_element_type=F32)
    lg_ref[...] = both[:, :LANES] + (both[:, LANES:] + cross) + rb_ref[...]


def _wo_residual(merged, xf, w_o_all, layer, mod_l, norm2_g, router_w, router_b, seq, tm):
    t, d = xf.shape
    ne = router_w.shape[1]
    tpb = seq // tm
    mod_spec = lambda idx: pl.BlockSpec((None, 1, d), lambda i: (i // tpb, 0, idx))
    rw = jnp.pad(router_w, ((0, 0), (0, LANES - ne)))
    rw_hi = rw.astype(BF16)
    rw = jnp.concatenate([rw_hi, (rw - rw_hi.astype(F32)).astype(BF16)], axis=1)
    rb = jnp.pad(router_b, (0, LANES - ne), constant_values=-jnp.inf).reshape(1, LANES)
    return pl.pallas_call(
        _wo_kernel,
        grid=(t // tm,),
        in_specs=[pl.BlockSpec((tm, d), lambda i: (i, 0)),
                  pl.BlockSpec((tm, d), lambda i: (i, 0)),
                  pl.BlockSpec((None, d, d), lambda i: (layer, 0, 0)),
                  mod_spec(2),
                  pl.BlockSpec((1, d), lambda i: (0, 0)),
                  mod_spec(4), mod_spec(3),
                  pl.BlockSpec((d, 2 * LANES), lambda i: (0, 0)),
                  pl.BlockSpec((1, LANES), lambda i: (0, 0))],
        out_specs=[pl.BlockSpec((tm, d), lambda i: (i, 0)), pl.BlockSpec((tm, d // 2), lambda i: (i, 0)),
                   pl.BlockSpec((tm, LANES), lambda i: (i, 0))],
        out_shape=[jax.ShapeDtypeStruct((t, d), F32), jax.ShapeDtypeStruct((t, d // 2), jnp.uint32),
                   jax.ShapeDtypeStruct((t, LANES), F32)],
        compiler_params=_params(("parallel",)),
        name="wo_residual",
    )(merged, xf, w_o_all, mod_l, norm2_g.reshape(1, d), mod_l, mod_l, rw, rb)


def _router_kernel(lg_ref, idx_ref, gate_ref, rank_ref, cnt_ref, run_ref, *, tm):
    @pl.when(pl.program_id(0) == 0)
    def _():
        run_ref[...] = jnp.zeros(run_ref.shape, F32)

    logits = lg_ref[...]
    lane = lax.broadcasted_iota(jnp.int32, (tm, LANES), 1)
    vals = logits
    picked = jnp.zeros((tm, LANES), F32)
    top_v, top_sel, top_i = [], [], []
    for _ in range(TOP_K):
        m = jnp.max(vals, axis=-1, keepdims=True)
        idx = jnp.min(jnp.where(vals == m, lane, LANES), axis=-1, keepdims=True)
        sel = lane == idx
        top_v.append(m)
        top_i.append(idx)
        top_sel.append(sel)
        vals = jnp.where(sel, -jnp.inf, vals)
        picked = picked + sel.astype(F32)

    row = lax.broadcasted_iota(jnp.int32, (tm, tm), 0)
    col = lax.broadcasted_iota(jnp.int32, (tm, tm), 1)
    before = jnp.dot((col < row).astype(BF16), picked.astype(BF16), preferred_element_type=F32) + run_ref[...]
    run_ref[...] = run_ref[...] + jnp.sum(picked, axis=0, keepdims=True)
    cnt_ref[...] = run_ref[...]

    exps = [jnp.exp(v - top_v[0]) for v in top_v]
    denom = exps[0] + exps[1] + exps[2] + exps[3]
    idx_out = jnp.zeros((tm, LANES), jnp.int32)
    rank_out = jnp.zeros((tm, LANES), jnp.int32)
    gate_out = jnp.zeros((tm, LANES), F32)
    for k in range(TOP_K):
        rank_k = jnp.sum(jnp.where(top_sel[k], before, 0.0), axis=-1, keepdims=True).astype(jnp.int32)
        idx_out = jnp.where(lane == k, top_i[k], idx_out)
        rank_out = jnp.where(lane == k, rank_k, rank_out)
        gate_out = jnp.where(lane == k, exps[k] / denom, gate_out)
    idx_ref[...] = idx_out
    rank_ref[...] = rank_out
    gate_ref[...] = gate_out


def _router(logits, tm):
    t = logits.shape[0]
    kern = functools.partial(_router_kernel, tm=tm)
    tok_spec = pl.BlockSpec((tm, LANES), lambda i: (i, 0))
    return pl.pallas_call(
        kern,
        grid=(t // tm,),
        in_specs=[tok_spec],
        out_specs=[tok_spec, tok_spec, tok_spec, pl.BlockSpec((1, LANES), lambda i: (0, 0))],
        out_shape=[jax.ShapeDtypeStruct((t, LANES), jnp.int32), jax.ShapeDtypeStruct((t, LANES), F32),
                   jax.ShapeDtypeStruct((t, LANES), jnp.int32), jax.ShapeDtypeStruct((1, LANES), F32)],
        scratch_shapes=[pltpu.VMEM((1, LANES), F32)],
        compiler_params=_params(("arbitrary",)),
        name="router",
    )(logits)


def _dispatch_kernel(dest_ref, h_ref, xs_in_hbm, xs_hbm, sem, *, tm):
    del xs_in_hbm

    def row_copy(t, k):
        return pltpu.make_async_copy(h_ref.at[pl.ds(t, 1)],
                                     xs_hbm.at[pl.ds(dest_ref[t * TOP_K + k], 1)], sem)

    for t in range(tm):
        for k in range(TOP_K):
            row_copy(t, k).start()
    for _ in range(TOP_K):
        pltpu.make_async_copy(h_ref, xs_hbm.at[pl.ds(0, tm)], sem).wait()


def _dispatch(h2, dest_flat, n_rows, tm, fill):
    t, d = h2.shape
    assert fill.shape == (n_rows, d) and fill.dtype == h2.dtype
    kern = functools.partial(_dispatch_kernel, tm=tm)
    return pl.pallas_call(
        kern,
        grid=(t // tm,),
        in_specs=[pl.BlockSpec((tm * TOP_K,), lambda i: (i,), memory_space=pltpu.SMEM),
                  pl.BlockSpec((tm, d), lambda i: (i, 0)),
                  pl.BlockSpec(memory_space=pl.ANY)],
        out_specs=pl.BlockSpec(memory_space=pl.ANY),
        out_shape=jax.ShapeDtypeStruct((n_rows, d), h2.dtype),
        scratch_shapes=[pltpu.SemaphoreType.DMA(())],
        input_output_aliases={2: 0},
        compiler_params=_params(("arbitrary",)),
        name="moe_dispatch",
    )(dest_flat, h2, fill)


def _expert_gu_kernel(be_ref, nv_ref, x_ref, w_ref, b_ref, o_ref, *, ff):
    del be_ref
    valid = pl.program_id(0) < nv_ref[0]

    @pl.when(valid)
    def _():
        x_lo, x_hi = _unpack_bf16_pair(x_ref[...])
        half = x_lo.shape[1]
        gu = (jnp.dot(x_lo.astype(BF16), w_ref[pl.ds(0, half), :], preferred_element_type=F32)
              + jnp.dot(x_hi.astype(BF16), w_ref[pl.ds(half, half), :], preferred_element_type=F32) + b_ref[...])
        g = jnp.minimum(gu[:, :ff], SWIGLU_LIMIT)
        u = jnp.clip(gu[:, ff:], -SWIGLU_LIMIT, SWIGLU_LIMIT)
        o_ref[...] = ((u + 1.0) * (g * jax.nn.sigmoid(SWIGLU_ALPHA * g))).astype(o_ref.dtype)

    @pl.when(jnp.logical_not(valid))
    def _():
        o_ref[...] = jnp.zeros(o_ref.shape, o_ref.dtype)


def _expert_down_kernel(be_ref, nv_ref, a_ref, w_ref, b_ref, o_ref, wb_ref):
    i = pl.program_id(0)
    valid = i < nv_ref[0]
    new_expert = jnp.logical_or(i == 0, be_ref[i] != be_ref[jnp.maximum(i - 1, 0)])

    @pl.when(jnp.logical_and(valid, new_expert))
    def _():
        wb_ref[...] = w_ref[...].astype(BF16)

    @pl.when(valid)
    def _():
        o_ref[...] = _pack_bf16_pair(jnp.dot(a_ref[...], wb_ref[...], preferred_element_type=F32) + b_ref[...])

    @pl.when(jnp.logical_not(valid))
    def _():
        o_ref[...] = jnp.zeros(o_ref.shape, o_ref.dtype)


def _expert_ffn(xs, block_e, n_valid, w_gu, b_gu, w_dn, b_dn, layer):
    n_rows, dp = xs.shape
    bm = EXPERT_BLOCK
    n_blocks = n_rows // bm
    nl, ne, d, ff2 = w_gu.shape
    ff = ff2 // 2
    act = pl.pallas_call(
        functools.partial(_expert_gu_kernel, ff=ff),
        grid_spec=pltpu.PrefetchScalarGridSpec(
            num_scalar_prefetch=2, grid=(n_blocks,),
            in_specs=[pl.BlockSpec((bm, dp), lambda i, be, nv: (i, 0)),
                      pl.BlockSpec((None, None, d, ff2), lambda i, be, nv: (layer, be[i], 0, 0)),
                      pl.BlockSpec((None, None, 1, ff2), lambda i, be, nv: (layer, be[i], 0, 0))],
            out_specs=pl.BlockSpec((bm, ff), lambda i, be, nv: (i, 0))),
        out_shape=jax.ShapeDtypeStruct((n_rows, ff), BF16),
        compiler_params=_params(("arbitrary",)),
        name="expert_gate_up",
    )(block_e, n_valid, xs, w_gu, b_gu.reshape(nl, ne, 1, ff2))
    return pl.pallas_call(
        _expert_down_kernel,
        grid_spec=pltpu.PrefetchScalarGridSpec(
            num_scalar_prefetch=2, grid=(n_blocks,),
            in_specs=[pl.BlockSpec((bm, ff), lambda i, be, nv: (i, 0)),
                      pl.BlockSpec((None, None, ff, d), lambda i, be, nv: (layer, be[i], 0, 0)),
                      pl.BlockSpec((None, None, 1, d), lambda i, be, nv: (layer, be[i], 0, 0))],
            out_specs=pl.BlockSpec((bm, dp), lambda i, be, nv: (i, 0)),
            scratch_shapes=[pltpu.VMEM((ff, d), BF16)]),
        out_shape=jax.ShapeDtypeStruct((n_rows, dp), jnp.uint32),
        compiler_params=_params(("arbitrary",)),
        name="expert_down",
    )(block_e, n_valid, act, w_dn, b_dn.reshape(nl, ne, 1, d))


def _combine_kernel(dest_ref, dest_next_ref, ys_hbm, x_ref, gate_ref, g2_ref, ng_ref, *rest, half, final):
    if final:
        o_ref, buf_a, buf_b, sem_a, sem_b = rest
    else:
        sc_ref, sh_ref, o_ref, h_ref, buf_a, buf_b, sem_a, sem_b = rest
    i = pl.program_id(0)

    def start_rows(dref, base, buf, sem):
        for t in range(half):
            for k in range(TOP_K):
                pltpu.make_async_copy(ys_hbm.at[pl.ds(dref[base + t * TOP_K + k], 1)],
                                      buf.at[k, pl.ds(t, 1)], sem).start()

    def wait_rows(buf, sem):
        for k in range(TOP_K):
            pltpu.make_async_copy(ys_hbm.at[pl.ds(0, half)], buf.at[k], sem).wait()

    def finish(buf, rows):
        gates = gate_ref[rows, :]
        moe_lo, moe_hi = (gates[:, 0:1] * part for part in _unpack_bf16_pair(buf[0]))
        for k in range(1, TOP_K):
            lo, hi = _unpack_bf16_pair(buf[k])
            moe_lo = moe_lo + gates[:, k:k + 1] * lo
            moe_hi = moe_hi + gates[:, k:k + 1] * hi
        x = x_ref[rows, :] + g2_ref[...] * jnp.concatenate([moe_lo, moe_hi], axis=1)
        if final:
            o_ref[rows, :] = x * lax.rsqrt(jnp.mean(x * x, axis=-1, keepdims=True) + NORM_EPS) * ng_ref[...]
        else:
            o_ref[rows, :] = x
            h_ref[rows, :] = _modulated_norm(x, ng_ref[...], sc_ref[...], sh_ref[...]).astype(h_ref.dtype)

    @pl.when(i == 0)
    def _():
        start_rows(dest_ref, 0, buf_a, sem_a)

    wait_rows(buf_a, sem_a)
    start_rows(dest_ref, half * TOP_K, buf_b, sem_b)
    finish(buf_a, pl.ds(0, half))
    wait_rows(buf_b, sem_b)
    start_rows(dest_next_ref, 0, buf_a, sem_a)
    finish(buf_b, pl.ds(half, half))

    @pl.when(i == pl.num_programs(0) - 1)
    def _():
        wait_rows(buf_a, sem_a)


def _combine(ys, dest_flat, xf, gates, mod_l, norm_g, mod_next, seq, half):
    t, d = xf.shape
    final = mod_next is None
    tm = 2 * half
    steps = t // tm
    tpb = seq // tm
    kern = functools.partial(_combine_kernel, half=half, final=final)
    tile = pl.BlockSpec((tm, d), lambda i: (i, 0))
    mod_spec = lambda idx: pl.BlockSpec((None, 1, d), lambda i: (i // tpb, 0, idx))
    in_specs = [pl.BlockSpec((tm * TOP_K,), lambda i: (i,), memory_space=pltpu.SMEM),
                pl.BlockSpec((tm * TOP_K,), lambda i: (jnp.minimum(i + 1, steps - 1),), memory_space=pltpu.SMEM),
                pl.BlockSpec(memory_space=pl.ANY),
                tile,
                pl.BlockSpec((tm, LANES), lambda i: (i, 0)),
                mod_spec(5),
                pl.BlockSpec((1, d), lambda i: (0, 0))]
    args = [dest_flat, dest_flat, ys, xf, gates, mod_l, norm_g.reshape(1, d)]
    if final:
        out_specs, out_shape = tile, jax.ShapeDtypeStruct((t, d), F32)
    else:
        in_specs += [mod_spec(1), mod_spec(0)]
        args += [mod_next, mod_next]
        out_specs = [tile, tile]
        out_shape = [jax.ShapeDtypeStruct((t, d), F32), jax.ShapeDtypeStruct((t, d), BF16)]
    return pl.pallas_call(
        kern,
        grid=(steps,),
        in_specs=in_specs,
        out_specs=out_specs,
        out_shape=out_shape,
        scratch_shapes=[pltpu.VMEM((TOP_K, half, d // 2), jnp.uint32), pltpu.VMEM((TOP_K, half, d // 2), jnp.uint32),
                        pltpu.SemaphoreType.DMA(()), pltpu.SemaphoreType.DMA(())],
        compiler_params=_params(("arbitrary",)),
        name="moe_combine",
    )(*args)


def _moe(h2p, logits, xf, mod_l, w_gu, b_gu, w_dn, b_dn, layer, fill, norm_g, mod_next, seq):
    t = h2p.shape[0]
    ne = w_gu.shape[1]
    bm = EXPERT_BLOCK
    top_i, gates, rank, counts = _router(logits, tm=256)
    cnt = counts[0, :ne].astype(jnp.int32)
    padded = (cnt + bm - 1) // bm * bm
    pad_end = jnp.cumsum(padded)
    pad_start = pad_end - padded
    experts = jnp.arange(ne, dtype=jnp.int32)
    start_of = jnp.sum(jnp.where(top_i[:, :TOP_K, None] == experts, pad_start, 0), axis=-1)
    dest = start_of + rank[:, :TOP_K]
    n_blocks = t * TOP_K // bm + ne
    blk_row = jnp.arange(n_blocks, dtype=jnp.int32)[:, None] * bm
    block_e = jnp.minimum(jnp.sum((pad_end[None, :] <= blk_row).astype(jnp.int32), axis=1), ne - 1)
    n_valid = (pad_end[-1:] // bm).astype(jnp.int32)
    dest_flat = dest.reshape(-1).astype(jnp.int32)

    n_rows = n_blocks * bm
    if fill is None:
        fill = jnp.zeros((n_rows, h2p.shape[1]), h2p.dtype)
    xs = _dispatch(h2p, dest_flat, n_rows, 256, fill)
    ys = _expert_ffn(xs, block_e, n_valid, w_gu, b_gu, w_dn, b_dn, layer)
    return _combine(ys, dest_flat, xf, gates, mod_l, norm_g, mod_next, seq, half=128), ys


def kernel(x, c, ada_w, ada_b, norm1_g, w_in, rel_bias, lam_q1, lam_k1, lam_q2, lam_k2, attn_subln_g, conv_w, conv_b, dt_bias, a_log, d_skip, ssd_norm_g, w_branch, w_o, norm2_g, router_w, router_b, w_gate_up, b_gate_up, w_down, b_down, final_g):
    batch, seq, d = x.shape
    t = batch * seq
    depth = ada_w.shape[0]
    att_w = N_ATT_HEADS * ATT_V_DIM
    ssd_w = SSD_GROUPS * GROUP_W
    n_heads_ssd = SSD_GROUPS * SSD_HEADS_PER_GROUP
    conv_ch = conv_w.shape[2]
    q_cols = 2 * N_ATT_HEADS * ATT_HEAD_DIM
    q_scale = ATT_HEAD_DIM ** -0.5 * math.log2(math.e)
    z_col = 2 * q_cols + att_w
    xbc_col = z_col + ssd_w
    dt_col = xbc_col + conv_ch
    col_scale = jnp.concatenate([jnp.full((q_cols,), q_scale, F32), jnp.ones((dt_col - q_cols,), F32)])

    mod = _adaln(c, ada_w, ada_b)
    bias_tiles, far = _attn_bias_tiles(rel_bias, seq, ATT_Q_BLOCK, ATT_KV_BLOCK)
    expand = _head_expand()
    xf = x.reshape(t, d)
    tm_big = min(1024, seq)

    w_main_all = (w_in[:, :, :dt_col] * col_scale).astype(BF16)
    w_gate_all = w_in[:, :, dt_col + n_heads_ssd:].astype(BF16)
    w_dt_all = jnp.pad(w_in[:, :, dt_col:dt_col + n_heads_ssd],
                       ((0, 0), (0, 0), (0, LANES - n_heads_ssd))).astype(BF16)
    w_branch_all = w_branch.astype(BF16)
    w_o_all = w_o.astype(BF16)
    w_gate_up_all = w_gate_up.astype(BF16)

    mods = [mod[layer].reshape(batch, 1, 6 * d) for layer in range(depth)]
    h = _prenorm(xf, norm1_g[0], mods[0], 1, 0, seq, tm=tm_big)
    ys_prev = None
    for layer in range(depth):
        mod_l = mods[layer]
        proj = _inproj(h, w_main_all, layer, BF16, tm=tm_big, tn=1024)
        gates = _inproj(h, w_gate_all, layer, BF16, tm=tm_big, tn=1024)
        dt_raw = _inproj(h, w_dt_all, layer, F32, tm=tm_big, tn=LANES)

        lam_init = 0.8 - 0.6 * math.exp(-0.3 * layer)
        lamv = jnp.stack([lam_q1[layer], lam_k1[layer], lam_q2[layer], lam_k2[layer]], axis=0)
        o_att = _attention(proj, bias_tiles, far, lamv, attn_subln_g[layer], batch, seq, lam_init)
        o_ssd = _ssd(proj, dt_raw, expand, conv_w[layer], conv_b[layer], dt_bias[layer], a_log[layer],
                     d_skip[layer], ssd_norm_g[layer], batch, seq, z_col, xbc_col)

        merged = _merge(o_att, o_ssd, gates, w_branch_all, layer, tm=tm_big, tn=512)
        xf, h2p, logits = _wo_residual(merged, xf, w_o_all, layer, mod_l, norm2_g[layer],
                                       router_w[layer], router_b[layer], seq, tm=256)

        moe = functools.partial(_moe, h2p, logits, xf, mod_l, w_gate_up_all, b_gate_up, w_down, b_down, layer, ys_prev)
        if layer == depth - 1:
            xf, ys_prev = moe(final_g, None, seq)
        else:
            (xf, h), ys_prev = moe(norm1_g[layer + 1], mods[layer + 1], seq)
    return xf.reshape(batch, seq, d)
```

```python
import functools
import math

import numpy as np
import jax
import jax.numpy as jnp
from jax import lax
from jax.experimental import pallas as pl
from jax.experimental.pallas import tpu as pltpu

F32 = jnp.float32
BF16 = jnp.bfloat16
HIGHEST = lax.Precision.HIGHEST

DEPTH = 2
CHUNK = 64
N_ATT_HEADS = 8
ATT_HEAD_DIM = 128
ATT_V_DIM = 2 * ATT_HEAD_DIM
REL_BUCKETS = 32
REL_MAX_DIST = 128
SSD_HEAD_DIM = 64
SSD_GROUPS = 8
SSD_HEADS_PER_GROUP = 8
SSD_STATE = 128
SSD_CONV = 4
N_EXPERTS = 32
TOP_K = 4
SWIGLU_LIMIT = 7.0
SWIGLU_ALPHA = 1.702
NORM_EPS = 1e-6
SUBLN_EPS = 1e-5

LANES = 128
SUBLANES = 8
BF16_ROWS = 16
VMEM_LIMIT = 56 * 1024 * 1024

ATT_Q_BLOCK = 512
ATT_KV_BLOCK = 256
SSD_BLOCK = 256
EXPERT_BLOCK = 256
GROUP_W = SSD_HEADS_PER_GROUP * SSD_HEAD_DIM


def _params(semantics):
    return pltpu.CompilerParams(dimension_semantics=semantics, vmem_limit_bytes=VMEM_LIMIT)


def _silu(v):
    half = 0.5 * v
    return half + half * jnp.tanh(half)


def _split3(v):
    hi = v.astype(BF16)
    rest = v - hi.astype(F32)
    mid = rest.astype(BF16)
    lo = (rest - mid.astype(F32)).astype(BF16)
    return hi, mid, lo


def _adaln_kernel(c_ref, w_ref, b_ref, o_ref):
    ca = _silu(c_ref[...])
    o_ref[...] = jnp.dot(ca, w_ref[...], precision=HIGHEST, preferred_element_type=F32) + b_ref[...]


def _adaln(c, ada_w, ada_b):
    nl, d, n = ada_w.shape
    b = c.shape[0]
    tn = 1024
    return pl.pallas_call(
        _adaln_kernel,
        grid=(nl, n // tn),
        in_specs=[pl.BlockSpec((b, d), lambda l, j: (0, 0)),
                  pl.BlockSpec((None, d, tn), lambda l, j: (l, 0, j)),
                  pl.BlockSpec((None, 1, tn), lambda l, j: (l, 0, j))],
        out_specs=pl.BlockSpec((None, b, tn), lambda l, j: (l, 0, j)),
        out_shape=jax.ShapeDtypeStruct((nl, b, n), F32),
        compiler_params=_params(("parallel", "parallel")),
        name="adaln",
    )(c, ada_w, ada_b.reshape(nl, 1, n))


def _modulated_norm(x, g, sc, sh):
    y = x * lax.rsqrt(jnp.mean(x * x, axis=-1, keepdims=True) + NORM_EPS) * g
    return y * (1.0 + sc) + sh


def _prenorm_kernel(x_ref, g_ref, sc_ref, sh_ref, h_ref):
    h_ref[...] = _modulated_norm(x_ref[...], g_ref[...], sc_ref[...], sh_ref[...]).astype(h_ref.dtype)


def _prenorm(xf, g, mod_l, sc_idx, sh_idx, seq, tm):
    t, d = xf.shape
    tpb = seq // tm
    return pl.pallas_call(
        _prenorm_kernel,
        grid=(t // tm,),
        in_specs=[pl.BlockSpec((tm, d), lambda i: (i, 0)),
                  pl.BlockSpec((1, d), lambda i: (0, 0)),
                  pl.BlockSpec((None, 1, d), lambda i: (i // tpb, 0, sc_idx)),
                  pl.BlockSpec((None, 1, d), lambda i: (i // tpb, 0, sh_idx))],
        out_specs=pl.BlockSpec((tm, d), lambda i: (i, 0)),
        out_shape=jax.ShapeDtypeStruct((t, d), BF16),
        compiler_params=_params(("parallel",)),
        name="prenorm",
    )(xf, g.reshape(1, d), mod_l, mod_l)


def _inproj_kernel(h_ref, w_ref, o_ref):
    o_ref[...] = jnp.dot(h_ref[...], w_ref[...], preferred_element_type=F32).astype(o_ref.dtype)


def _inproj(h, w_all, layer, out_dtype, tm, tn):
    t, d = h.shape
    n = w_all.shape[2]
    return pl.pallas_call(
        _inproj_kernel,
        grid=(t // tm, n // tn),
        in_specs=[pl.BlockSpec((tm, d), lambda i, j: (i, 0)),
                  pl.BlockSpec((None, d, tn), lambda i, j: (layer, 0, j))],
        out_specs=pl.BlockSpec((tm, tn), lambda i, j: (i, j)),
        out_shape=jax.ShapeDtypeStruct((t, n), out_dtype),
        compiler_params=_params(("parallel", "arbitrary")),
        name="inproj",
    )(h, w_all)


def _t5_bucket(rel):
    half = REL_BUCKETS // 2
    exact = half // 2
    ret = jnp.where(rel > 0, half, 0)
    n = jnp.abs(rel)
    nf = jnp.maximum(n, 1).astype(jnp.float32)
    large = exact + (jnp.log(nf / exact) / math.log(REL_MAX_DIST / exact) * (half - exact)).astype(jnp.int32)
    large = jnp.minimum(large, half - 1)
    return ret + jnp.where(n < exact, n, large)


def _attn_bias_tiles(rel_bias, seq, qb, kb):
    assert kb >= REL_MAX_DIST and kb % CHUNK == 0 and qb % kb == 0
    nh = rel_bias.shape[1]
    lo = kb + qb - 1
    rel = jnp.arange(-lo, qb, dtype=jnp.int32)
    vec = (rel_bias[_t5_bucket(rel)].astype(F32) * math.log2(math.e)).T
    period = qb + kb
    jj = np.arange(period)
    jj = np.where(jj < qb, jj, jj - period)
    qq = np.arange(qb)[None, :]
    tiles = []
    for off in range(-1, qb // kb):
        row = vec[:, np.clip(lo + off * kb - jj, 0, lo + qb - 1)]
        skew = jnp.tile(row, (1, kb))[:, :kb * (period - 1)].reshape(nh, kb, period - 1)[:, :, :qb]
        kk = off * kb + np.arange(kb)[:, None]
        visible = (kk // CHUNK) <= (qq // CHUNK)
        tiles.append(jnp.where(visible[None], skew, -jnp.inf))
    far = rel_bias[_t5_bucket(jnp.full((1,), -(seq - 1), jnp.int32))][0].astype(F32)
    return jnp.stack(tiles, axis=1), far


def _attn_kernel(far_ref, q1_ref, q2_ref, k1_ref, k2_ref, v_ref, bias_ref, lam_ref, g_ref, o_ref,
                 vt_ref, acc1_ref, acc2_ref, sa1_ref, sa2_ref, sb1_ref, sb2_ref, *, qb, kb, dv, lam_init):
    h = pl.program_id(1)
    i = pl.program_id(2)
    far = far_ref[h] * math.log2(math.e)
    nt_dims = (((1,), (1,)), ((), ()))
    per_tile = qb // kb
    pad_rows = vt_ref.shape[1] - dv
    s_a = (sa1_ref, sa2_ref)
    s_b = (sb1_ref, sb2_ref)

    @pl.when(i == 0)
    def _():
        rr = lax.broadcasted_iota(jnp.int32, (dv, dv), 0)
        cc = lax.broadcasted_iota(jnp.int32, (dv, dv), 1)
        eye = jnp.where(rr == cc, 1.0, 0.0).astype(BF16)
        ones_row = jnp.where(lax.broadcasted_iota(jnp.int32, (pad_rows, kb), 0) == 0, 1.0, 0.0).astype(BF16)

        def tr_body(j, carry):
            vj = v_ref[pl.ds(pl.multiple_of(j * kb, kb), kb), :]
            vt_ref[j, pl.ds(0, dv), :] = vj.T
            vt_ref[j, pl.ds(dv, pad_rows), :] = ones_row
            return carry

        lax.fori_loop(0, vt_ref.shape[0], tr_body, 0)

    acc1_ref[...] = jnp.zeros(acc1_ref.shape, F32)
    acc2_ref[...] = jnp.zeros(acc2_ref.shape, F32)

    def scores(j):
        ks = pl.ds(pl.multiple_of(j * kb, kb), kb)
        return (lax.dot_general(k1_ref[ks, :], q1_ref[...], nt_dims, preferred_element_type=F32),
                lax.dot_general(k2_ref[ks, :], q2_ref[...], nt_dims, preferred_element_type=F32))

    def softmax_step(s, m, bias, shift):
        if bias is not None:
            s = s + bias
        m_new = jnp.maximum(m, jnp.max(s, axis=0, keepdims=True) + shift)
        return m_new, jnp.exp2(m - m_new), jnp.exp2(s - (m_new - shift)).astype(BF16)

    def step(carry, j, src, dst, bias, shift):
        m1, m2 = carry
        if dst is not None:
            dst[0][...], dst[1][...] = scores(j + 1)
        m1, alpha1, p1 = softmax_step(src[0][...], m1, bias, shift)
        pv1 = jnp.dot(vt_ref[j], p1, preferred_element_type=F32)
        m2, alpha2, p2 = softmax_step(src[1][...], m2, bias, shift)
        pv2 = jnp.dot(vt_ref[j], p2, preferred_element_type=F32)
        acc1_ref[...] = acc1_ref[...] * alpha1 + pv1
        acc2_ref[...] = acc2_ref[...] * alpha2 + pv2
        return m1, m2

    def pair(carry, j, kinds, final=False):
        (bias_a, shift_a), (bias_b, shift_b) = kinds
        carry = step(carry, j, s_a, s_b, bias_a, shift_a)
        return step(carry, j + 1, s_b, None if final else s_a, bias_b, shift_b)

    assert per_tile == 2
    far_kind = (None, far)
    neg = jnp.full((1, qb), -jnp.inf, F32)
    s_a[0][...], s_a[1][...] = scores(0)
    n_ff = jnp.maximum(i - 1, 0)
    carry = lax.fori_loop(0, n_ff, lambda t, c: pair(c, 2 * t, (far_kind, far_kind)), (neg, neg))
    carry = lax.fori_loop(n_ff, i, lambda t, c: pair(c, 2 * t, (far_kind, (bias_ref[0], 0.0))), carry)
    pair(carry, 2 * i, ((bias_ref[1], 0.0), (bias_ref[2], 0.0)), final=True)

    lamv = lam_ref[...]
    lam = (jnp.exp(jnp.sum(lamv[0:1] * lamv[1:2], axis=-1, keepdims=True))
           - jnp.exp(jnp.sum(lamv[2:3] * lamv[3:4], axis=-1, keepdims=True)) + lam_init)
    a1 = acc1_ref[...]
    a2 = acc2_ref[...]
    o = (a1[:dv] / a1[dv:dv + 1] - lam * (a2[:dv] / a2[dv:dv + 1])).T
    y = o * lax.rsqrt(jnp.mean(o * o, axis=-1, keepdims=True) + SUBLN_EPS) * g_ref[...]
    o_ref[...] = (y * (1.0 - lam_init)).astype(o_ref.dtype)


def _attention(proj, bias_tiles, far, lamv, subln_g, batch, seq, lam_init):
    qb, kb = ATT_Q_BLOCK, ATT_KV_BLOCK
    nq = seq // qb
    nh = N_ATT_HEADS
    dk, dv = ATT_HEAD_DIM, ATT_V_DIM
    t = batch * seq
    k_off = 2 * nh
    v_off = 4 * nh * dk // dv
    kern = functools.partial(_attn_kernel, qb=qb, kb=kb, dv=dv, lam_init=lam_init)
    return pl.pallas_call(
        kern,
        grid=(batch, nh, nq),
        in_specs=[pl.BlockSpec(memory_space=pltpu.SMEM),
                  pl.BlockSpec((qb, dk), lambda b, h, i: (b * nq + i, h)),
                  pl.BlockSpec((qb, dk), lambda b, h, i: (b * nq + i, nh + h)),
                  pl.BlockSpec((seq, dk), lambda b, h, i: (b, k_off + h)),
                  pl.BlockSpec((seq, dk), lambda b, h, i: (b, k_off + nh + h)),
                  pl.BlockSpec((seq, dv), lambda b, h, i: (b, v_off + h)),
                  pl.BlockSpec((None, qb // kb + 1, kb, qb), lambda b, h, i: (h, 0, 0, 0)),
                  pl.BlockSpec((4, dk), lambda b, h, i: (0, 0)),
                  pl.BlockSpec((1, dv), lambda b, h, i: (0, 0))],
        out_specs=pl.BlockSpec((qb, dv), lambda b, h, i: (b * nq + i, h)),
        out_shape=jax.ShapeDtypeStruct((t, nh * dv), BF16),
        scratch_shapes=[pltpu.VMEM((seq // kb, dv + BF16_ROWS, kb), BF16),
                        pltpu.VMEM((dv + BF16_ROWS, qb), F32), pltpu.VMEM((dv + BF16_ROWS, qb), F32)]
        + [pltpu.VMEM((kb, qb), F32)] * 4,
        compiler_params=_params(("parallel", "parallel", "arbitrary")),
        name="diff_attention",
    )(far, proj, proj, proj, proj, proj, bias_tiles, lamv, subln_g.reshape(1, dv))


def _conv_silu(x_ref, hist_ref, w_ref, b_ref, blk):
    x = x_ref[...].astype(F32)
    hist_ref[pl.ds(SUBLANES, blk), :] = x
    w = w_ref[...]
    acc = x * w[SSD_CONV - 1:SSD_CONV] + b_ref[...]
    for j in range(1, SSD_CONV):
        acc = acc + hist_ref[pl.ds(SUBLANES - j, blk), :] * w[SSD_CONV - 1 - j:SSD_CONV - j]
    hist_ref[pl.ds(0, SUBLANES), :] = x[blk - SUBLANES:]
    return _silu(acc)


def _ssd_kernel(xs_ref, bm_ref, cm_ref, z_ref, dt_ref, e_ref, cwx_ref, cwb_ref, cwc_ref, cbx_ref, cbb_ref,
                cbc_ref, dtb_ref, alog_ref, dskip_ref, ng_ref, o_ref,
                state_ref, tx_ref, tb_ref, tc_ref, act_ref, *, blk):
    g = pl.program_id(1)

    @pl.when(pl.program_id(2) == 0)
    def _():
        state_ref[...] = jnp.zeros(state_ref.shape, F32)
        for hist_ref in (tx_ref, tb_ref, tc_ref):
            hist_ref[pl.ds(0, SUBLANES), :] = jnp.zeros((SUBLANES, hist_ref.shape[1]), F32)

    xs = _conv_silu(xs_ref, tx_ref, cwx_ref, cbx_ref, blk)
    bm = _conv_silu(bm_ref, tb_ref, cwb_ref, cbb_ref, blk)
    cm = _conv_silu(cm_ref, tc_ref, cwc_ref, cbc_ref, blk)

    dt = jax.nn.softplus(dt_ref[...] + dtb_ref[...])
    da = dt * (-jnp.exp(alog_ref[...]))
    row = lax.broadcasted_iota(jnp.int32, (blk, blk), 0)
    col = lax.broadcasted_iota(jnp.int32, (blk, blk), 1)
    causal = col <= row
    tri = jnp.where(causal, 1.0, 0.0).astype(BF16)
    acum = sum(jnp.dot(tri, part, preferred_element_type=F32) for part in _split3(da))
    both_e = jnp.dot(jnp.concatenate([jnp.concatenate(_split3(dt), axis=1),
                                      jnp.concatenate(_split3(acum), axis=1)], axis=0),
                     e_ref[...], preferred_element_type=F32)
    dt_e = both_e[:blk]
    acum_e = both_e[blk:]
    act_ref[...] = acum.T

    xdt = xs * dt_e
    xdt_b = xdt.astype(BF16)
    cm_b = cm.astype(BF16)
    cb = lax.dot_general(cm_b, bm.astype(BF16), (((1,), (1,)), ((), ())), preferred_element_type=F32)
    lane = lax.broadcasted_iota(jnp.int32, (blk, LANES), 1)
    halves = (lane < SSD_HEAD_DIM, lane >= SSD_HEAD_DIM)
    pieces = []
    for pair in range(GROUP_W // LANES):
        xp = xdt_b[:, pair * LANES:(pair + 1) * LANES]
        yp = jnp.zeros((blk, LANES), F32)
        for hh in range(2):
            r = 2 * pair + hh
            a_col = acum_e[:, r * SSD_HEAD_DIM:r * SSD_HEAD_DIM + 1]
            a_row = act_ref[pl.ds(g * SSD_HEADS_PER_GROUP + r, 1), :]
            decay = jnp.exp(jnp.where(causal, a_col - a_row, -jnp.inf))
            mat = (cb * decay).astype(BF16)
            yp = yp + jnp.dot(mat, jnp.where(halves[hh], xp, jnp.zeros_like(xp)), preferred_element_type=F32)
        pieces.append(yp)
    y = jnp.concatenate(pieces, axis=1)

    st = state_ref[...]
    y = y + jnp.exp(acum_e) * jnp.dot(cm_b, st.astype(BF16), preferred_element_type=F32)
    a_last = acum_e[blk - 1:blk, :]
    wgt = (xdt * jnp.exp(a_last - acum_e)).astype(BF16)
    state_ref[...] = st * jnp.exp(a_last) + jnp.dot(bm.T.astype(BF16), wgt, preferred_element_type=F32)

    y = y + dskip_ref[...] * xs
    y = y * _silu(z_ref[...].astype(F32))
    y = y * lax.rsqrt(jnp.mean(y * y, axis=-1, keepdims=True) + SUBLN_EPS)
    o_ref[...] = (y * ng_ref[...]).astype(o_ref.dtype)


def _ssd(proj, dt_raw, expand, conv_w, conv_b, dt_bias, a_log, d_skip, norm_g, batch, seq, z_col, xbc_col):
    blk = SSD_BLOCK
    nc = seq // blk
    ng = SSD_GROUPS
    t = batch * seq
    width = ng * GROUP_W
    heads = ng * SSD_HEADS_PER_GROUP
    z_blk = z_col // GROUP_W
    xs_blk = xbc_col // GROUP_W
    b_blk = (xbc_col + width) // SSD_STATE
    c_blk = b_blk + ng
    cw_b_blk = width // SSD_STATE
    pad = LANES - heads
    row = lambda b, g, c: b * nc + c
    dtb = jnp.pad(dt_bias, (0, pad)).reshape(1, LANES)
    alog = jnp.pad(a_log, (0, pad)).reshape(1, LANES)
    dskip = jnp.repeat(d_skip, SSD_HEAD_DIM).reshape(1, width)
    cb2 = conv_b.reshape(1, -1)
    kern = functools.partial(_ssd_kernel, blk=blk)
    return pl.pallas_call(
        kern,
        grid=(batch, ng, nc),
        in_specs=[pl.BlockSpec((blk, GROUP_W), lambda b, g, c: (row(b, g, c), xs_blk + g)),
                  pl.BlockSpec((blk, SSD_STATE), lambda b, g, c: (row(b, g, c), b_blk + g)),
                  pl.BlockSpec((blk, SSD_STATE), lambda b, g, c: (row(b, g, c), c_blk + g)),
                  pl.BlockSpec((blk, GROUP_W), lambda b, g, c: (row(b, g, c), z_blk + g)),
                  pl.BlockSpec((blk, LANES), lambda b, g, c: (row(b, g, c), 0)),
                  pl.BlockSpec((None, 3 * LANES, GROUP_W), lambda b, g, c: (g, 0, 0)),
                  pl.BlockSpec((SSD_CONV, GROUP_W), lambda b, g, c: (0, g)),
                  pl.BlockSpec((SSD_CONV, SSD_STATE), lambda b, g, c: (0, cw_b_blk + g)),
                  pl.BlockSpec((SSD_CONV, SSD_STATE), lambda b, g, c: (0, cw_b_blk + ng + g)),
                  pl.BlockSpec((1, GROUP_W), lambda b, g, c: (0, g)),
                  pl.BlockSpec((1, SSD_STATE), lambda b, g, c: (0, cw_b_blk + g)),
                  pl.BlockSpec((1, SSD_STATE), lambda b, g, c: (0, cw_b_blk + ng + g)),
                  pl.BlockSpec((1, LANES), lambda b, g, c: (0, 0)),
                  pl.BlockSpec((1, LANES), lambda b, g, c: (0, 0)),
                  pl.BlockSpec((1, GROUP_W), lambda b, g, c: (0, g)),
                  pl.BlockSpec((1, GROUP_W), lambda b, g, c: (0, g))],
        out_specs=pl.BlockSpec((blk, GROUP_W), lambda b, g, c: (row(b, g, c), g)),
        out_shape=jax.ShapeDtypeStruct((t, width), BF16),
        scratch_shapes=[pltpu.VMEM((SSD_STATE, GROUP_W), F32),
                        pltpu.VMEM((SUBLANES + blk, GROUP_W), F32),
                        pltpu.VMEM((SUBLANES + blk, SSD_STATE), F32),
                        pltpu.VMEM((SUBLANES + blk, SSD_STATE), F32),
                        pltpu.VMEM((LANES, blk), F32)],
        compiler_params=_params(("parallel", "parallel", "arbitrary")),
        name="ssd",
    )(proj, proj, proj, proj, dt_raw, expand, conv_w, conv_w, conv_w, cb2, cb2, cb2, dtb, alog, dskip,
      norm_g.reshape(1, width))


def _head_expand():
    e = np.zeros((SSD_GROUPS, LANES, GROUP_W), np.float32)
    for g in range(SSD_GROUPS):
        for r in range(SSD_HEADS_PER_GROUP):
            e[g, g * SSD_HEADS_PER_GROUP + r, r * SSD_HEAD_DIM:(r + 1) * SSD_HEAD_DIM] = 1.0
    return jnp.asarray(np.tile(e, (1, 3, 1)), dtype=BF16)


def _merge_kernel(oa_ref, os_ref, wa_ref, ws1_ref, ws2_ref, ga_ref, gs_ref, o_ref):
    ka = wa_ref.shape[0]
    ya = jnp.dot(oa_ref[...], wa_ref[...], preferred_element_type=F32)
    ys = (jnp.dot(os_ref[:, pl.ds(0, ka)], ws1_ref[...], preferred_element_type=F32)
          + jnp.dot(os_ref[:, pl.ds(ka, ka)], ws2_ref[...], preferred_element_type=F32))
    merged = jax.nn.sigmoid(ga_ref[...].astype(F32)) * ya + jax.nn.sigmoid(gs_ref[...].astype(F32)) * ys
    o_ref[...] = merged.astype(o_ref.dtype)


def _merge(o_att, o_ssd, gates, w_branch_all, layer, tm, tn):
    t, ka = o_att.shape
    ks = o_ssd.shape[1]
    d = w_branch_all.shape[2]
    assert ks == 2 * ka
    w_spec = lambda blk: pl.BlockSpec((None, ka, tn), lambda i, j: (layer, blk, j))
    gs_blk = d // tn
    return pl.pallas_call(
        _merge_kernel,
        grid=(t // tm, d // tn),
        in_specs=[pl.BlockSpec((tm, ka), lambda i, j: (i, 0)),
                  pl.BlockSpec((tm, ks), lambda i, j: (i, 0)),
                  w_spec(0), w_spec(1), w_spec(2),
                  pl.BlockSpec((tm, tn), lambda i, j: (i, j)),
                  pl.BlockSpec((tm, tn), lambda i, j: (i, gs_blk + j))],
        out_specs=pl.BlockSpec((tm, tn), lambda i, j: (i, j)),
        out_shape=jax.ShapeDtypeStruct((t, d), BF16),
        compiler_params=_params(("parallel", "arbitrary")),
        name="branch_merge",
    )(o_att, o_ssd, w_branch_all, w_branch_all, w_branch_all, gates, gates)


def _pack_bf16_pair(v):
    n = v.shape[1] // 2
    lo = lax.bitcast_convert_type(v[:, :n].astype(BF16).astype(F32), jnp.uint32)
    hi = lax.bitcast_convert_type(v[:, n:].astype(BF16).astype(F32), jnp.uint32)
    return (lo >> 16) | (hi & jnp.uint32(0xFFFF0000))


def _unpack_bf16_pair(p):
    lo = lax.bitcast_convert_type(p << 16, F32)
    hi = lax.bitcast_convert_type(p & jnp.uint32(0xFFFF0000), F32)
    return lo, hi


def _wo_kernel(m_ref, x_ref, w_ref, g1_ref, ng_ref, sc_ref, sh_ref, rw_ref, rb_ref, xo_ref, hp_ref, lg_ref):
    y = jnp.dot(m_ref[...], w_ref[...], preferred_element_type=F32)
    x = x_ref[...] + g1_ref[...] * y
    xo_ref[...] = x
    h2 = _modulated_norm(x, ng_ref[...], sc_ref[...], sh_ref[...])
    hp_ref[...] = _pack_bf16_pair(h2)
    h_hi = h2.astype(BF16)
    h_lo = (h2 - h_hi.astype(F32)).astype(BF16)
    both = jnp.dot(h_hi, rw_ref[...], preferred_element_type=F32)
    cross = jnp.dot(h_lo, rw_ref[:, pl.ds(0, LANES)], preferred_element_type=F32)
    lg_ref[...] = both[:, :LANES] + (both[:, LANES:] + cross) + rb_ref[...]


def _wo_residual(merged, xf, w_o_all, layer, mod_l, norm2_g, router_w, router_b, seq, tm):
    t, d = xf.shape
    ne = router_w.shape[1]
    tpb = seq // tm
    mod_spec = lambda idx: pl.BlockSpec((None, 1, d), lambda i: (i // tpb, 0, idx))
    rw = jnp.pad(router_w, ((0, 0), (0, LANES - ne)))
    rw_hi = rw.astype(BF16)
    rw = jnp.concatenate([rw_hi, (rw - rw_hi.astype(F32)).astype(BF16)], axis=1)
    rb = jnp.pad(router_b, (0, LANES - ne), constant_values=-jnp.inf).reshape(1, LANES)
    return pl.pallas_call(
        _wo_kernel,
        grid=(t // tm,),
        in_specs=[pl.BlockSpec((tm, d), lambda i: (i, 0)),
                  pl.BlockSpec((tm, d), lambda i: (i, 0)),
                  pl.BlockSpec((None, d, d), lambda i: (layer, 0, 0)),
                  mod_spec(2),
                  pl.BlockSpec((1, d), lambda i: (0, 0)),
                  mod_spec(4), mod_spec(3),
                  pl.BlockSpec((d, 2 * LANES), lambda i: (0, 0)),
                  pl.BlockSpec((1, LANES), lambda i: (0, 0))],
        out_specs=[pl.BlockSpec((tm, d), lambda i: (i, 0)), pl.BlockSpec((tm, d // 2), lambda i: (i, 0)),
                   pl.BlockSpec((tm, LANES), lambda i: (i, 0))],
        out_shape=[jax.ShapeDtypeStruct((t, d), F32), jax.ShapeDtypeStruct((t, d // 2), jnp.uint32),
                   jax.ShapeDtypeStruct((t, LANES), F32)],
        compiler_params=_params(("parallel",)),
        name="wo_residual",
    )(merged, xf, w_o_all, mod_l, norm2_g.reshape(1, d), mod_l, mod_l, rw, rb)


def _router_kernel(lg_ref, idx_ref, gate_ref, rank_ref, cnt_ref, run_ref, *, tm):
    @pl.when(pl.program_id(0) == 0)
    def _():
        run_ref[...] = jnp.zeros(run_ref.shape, F32)

    logits = lg_ref[...]
    lane = lax.broadcasted_iota(jnp.int32, (tm, LANES), 1)
    vals = logits
    picked = jnp.zeros((tm, LANES), F32)
    top_v, top_sel, top_i = [], [], []
    for _ in range(TOP_K):
        m = jnp.max(vals, axis=-1, keepdims=True)
        idx = jnp.min(jnp.where(vals == m, lane, LANES), axis=-1, keepdims=True)
        sel = lane == idx
        top_v.append(m)
        top_i.append(idx)
        top_sel.append(sel)
        vals = jnp.where(sel, -jnp.inf, vals)
        picked = picked + sel.astype(F32)

    row = lax.broadcasted_iota(jnp.int32, (tm, tm), 0)
    col = lax.broadcasted_iota(jnp.int32, (tm, tm), 1)
    before = jnp.dot((col < row).astype(BF16), picked.astype(BF16), preferred_element_type=F32) + run_ref[...]
    run_ref[...] = run_ref[...] + jnp.sum(picked, axis=0, keepdims=True)
    cnt_ref[...] = run_ref[...]

    exps = [jnp.exp(v - top_v[0]) for v in top_v]
    denom = exps[0] + exps[1] + exps[2] + exps[3]
    idx_out = jnp.zeros((tm, LANES), jnp.int32)
    rank_out = jnp.zeros((tm, LANES), jnp.int32)
    gate_out = jnp.zeros((tm, LANES), F32)
    for k in range(TOP_K):
        rank_k = jnp.sum(jnp.where(top_sel[k], before, 0.0), axis=-1, keepdims=True).astype(jnp.int32)
        idx_out = jnp.where(lane == k, top_i[k], idx_out)
        rank_out = jnp.where(lane == k, rank_k, rank_out)
        gate_out = jnp.where(lane == k, exps[k] / denom, gate_out)
    idx_ref[...] = idx_out
    rank_ref[...] = rank_out
    gate_ref[...] = gate_out


def _router(logits, tm):
    t = logits.shape[0]
    kern = functools.partial(_router_kernel, tm=tm)
    tok_spec = pl.BlockSpec((tm, LANES), lambda i: (i, 0))
    return pl.pallas_call(
        kern,
        grid=(t // tm,),
        in_specs=[tok_spec],
        out_specs=[tok_spec, tok_spec, tok_spec, pl.BlockSpec((1, LANES), lambda i: (0, 0))],
        out_shape=[jax.ShapeDtypeStruct((t, LANES), jnp.int32), jax.ShapeDtypeStruct((t, LANES), F32),
                   jax.ShapeDtypeStruct((t, LANES), jnp.int32), jax.ShapeDtypeStruct((1, LANES), F32)],
        scratch_shapes=[pltpu.VMEM((1, LANES), F32)],
        compiler_params=_params(("arbitrary",)),
        name="router",
    )(logits)


def _dispatch_kernel(dest_ref, h_ref, xs_in_hbm, xs_hbm, sem, *, tm):
    del xs_in_hbm

    def row_copy(t, k):
        return pltpu.make_async_copy(h_ref.at[pl.ds(t, 1)],
                                     xs_hbm.at[pl.ds(dest_ref[t * TOP_K + k], 1)], sem)

    for t in range(tm):
        for k in range(TOP_K):
            row_copy(t, k).start()
    for _ in range(TOP_K):
        pltpu.make_async_copy(h_ref, xs_hbm.at[pl.ds(0, tm)], sem).wait()


def _dispatch(h2, dest_flat, n_rows, tm, fill):
    t, d = h2.shape
    assert fill.shape == (n_rows, d) and fill.dtype == h2.dtype
    kern = functools.partial(_dispatch_kernel, tm=tm)
    return pl.pallas_call(
        kern,
        grid=(t // tm,),
        in_specs=[pl.BlockSpec((tm * TOP_K,), lambda i: (i,), memory_space=pltpu.SMEM),
                  pl.BlockSpec((tm, d), lambda i: (i, 0)),
                  pl.BlockSpec(memory_space=pl.ANY)],
        out_specs=pl.BlockSpec(memory_space=pl.ANY),
        out_shape=jax.ShapeDtypeStruct((n_rows, d), h2.dtype),
        scratch_shapes=[pltpu.SemaphoreType.DMA(())],
        input_output_aliases={2: 0},
        compiler_params=_params(("arbitrary",)),
        name="moe_dispatch",
    )(dest_flat, h2, fill)


def _expert_gu_kernel(be_ref, nv_ref, x_ref, w_ref, b_ref, o_ref, *, ff):
    del be_ref
    valid = pl.program_id(0) < nv_ref[0]

    @pl.when(valid)
    def _():
        x_lo, x_hi = _unpack_bf16_pair(x_ref[...])
        half = x_lo.shape[1]
        gu = (jnp.dot(x_lo.astype(BF16), w_ref[pl.ds(0, half), :], preferred_element_type=F32)
              + jnp.dot(x_hi.astype(BF16), w_ref[pl.ds(half, half), :], preferred_element_type=F32) + b_ref[...])
        g = jnp.minimum(gu[:, :ff], SWIGLU_LIMIT)
        u = jnp.clip(gu[:, ff:], -SWIGLU_LIMIT, SWIGLU_LIMIT)
        o_ref[...] = ((u + 1.0) * (g * jax.nn.sigmoid(SWIGLU_ALPHA * g))).astype(o_ref.dtype)

    @pl.when(jnp.logical_not(valid))
    def _():
        o_ref[...] = jnp.zeros(o_ref.shape, o_ref.dtype)


def _expert_down_kernel(be_ref, nv_ref, a_ref, w_ref, b_ref, o_ref, wb_ref):
    i = pl.program_id(0)
    valid = i < nv_ref[0]
    new_expert = jnp.logical_or(i == 0, be_ref[i] != be_ref[jnp.maximum(i - 1, 0)])

    @pl.when(jnp.logical_and(valid, new_expert))
    def _():
        wb_ref[...] = w_ref[...].astype(BF16)

    @pl.when(valid)
    def _():
        o_ref[...] = _pack_bf16_pair(jnp.dot(a_ref[...], wb_ref[...], preferred_element_type=F32) + b_ref[...])

    @pl.when(jnp.logical_not(valid))
    def _():
        o_ref[...] = jnp.zeros(o_ref.shape, o_ref.dtype)


def _expert_ffn(xs, block_e, n_valid, w_gu, b_gu, w_dn, b_dn, layer):
    n_rows, dp = xs.shape
    bm = EXPERT_BLOCK
    n_blocks = n_rows // bm
    nl, ne, d, ff2 = w_gu.shape
    ff = ff2 // 2
    act = pl.pallas_call(
        functools.partial(_expert_gu_kernel, ff=ff),
        grid_spec=pltpu.PrefetchScalarGridSpec(
            num_scalar_prefetch=2, grid=(n_blocks,),
            in_specs=[pl.BlockSpec((bm, dp), lambda i, be, nv: (i, 0)),
                      pl.BlockSpec((None, None, d, ff2), lambda i, be, nv: (layer, be[i], 0, 0)),
                      pl.BlockSpec((None, None, 1, ff2), lambda i, be, nv: (layer, be[i], 0, 0))],
            out_specs=pl.BlockSpec((bm, ff), lambda i, be, nv: (i, 0))),
        out_shape=jax.ShapeDtypeStruct((n_rows, ff), BF16),
        compiler_params=_params(("arbitrary",)),
        name="expert_gate_up",
    )(block_e, n_valid, xs, w_gu, b_gu.reshape(nl, ne, 1, ff2))
    return pl.pallas_call(
        _expert_down_kernel,
        grid_spec=pltpu.PrefetchScalarGridSpec(
            num_scalar_prefetch=2, grid=(n_blocks,),
            in_specs=[pl.BlockSpec((bm, ff), lambda i, be, nv: (i, 0)),
                      pl.BlockSpec((None, None, ff, d), lambda i, be, nv: (layer, be[i], 0, 0)),
                      pl.BlockSpec((None, None, 1, d), lambda i, be, nv: (layer, be[i], 0, 0))],
            out_specs=pl.BlockSpec((bm, dp), lambda i, be, nv: (i, 0)),
            scratch_shapes=[pltpu.VMEM((ff, d), BF16)]),
        out_shape=jax.ShapeDtypeStruct((n_rows, dp), jnp.uint32),
        compiler_params=_params(("arbitrary",)),
        name="expert_down",
    )(block_e, n_valid, act, w_dn, b_dn.reshape(nl, ne, 1, d))


def _combine_kernel(dest_ref, dest_next_ref, ys_hbm, x_ref, gate_ref, g2_ref, ng_ref, *rest, half, final):
    if final:
        o_ref, buf_a, buf_b, sem_a, sem_b = rest
    else:
        sc_ref, sh_ref, o_ref, h_ref, buf_a, buf_b, sem_a, sem_b = rest
    i = pl.program_id(0)

    def start_rows(dref, base, buf, sem):
        for t in range(half):
            for k in range(TOP_K):
                pltpu.make_async_copy(ys_hbm.at[pl.ds(dref[base + t * TOP_K + k], 1)],
                                      buf.at[k, pl.ds(t, 1)], sem).start()

    def wait_rows(buf, sem):
        for k in range(TOP_K):
            pltpu.make_async_copy(ys_hbm.at[pl.ds(0, half)], buf.at[k], sem).wait()

    def finish(buf, rows):
        gates = gate_ref[rows, :]
        moe_lo, moe_hi = (gates[:, 0:1] * part for part in _unpack_bf16_pair(buf[0]))
        for k in range(1, TOP_K):
            lo, hi = _unpack_bf16_pair(buf[k])
            moe_lo = moe_lo + gates[:, k:k + 1] * lo
            moe_hi = moe_hi + gates[:, k:k + 1] * hi
        x = x_ref[rows, :] + g2_ref[...] * jnp.concatenate([moe_lo, moe_hi], axis=1)
        if final:
            o_ref[rows, :] = x * lax.rsqrt(jnp.mean(x * x, axis=-1, keepdims=True) + NORM_EPS) * ng_ref[...]
        else:
            o_ref[rows, :] = x
            h_ref[rows, :] = _modulated_norm(x, ng_ref[...], sc_ref[...], sh_ref[...]).astype(h_ref.dtype)

    @pl.when(i == 0)
    def _():
        start_rows(dest_ref, 0, buf_a, sem_a)

    wait_rows(buf_a, sem_a)
    start_rows(dest_ref, half * TOP_K, buf_b, sem_b)
    finish(buf_a, pl.ds(0, half))
    wait_rows(buf_b, sem_b)
    start_rows(dest_next_ref, 0, buf_a, sem_a)
    finish(buf_b, pl.ds(half, half))

    @pl.when(i == pl.num_programs(0) - 1)
    def _():
        wait_rows(buf_a, sem_a)


def _combine(ys, dest_flat, xf, gates, mod_l, norm_g, mod_next, seq, half):
    t, d = xf.shape
    final = mod_next is None
    tm = 2 * half
    steps = t // tm
    tpb = seq // tm
    kern = functools.partial(_combine_kernel, half=half, final=final)
    tile = pl.BlockSpec((tm, d), lambda i: (i, 0))
    mod_spec = lambda idx: pl.BlockSpec((None, 1, d), lambda i: (i // tpb, 0, idx))
    in_specs = [pl.BlockSpec((tm * TOP_K,), lambda i: (i,), memory_space=pltpu.SMEM),
                pl.BlockSpec((tm * TOP_K,), lambda i: (jnp.minimum(i + 1, steps - 1),), memory_space=pltpu.SMEM),
                pl.BlockSpec(memory_space=pl.ANY),
                tile,
                pl.BlockSpec((tm, LANES), lambda i: (i, 0)),
                mod_spec(5),
                pl.BlockSpec((1, d), lambda i: (0, 0))]
    args = [dest_flat, dest_flat, ys, xf, gates, mod_l, norm_g.reshape(1, d)]
    if final:
        out_specs, out_shape = tile, jax.ShapeDtypeStruct((t, d), F32)
    else:
        in_specs += [mod_spec(1), mod_spec(0)]
        args += [mod_next, mod_next]
        out_specs = [tile, tile]
        out_shape = [jax.ShapeDtypeStruct((t, d), F32), jax.ShapeDtypeStruct((t, d), BF16)]
    return pl.pallas_call(
        kern,
        grid=(steps,),
        in_specs=in_specs,
        out_specs=out_specs,
        out_shape=out_shape,
        scratch_shapes=[pltpu.VMEM((TOP_K, half, d // 2), jnp.uint32), pltpu.VMEM((TOP_K, half, d // 2), jnp.uint32),
                        pltpu.SemaphoreType.DMA(()), pltpu.SemaphoreType.DMA(())],
        compiler_params=_params(("arbitrary",)),
        name="moe_combine",
    )(*args)


def _moe(h2p, logits, xf, mod_l, w_gu, b_gu, w_dn, b_dn, layer, fill, norm_g, mod_next, seq):
    t = h2p.shape[0]
    ne = w_gu.shape[1]
    bm = EXPERT_BLOCK
    top_i, gates, rank, counts = _router(logits, tm=256)
    cnt = counts[0, :ne].astype(jnp.int32)
    padded = (cnt + bm - 1) // bm * bm
    pad_end = jnp.cumsum(padded)
    pad_start = pad_end - padded
    experts = jnp.arange(ne, dtype=jnp.int32)
    start_of = jnp.sum(jnp.where(top_i[:, :TOP_K, None] == experts, pad_start, 0), axis=-1)
    dest = start_of + rank[:, :TOP_K]
    n_blocks = t * TOP_K // bm + ne
    blk_row = jnp.arange(n_blocks, dtype=jnp.int32)[:, None] * bm
    block_e = jnp.minimum(jnp.sum((pad_end[None, :] <= blk_row).astype(jnp.int32), axis=1), ne - 1)
    n_valid = (pad_end[-1:] // bm).astype(jnp.int32)
    dest_flat = dest.reshape(-1).astype(jnp.int32)

    n_rows = n_blocks * bm
    if fill is None:
        fill = jnp.zeros((n_rows, h2p.shape[1]), h2p.dtype)
    xs = _dispatch(h2p, dest_flat, n_rows, 256, fill)
    ys = _expert_ffn(xs, block_e, n_valid, w_gu, b_gu, w_dn, b_dn, layer)
    return _combine(ys, dest_flat, xf, gates, mod_l, norm_g, mod_next, seq, half=128), ys


def kernel(x, c, ada_w, ada_b, norm1_g, w_in, rel_bias, lam_q1, lam_k1, lam_q2, lam_k2, attn_subln_g, conv_w, conv_b, dt_bias, a_log, d_skip, ssd_norm_g, w_branch, w_o, norm2_g, router_w, router_b, w_gate_up, b_gate_up, w_down, b_down, final_g):
    batch, seq, d = x.shape
    t = batch * seq
    depth = ada_w.shape[0]
    att_w = N_ATT_HEADS * ATT_V_DIM
    ssd_w = SSD_GROUPS * GROUP_W
    n_heads_ssd = SSD_GROUPS * SSD_HEADS_PER_GROUP
    conv_ch = conv_w.shape[2]
    q_cols = 2 * N_ATT_HEADS * ATT_HEAD_DIM
    q_scale = ATT_HEAD_DIM ** -0.5 * math.log2(math.e)
    z_col = 2 * q_cols + att_w
    xbc_col = z_col + ssd_w
    dt_col = xbc_col + conv_ch
    col_scale = jnp.concatenate([jnp.full((q_cols,), q_scale, F32), jnp.ones((dt_col - q_cols,), F32)])

    mod = _adaln(c, ada_w, ada_b)
    bias_tiles, far = _attn_bias_tiles(rel_bias, seq, ATT_Q_BLOCK, ATT_KV_BLOCK)
    expand = _head_expand()
    xf = x.reshape(t, d)
    tm_big = min(1024, seq)

    w_main_all = (w_in[:, :, :dt_col] * col_scale).astype(BF16)
    w_gate_all = w_in[:, :, dt_col + n_heads_ssd:].astype(BF16)
    w_dt_all = jnp.pad(w_in[:, :, dt_col:dt_col + n_heads_ssd],
                       ((0, 0), (0, 0), (0, LANES - n_heads_ssd))).astype(BF16)
    w_branch_all = w_branch.astype(BF16)
    w_o_all = w_o.astype(BF16)
    w_gate_up_all = w_gate_up.astype(BF16)

    mods = [mod[layer].reshape(batch, 1, 6 * d) for layer in range(depth)]
    h = _prenorm(xf, norm1_g[0], mods[0], 1, 0, seq, tm=tm_big)
    ys_prev = None
    for layer in range(depth):
        mod_l = mods[layer]
        proj = _inproj(h, w_main_all, layer, BF16, tm=tm_big, tn=2048)
        gates = _inproj(h, w_gate_all, layer, BF16, tm=tm_big, tn=2048)
        dt_raw = _inproj(h, w_dt_all, layer, F32, tm=tm_big, tn=LANES)

        lam_init = 0.8 - 0.6 * math.exp(-0.3 * layer)
        lamv = jnp.stack([lam_q1[layer], lam_k1[layer], lam_q2[layer], lam_k2[layer]], axis=0)
        o_att = _attention(proj, bias_tiles, far, lamv, attn_subln_g[layer], batch, seq, lam_init)
        o_ssd = _ssd(proj, dt_raw, expand, conv_w[layer], conv_b[layer], dt_bias[layer], a_log[layer],
                     d_skip[layer], ssd_norm_g[layer], batch, seq, z_col, xbc_col)

        merged = _merge(o_att, o_ssd, gates, w_branch_all, layer, tm=tm_big, tn=512)
        xf, h2p, logits = _wo_residual(merged, xf, w_o_all, layer, mod_l, norm2_g[layer],
                                       router_w[layer], router_b[layer], seq, tm=512)

        moe = functools.partial(_moe, h2p, logits, xf, mod_l, w_gate_up_all, b_gate_up, w_down, b_down, layer, ys_prev)
        if layer == depth - 1:
            xf, ys_prev = moe(final_g, None, seq)
        else:
            (xf, h), ys_prev = moe(norm1_g[layer + 1], mods[layer + 1], seq)
    return xf.reshape(batch, seq, d)
```
